```python
import math
import jax, jax.numpy as jnp
from jax import lax
import numpy as np

D_MODEL = 2048
BATCH = 4
SEQ = 2048
DEPTH = 1

CHUNK = 64
Q_BLOCK = 128
HEAD_DIM = 128
SB_HEADS = 8
RET_HEADS = 8
SB_WIDTH = SB_HEADS * HEAD_DIM
RET_WIDTH = RET_HEADS * HEAD_DIM
MIX_WIDTH = SB_WIDTH + RET_WIDTH
IN_WIDTH = 3 * SB_WIDTH + 4 * RET_WIDTH
ROPE_BASE = 10000.0
N_EXPERTS = 32
TOP_K = 4
D_FF = D_MODEL
SWIGLU_LIMIT = 7.0
SWIGLU_ALPHA = 1.702
MOE_BLOCK = 128
LN_EPS = 1e-5
GN_EPS = 1e-5
DEEPNORM_ALPHA = (2 * DEPTH) ** 0.25
DEEPNORM_BETA = (8 * DEPTH) ** -0.25

kernel_name = "hybrid_stickbreak_retention_moe_deepnorm"


def layer_norm(x, gain, bias):
    xf = x.astype(jnp.float32)
    mu = jnp.mean(xf, axis=-1, keepdims=True)
    var = jnp.mean(jnp.square(xf - mu), axis=-1, keepdims=True)
    y = (xf - mu) * lax.rsqrt(var + LN_EPS)
    return (y * gain.astype(jnp.float32) + bias.astype(jnp.float32)).astype(x.dtype)


def rotary(x):
    s, d = x.shape[1], x.shape[3]
    inv_freq = ROPE_BASE ** (-jnp.arange(0, d, 2, dtype=jnp.float32) / d)
    ang = jnp.arange(s, dtype=jnp.float32)[:, None] * inv_freq[None, :]
    cos = jnp.cos(ang)[None, :, None, :]
    sin = jnp.sin(ang)[None, :, None, :]
    xf = x.astype(jnp.float32)
    x1, x2 = xf[..., : d // 2], xf[..., d // 2:]
    return jnp.concatenate([x1 * cos - x2 * sin, x1 * sin + x2 * cos], axis=-1).astype(x.dtype)


def stick_breaking_attention(q, k, v):
    b, h, s, d = q.shape
    nq = s // Q_BLOCK
    q_blocks = q.reshape(b, h, nq, Q_BLOCK, d).transpose(2, 0, 1, 3, 4)
    starts = jnp.arange(nq, dtype=jnp.int32) * Q_BLOCK
    kf = k.astype(jnp.float32)
    vf = v.astype(jnp.float32)
    key_pos = jnp.arange(s, dtype=jnp.int32)
    scale = 1.0 / math.sqrt(d)

    def one_block(args):
        q_blk, start = args
        z = jnp.einsum('bhqd,bhkd->bhqk', q_blk.astype(jnp.float32), kf) * scale
        q_pos = start + jnp.arange(Q_BLOCK, dtype=jnp.int32)
        mask = key_pos[None, :] < q_pos[:, None]
        log_not_beta = jnp.where(mask, jax.nn.log_sigmoid(-z), 0.0)
        later = lax.cumsum(log_not_beta, axis=3, reverse=True) - log_not_beta
        weights = jnp.where(mask, jnp.exp(jax.nn.log_sigmoid(z) + later), 0.0)
        return jnp.einsum('bhqk,bhkd->bhqd', weights, vf)

    o = lax.map(one_block, (q_blocks, starts))
    return o.transpose(1, 0, 3, 2, 4).reshape(b, s, h * d).astype(q.dtype)


def chunk_retention(q, k, v, gn_gain):
    b, s, h, d = q.shape
    n = s // CHUNK
    log_gamma = jnp.log1p(-jnp.exp2(-5.0 - jnp.arange(h, dtype=jnp.float32)))
    to_chunks = lambda a: a.astype(jnp.float32).reshape(b, n, CHUNK, h, d).transpose(0, 3, 1, 2, 4)
    qf = to_chunks(q)
    kf = to_chunks(k) * (d ** -0.5)
    vf = to_chunks(v)
    i = jnp.arange(CHUNK, dtype=jnp.float32)
    intra_decay = jnp.exp(log_gamma[:, None, None] * jnp.abs(i[:, None] - i[None, :]))
    scores = jnp.einsum('bhncd,bhnjd->bhncj', qf, kf) * intra_decay[:, None]
    o_intra = jnp.einsum('bhncj,bhnje->bhnce', scores, vf)
    k_decay = jnp.exp(log_gamma[:, None] * (CHUNK - 1 - i))
    chunk_kv = jnp.einsum('bhnjd,bhnje->nbhde', kf * k_decay[:, None, :, None], vf)
    chunk_decay = jnp.exp(log_gamma * CHUNK)[None, :, None, None]

    def step(state, kv):
        return state * chunk_decay + kv, state

    _, prev_states = lax.scan(step, jnp.zeros((b, h, d, d), jnp.float32), chunk_kv)
    q_decay = jnp.exp(log_gamma[:, None] * (i + 1.0))
    o_cross = jnp.einsum('bhncd,nbhde->bhnce', qf * q_decay[:, None, :, None], prev_states)
    o = o_intra + o_cross
    mu = jnp.mean(o, axis=-1, keepdims=True)
    var = jnp.mean(jnp.square(o - mu), axis=-1, keepdims=True)
    o = (o - mu) * lax.rsqrt(var + GN_EPS)
    o = o.transpose(0, 2, 3, 1, 4).reshape(b, s, h * d) * gn_gain.astype(jnp.float32)
    return o.astype(q.dtype)


def clamped_swiglu(hid):
    gate, up = hid[..., :D_FF], hid[..., D_FF:]
    gate = jnp.minimum(gate, SWIGLU_LIMIT)
    up = jnp.clip(up, -SWIGLU_LIMIT, SWIGLU_LIMIT)
    return (up + 1.0) * (gate * jax.nn.sigmoid(SWIGLU_ALPHA * gate))


def moe_ffn(x, w_router, b_router, w_gate_up, b_gate_up, w_down, b_down):
    b, s, dm = x.shape
    t = b * s
    xt = x.reshape(t, dm)
    logits = (xt @ w_router + b_router).astype(jnp.float32)
    top_vals, top_idx = lax.top_k(logits, TOP_K)
    gates = jax.nn.softmax(top_vals, axis=-1)
    tk = t * TOP_K
    flat_e = top_idx.reshape(tk).astype(jnp.int32)
    flat_tok = jnp.arange(tk, dtype=jnp.int32) // TOP_K
    order = jnp.argsort(flat_e)
    sorted_e = flat_e[order]
    sorted_tok = flat_tok[order]
    counts = jnp.bincount(flat_e, length=N_EXPERTS).astype(jnp.int32)
    padded = (counts + MOE_BLOCK - 1) // MOE_BLOCK * MOE_BLOCK
    un_start = jnp.cumsum(counts) - counts
    pad_end = jnp.cumsum(padded)
    pad_start = pad_end - padded
    dest = pad_start[sorted_e] + jnp.arange(tk, dtype=jnp.int32) - un_start[sorted_e]
    n_rows = (tk + MOE_BLOCK - 1) // MOE_BLOCK * MOE_BLOCK + N_EXPERTS * MOE_BLOCK
    n_blocks = n_rows // MOE_BLOCK
    row_tok = jnp.full((n_rows,), t, jnp.int32).at[dest].set(sorted_tok)
    x_pad = jnp.concatenate([xt, jnp.zeros((1, dm), xt.dtype)], axis=0)
    x_rows = x_pad[row_tok].reshape(n_blocks, MOE_BLOCK, dm)
    block_e = jnp.minimum(
        jnp.searchsorted(pad_end, jnp.arange(n_blocks, dtype=jnp.int32) * MOE_BLOCK, side='right'),
        N_EXPERTS - 1)

    def expert_block(args):
        xb, e = args
        hid = xb @ w_gate_up[e] + b_gate_up[e]
        return clamped_swiglu(hid) @ w_down[e] + b_down[e]

    y_rows = lax.map(expert_block, (x_rows, block_e)).reshape(n_rows, dm)
    dest_tok = jnp.zeros((tk,), jnp.int32).at[order].set(dest).reshape(t, TOP_K)
    y = jnp.einsum('tk,tkd->td', gates.astype(y_rows.dtype), y_rows[dest_tok])
    return y.reshape(b, s, dm)


def setup_inputs(seed: int = 0) -> dict:
    key = jax.random.key(seed)
    ks = jax.random.split(key, 16)
    f32 = jnp.float32
    nrm = lambda k, shape: jax.random.normal(k, shape, f32)
    return {
        "x": nrm(ks[0], (BATCH, SEQ, D_MODEL)),
        "w_in": nrm(ks[1], (DEPTH, D_MODEL, IN_WIDTH)) * D_MODEL ** -0.5,
        "ret_gn_gain": 1.0 + 0.02 * nrm(ks[2], (DEPTH, RET_WIDTH)),
        "w_out": nrm(ks[3], (DEPTH, MIX_WIDTH, D_MODEL)) * (MIX_WIDTH ** -0.5 * DEEPNORM_BETA),
        "ln1_gain": 1.0 + 0.02 * nrm(ks[4], (DEPTH, D_MODEL)),
        "ln1_bias": 0.02 * nrm(ks[5], (DEPTH, D_MODEL)),
        "w_router": nrm(ks[6], (DEPTH, D_MODEL, N_EXPERTS)) * D_MODEL ** -0.5,
        "b_router": 0.01 * nrm(ks[7], (DEPTH, N_EXPERTS)),
        "w_gate_up": nrm(ks[8], (DEPTH, N_EXPERTS, D_MODEL, 2 * D_FF)) * D_MODEL ** -0.5,
        "b_gate_up": 0.01 * nrm(ks[9], (DEPTH, N_EXPERTS, 2 * D_FF)),
        "w_down": nrm(ks[10], (DEPTH, N_EXPERTS, D_FF, D_MODEL)) * (D_FF ** -0.5 * DEEPNORM_BETA),
        "b_down": 0.01 * nrm(ks[11], (DEPTH, N_EXPERTS, D_MODEL)),
        "ln2_gain": 1.0 + 0.02 * nrm(ks[12], (DEPTH, D_MODEL)),
        "ln2_bias": 0.02 * nrm(ks[13], (DEPTH, D_MODEL)),
    }


def reference(x, w_in, ret_gn_gain, w_out, ln1_gain, ln1_bias, w_router, b_router,
              w_gate_up, b_gate_up, w_down, b_down, ln2_gain, ln2_bias):
    b, s, _ = x.shape
    splits = [SB_WIDTH, 2 * SB_WIDTH, 3 * SB_WIDTH,
              3 * SB_WIDTH + RET_WIDTH, 3 * SB_WIDTH + 2 * RET_WIDTH, 3 * SB_WIDTH + 3 * RET_WIDTH]
    for layer in range(DEPTH):
        proj = x @ w_in[layer]
        sb_q, sb_k, sb_v, r_q, r_k, r_v, r_g = jnp.split(proj, splits, axis=-1)
        sb_heads = lambda a: a.reshape(b, s, SB_HEADS, HEAD_DIM).transpose(0, 2, 1, 3)
        sb_out = stick_breaking_attention(sb_heads(sb_q), sb_heads(sb_k), sb_heads(sb_v))
        ret_heads = lambda a: a.reshape(b, s, RET_HEADS, HEAD_DIM)
        ret_out = chunk_retention(rotary(ret_heads(r_q)), rotary(ret_heads(r_k)), ret_heads(r_v),
                                  ret_gn_gain[layer]) * jax.nn.silu(r_g)
        mix = jnp.concatenate([sb_out.astype(x.dtype), ret_out.astype(x.dtype)], axis=-1) @ w_out[layer]
        x = layer_norm(DEEPNORM_ALPHA * x + mix, ln1_gain[layer], ln1_bias[layer])
        ffn = moe_ffn(x, w_router[layer], b_router[layer], w_gate_up[layer], b_gate_up[layer],
                      w_down[layer], b_down[layer])
        x = layer_norm(DEEPNORM_ALPHA * x + ffn.astype(x.dtype), ln2_gain[layer], ln2_bias[layer])
    return x
```

```python
import functools
import math

import jax
import jax.numpy as jnp
from jax import lax
from jax.experimental import pallas as pl
from jax.experimental.pallas import tpu as pltpu

F32 = jnp.float32
BF16 = jnp.bfloat16

HEAD_DIM = 128
SB_HEADS = 8
RET_HEADS = 8
CHUNK = 64
ROPE_BASE = 10000.0
N_EXPERTS = 32
TOP_K = 4
SWIGLU_LIMIT = 7.0
SWIGLU_ALPHA = 1.702
LN_EPS = 1e-5
GN_EPS = 1e-5

V7X_VMEM_LIMIT_BYTES = 56 * 1024 * 1024
LANES = 128
ROW_BLOCK = 128
NEG_BIG = -1e30


def _cparams(sem, vmem=V7X_VMEM_LIMIT_BYTES):
    return pltpu.CompilerParams(dimension_semantics=sem, vmem_limit_bytes=vmem)


def _matmul_kernel(x_ref, w_ref, o_ref):
    o_ref[...] = jnp.dot(x_ref[...], w_ref[...], preferred_element_type=F32).astype(o_ref.dtype)


def _in_proj(x_bf, w_bf, tm=1024, tn=1024):
    t, d = x_bf.shape
    n = w_bf.shape[1]
    tm = min(tm, t)
    return pl.pallas_call(
        _matmul_kernel,
        grid=(n // tn, t // tm),
        in_specs=[pl.BlockSpec((tm, d), lambda j, i: (i, 0)),
                  pl.BlockSpec((d, tn), lambda j, i: (0, j))],
        out_specs=pl.BlockSpec((tm, tn), lambda j, i: (i, j)),
        out_shape=jax.ShapeDtypeStruct((t, n), BF16),
        compiler_params=_cparams(("arbitrary", "arbitrary")),
        name="in_proj",
    )(x_bf, w_bf)


def _sb_kernel(q_ref, k_ref, v_ref, o_ref, *, seq, scale):
    blk = 128
    nq = seq // blk
    row = lax.broadcasted_iota(jnp.int32, (blk, blk), 0)
    col = lax.broadcasted_iota(jnp.int32, (blk, blk), 1)
    causal = col < row
    r2 = lax.broadcasted_iota(jnp.int32, (2 * blk, 2 * blk), 0)
    c2 = lax.broadcasted_iota(jnp.int32, (2 * blk, 2 * blk), 1)
    cum_rhs = jnp.where((c2 >= blk) | ((r2 & (blk - 1)) >= c2), 1.0, 0.0).astype(BF16)

    def block(qb, kb, carry, acc, masked):
        ks = pl.ds(pl.multiple_of(kb * blk, blk), blk)
        k = k_ref[ks, :]
        v = v_ref[ks, :]
        z = lax.dot_general(qb, k, (((1,), (1,)), ((), ())), preferred_element_type=F32) * scale
        lnb = -(jnp.maximum(z, 0.0) + jnp.log(1.0 + jnp.exp(-jnp.abs(z))))
        if masked:
            lnb = jnp.where(causal, lnb, 0.0)
        hi = lnb.astype(BF16)
        lo = (lnb - hi.astype(F32)).astype(BF16)
        sums = jnp.dot(jnp.concatenate([hi, lo], axis=1), cum_rhs, preferred_element_type=F32)
        incl = sums[:, :blk]
        total = sums[:, blk:]
        w = jnp.exp(z + incl + carry)
        if masked:
            w = jnp.where(causal, w, 0.0)
        acc = acc + jnp.dot(w.astype(BF16), v, preferred_element_type=F32)
        return carry + total, acc

    def q_body(qi, _):
        qs = pl.ds(pl.multiple_of(qi * blk, blk), blk)
        qb = q_ref[qs, :]
        zeros = jnp.zeros((blk, blk), F32)
        carry, acc = block(qb, qi, zeros, zeros, True)

        def kb_body(t, c):
            return block(qb, qi - 1 - t, c[0], c[1], False)

        carry, acc = lax.fori_loop(0, qi, kb_body, (carry, acc))
        o_ref[qs, :] = acc.astype(o_ref.dtype)
        return 0

    lax.fori_loop(0, nq, q_body, 0)


def _stickbreak(proj, batch, seq):
    t = batch * seq
    h = SB_HEADS
    kern = functools.partial(_sb_kernel, seq=seq, scale=1.0 / math.sqrt(HEAD_DIM))
    spec = lambda off: pl.BlockSpec((seq, HEAD_DIM), lambda b, hh: (b, off + hh))
    return pl.pallas_call(
        kern,
        grid=(batch, h),
        in_specs=[spec(0), spec(h), spec(2 * h)],
        out_specs=pl.BlockSpec((seq, HEAD_DIM), lambda b, hh: (b, hh)),
        out_shape=jax.ShapeDtypeStruct((t, h * HEAD_DIM), BF16),
        compiler_params=_cparams(("arbitrary", "arbitrary")),
        name="stickbreak",
    )(proj, proj, proj)


def _ret_kernel(q_ref, k_ref, v_ref, g_ref, cos_ref, sin_ref, intra_ref, qdec_ref, kdec_ref,
                cdec_ref, gain_ref, o_ref, *, seq):
    n_chunks = seq // CHUNK
    half = HEAD_DIM // 2
    intra = intra_ref[...]
    qdec = qdec_ref[...]
    kdec = kdec_ref[...]
    cdec = cdec_ref[...]
    gain = gain_ref[...]
    k_scale = HEAD_DIM ** -0.5

    def chunk(n, state):
        rs = pl.ds(pl.multiple_of(n * CHUNK, CHUNK), CHUNK)
        q = q_ref[rs, :].astype(F32)
        k = k_ref[rs, :].astype(F32)
        v = v_ref[rs, :]
        g = g_ref[rs, :].astype(F32)
        cos = cos_ref[rs, :]
        sin = sin_ref[rs, :]
        qr = q * cos + pltpu.roll(q, half, 1) * sin
        kr = (k * cos + pltpu.roll(k, half, 1) * sin) * k_scale
        scores = lax.dot_general(qr.astype(BF16), kr.astype(BF16), (((1,), (1,)), ((), ())),
                                 preferred_element_type=F32) * intra
        o = jnp.dot(scores.astype(BF16), v, preferred_element_type=F32)
        o = o + jnp.dot((qr * qdec).astype(BF16), state.astype(BF16), preferred_element_type=F32)
        kv = lax.dot_general((kr * kdec).astype(BF16), v, (((0,), (0,)), ((), ())),
                             preferred_element_type=F32)
        state = state * cdec + kv
        mu = jnp.mean(o, axis=-1, keepdims=True)
        var = jnp.mean(jnp.square(o - mu), axis=-1, keepdims=True)
        on = (o - mu) * lax.rsqrt(var + GN_EPS)
        out = on * gain * (g * jax.nn.sigmoid(g))
        o_ref[rs, :] = out.astype(o_ref.dtype)
        return state

    lax.fori_loop(0, n_chunks, chunk, jnp.zeros((HEAD_DIM, HEAD_DIM), F32))


def _retention_tables(seq):
    d = HEAD_DIM
    inv_freq = ROPE_BASE ** (-jnp.arange(0, d, 2, dtype=F32) / d)
    ang = jnp.arange(seq, dtype=F32)[:, None] * inv_freq[None, :]
    cos, sin = jnp.cos(ang), jnp.sin(ang)
    cos_full = jnp.concatenate([cos, cos], axis=-1)
    sin_signed = jnp.concatenate([-sin, sin], axis=-1)
    log_gamma = jnp.log1p(-jnp.exp2(-5.0 - jnp.arange(RET_HEADS, dtype=F32)))
    i = jnp.arange(CHUNK, dtype=F32)
    intra = jnp.exp(log_gamma[:, None, None] * jnp.abs(i[:, None] - i[None, :]))
    k_decay = jnp.exp(log_gamma[:, None] * (CHUNK - 1 - i))
    q_decay = jnp.exp(log_gamma[:, None] * (i + 1.0))
    c_decay = jnp.exp(log_gamma * CHUNK)
    bc = lambda a: jnp.broadcast_to(a[..., None], a.shape + (d,))
    return cos_full, sin_signed, intra, bc(q_decay), bc(k_decay), bc(c_decay[:, None])


def _retention(proj, gn_gain, batch, seq):
    t = batch * seq
    h = RET_HEADS
    base = 3 * SB_HEADS
    cos_full, sin_signed, intra, qdec, kdec, cdec = _retention_tables(seq)
    spec = lambda off: pl.BlockSpec((seq, HEAD_DIM), lambda b, hh: (b, base + off + hh))
    full = pl.BlockSpec((seq, HEAD_DIM), lambda b, hh: (0, 0))
    per_head = lambda r, c: pl.BlockSpec((None, r, c), lambda b, hh: (hh, 0, 0))
    return pl.pallas_call(
        functools.partial(_ret_kernel, seq=seq),
        grid=(batch, h),
        in_specs=[spec(0), spec(h), spec(2 * h), spec(3 * h), full, full,
                  per_head(CHUNK, CHUNK), per_head(CHUNK, HEAD_DIM), per_head(CHUNK, HEAD_DIM),
                  per_head(1, HEAD_DIM),
                  pl.BlockSpec((1, HEAD_DIM), lambda b, hh: (0, hh))],
        out_specs=pl.BlockSpec((seq, HEAD_DIM), lambda b, hh: (b, hh)),
        out_shape=jax.ShapeDtypeStruct((t, h * HEAD_DIM), BF16),
        compiler_params=_cparams(("arbitrary", "arbitrary")),
        name="retention",
    )(proj, proj, proj, proj, cos_full, sin_signed, intra, qdec, kdec, cdec,
      gn_gain.reshape(1, h * HEAD_DIM).astype(F32))


def _layer_norm(hid, gain, bias):
    mu = jnp.mean(hid, axis=-1, keepdims=True)
    cen = hid - mu
    var = jnp.mean(jnp.square(cen), axis=-1, keepdims=True)
    return cen * lax.rsqrt(var + LN_EPS) * gain + bias


def _outproj_kernel(sb_ref, ret_ref, x_ref, w_ref, g_ref, b_ref, wr_ref, br_ref,
                    x1_ref, meta_ref, cnt_ref, *, alpha, sb_width):
    tm = x_ref.shape[0]
    mix = jnp.dot(sb_ref[...], w_ref[:sb_width, :], preferred_element_type=F32)
    mix = mix + jnp.dot(ret_ref[...], w_ref[sb_width:, :], preferred_element_type=F32)
    x1 = _layer_norm(alpha * x_ref[...] + mix, g_ref[...], b_ref[...])
    x1_ref[...] = x1

    logits = jnp.dot(x1, wr_ref[...], preferred_element_type=F32,
                     precision=lax.Precision.HIGHEST) + br_ref[...]
    lane = lax.broadcasted_iota(jnp.int32, (tm, LANES), 1).astype(F32)
    vals = logits
    tops, idxs, hots = [], [], []
    for _ in range(TOP_K):
        m = jnp.max(vals, axis=-1, keepdims=True)
        idx = jnp.min(jnp.where(vals == m, lane, float(LANES)), axis=-1, keepdims=True)
        hot = lane == idx
        vals = jnp.where(hot, NEG_BIG * 2.0, vals)
        tops.append(m)
        idxs.append(idx)
        hots.append(hot)
    exps = [jnp.exp(m - tops[0]) for m in tops]
    denom = exps[0] + exps[1] + exps[2] + exps[3]
    gates = [e / denom for e in exps]

    @pl.when(pl.program_id(0) == 0)
    def _():
        cnt_ref[...] = jnp.zeros_like(cnt_ref)

    multi = jnp.zeros((tm, LANES), F32)
    for hot in hots:
        multi = multi + jnp.where(hot, 1.0, 0.0)
    r = lax.broadcasted_iota(jnp.int32, (tm, tm), 0)
    c = lax.broadcasted_iota(jnp.int32, (tm, tm), 1)
    strict_lower = jnp.where(c < r, 1.0, 0.0).astype(BF16)
    before = jnp.dot(strict_lower, multi.astype(BF16), preferred_element_type=F32) + cnt_ref[...]
    cnt_ref[...] = cnt_ref[...] + jnp.sum(multi, axis=0, keepdims=True)

    meta = jnp.zeros((tm, LANES), F32)
    for kk in range(TOP_K):
        rank = jnp.sum(jnp.where(hots[kk], before, 0.0), axis=-1, keepdims=True)
        meta = jnp.where(lane == kk, idxs[kk], meta)
        meta = jnp.where(lane == TOP_K + kk, gates[kk], meta)
        meta = jnp.where(lane == 2 * TOP_K + kk, rank, meta)
    meta_ref[...] = meta


def _outproj(sb, ret, xt, w_out_bf, ln_g, ln_b, w_router, b_router, alpha, tm=256):
    t, d = xt.shape
    tm = min(tm, t)
    sbw = sb.shape[1]
    wr = jnp.zeros((d, LANES), F32).at[:, :N_EXPERTS].set(w_router.astype(F32))
    br = jnp.full((1, LANES), NEG_BIG, F32).at[0, :N_EXPERTS].set(b_router.astype(F32))
    row = lambda w: pl.BlockSpec((tm, w), lambda i: (i, 0))
    const = lambda r, c: pl.BlockSpec((r, c), lambda i: (0, 0))
    return pl.pallas_call(
        functools.partial(_outproj_kernel, alpha=alpha, sb_width=sbw),
        grid=(t // tm,),
        in_specs=[row(sbw), row(ret.shape[1]), row(d), const(d, d), const(1, d), const(1, d),
                  const(d, LANES), const(1, LANES)],
        out_specs=[row(d), row(LANES), const(1, LANES)],
        out_shape=[jax.ShapeDtypeStruct((t, d), F32),
                   jax.ShapeDtypeStruct((t, LANES), F32),
                   jax.ShapeDtypeStruct((1, LANES), F32)],
        compiler_params=_cparams(("arbitrary",)),
        name="outproj_ln_router",
    )(sb, ret, xt, w_out_bf, ln_g.reshape(1, d).astype(F32), ln_b.reshape(1, d).astype(F32), wr, br)


def _gather_kernel(tok_ref, x_hbm, o_ref, buf, sem, *, rows):
    base = pl.program_id(0) * rows

    def row_copy(r):
        tok = tok_ref[base + r]
        return pltpu.make_async_copy(x_hbm.at[pl.ds(tok, 1), :], buf.at[pl.ds(r, 1), :], sem)

    def issue(r, _):
        row_copy(r).start()
        return 0

    def drain(r, _):
        row_copy(r).wait()
        return 0

    lax.fori_loop(0, rows, issue, 0)
    lax.fori_loop(0, rows, drain, 0)
    o_ref[...] = buf[...].astype(o_ref.dtype)


def _gather_rows(row_tok, x1, rows=256):
    n_rows = row_tok.shape[0]
    d = x1.shape[1]
    return pl.pallas_call(
        functools.partial(_gather_kernel, rows=rows),
        grid_spec=pltpu.PrefetchScalarGridSpec(
            num_scalar_prefetch=1,
            grid=(n_rows // rows,),
            in_specs=[pl.BlockSpec(memory_space=pl.ANY)],
            out_specs=pl.BlockSpec((rows, d), lambda i, tok: (i, 0)),
            scratch_shapes=[pltpu.VMEM((rows, d), F32), pltpu.SemaphoreType.DMA(())],
        ),
        out_shape=jax.ShapeDtypeStruct((n_rows, d), BF16),
        compiler_params=_cparams(("arbitrary",)),
        name="gather_rows",
    )(row_tok, x1)


def _moe_kernel(ie_ref, ij_ref, ist_ref, ins_ref, tail_ref, x_hbm, wg_ref, wu_ref, bg_ref, bu_ref,
                wd_ref, bd_ref, y_hbm, x_vmem, acc, wg_bf, wu_bf, wd_bf, sem_in, sem_out, *, n_f):
    del ie_ref, ij_ref
    i = pl.program_id(0)
    j = pl.program_id(1)
    nsub = ins_ref[i]
    start = ist_ref[i]
    d = acc.shape[1]

    def rows_of(r):
        return pl.ds(pl.multiple_of(r * ROW_BLOCK, ROW_BLOCK), ROW_BLOCK)

    @pl.when((i == 0) & (j == 0))
    def _():
        acc[pl.ds(0, ROW_BLOCK), :] = jnp.zeros((ROW_BLOCK, d), F32)

        def zero_copy(bk):
            dst = pl.ds(pl.multiple_of(bk * ROW_BLOCK, ROW_BLOCK), ROW_BLOCK)
            return pltpu.make_async_copy(acc.at[pl.ds(0, ROW_BLOCK), :], y_hbm.at[dst, :], sem_out)

        def z_start(bk, _):
            zero_copy(bk).start()
            return 0

        def z_wait(bk, _):
            zero_copy(bk).wait()
            return 0

        n_blocks = y_hbm.shape[0] // ROW_BLOCK
        lax.fori_loop(tail_ref[0], n_blocks, z_start, 0)
        lax.fori_loop(tail_ref[0], n_blocks, z_wait, 0)

    def in_copy(r):
        src = pl.ds(pl.multiple_of(start + r * ROW_BLOCK, ROW_BLOCK), ROW_BLOCK)
        return pltpu.make_async_copy(x_hbm.at[src, :], x_vmem.at[rows_of(r), :], sem_in)

    def out_copy(r):
        dst = pl.ds(pl.multiple_of(start + r * ROW_BLOCK, ROW_BLOCK), ROW_BLOCK)
        return pltpu.make_async_copy(acc.at[rows_of(r), :], y_hbm.at[dst, :], sem_out)

    def for_each_sub(fn):
        def body(r, _):
            fn(r)
            return 0
        lax.fori_loop(0, nsub, body, 0)

    @pl.when(nsub > 0)
    def _():
        @pl.when(j == 0)
        def _():
            for_each_sub(lambda r: in_copy(r).start())
            bias_rows = jnp.broadcast_to(bd_ref[...], (ROW_BLOCK, d))

            def init(r):
                acc[rows_of(r), :] = bias_rows
            for_each_sub(init)
            for_each_sub(lambda r: in_copy(r).wait())

        wg_bf[...] = wg_ref[...].astype(BF16)
        wu_bf[...] = wu_ref[...].astype(BF16)
        wd_bf[...] = wd_ref[...].astype(BF16)

        def sub(r):
            xb = x_vmem[rows_of(r), :]
            gate = jnp.dot(xb, wg_bf[...], preferred_element_type=F32) + bg_ref[...]
            up = jnp.dot(xb, wu_bf[...], preferred_element_type=F32) + bu_ref[...]
            gate = jnp.minimum(gate, SWIGLU_LIMIT)
            up = jnp.clip(up, -SWIGLU_LIMIT, SWIGLU_LIMIT)
            act = (up + 1.0) * (gate * jax.nn.sigmoid(SWIGLU_ALPHA * gate))
            acc[rows_of(r), :] += jnp.dot(act.astype(BF16), wd_bf[...], preferred_element_type=F32)
        for_each_sub(sub)

        @pl.when(j == n_f - 1)
        def _():
            for_each_sub(lambda r: out_copy(r).start())
            for_each_sub(lambda r: out_copy(r).wait())


def _moe_ffn(item_e, item_j, item_start, item_nsub, tail_block, x_rows, w_gate_up, b_gate_up, w_down, b_down,
             r_max, tf=512):
    n_rows, d = x_rows.shape
    n_e, _, two_f = w_gate_up.shape
    d_ff = two_f // 2
    n_f = d_ff // tf
    n_items = item_e.shape[0]

    def jf(i, j, ij):
        return jnp.where(ij[i] < 0, j, ij[i])

    in_specs = [
        pl.BlockSpec(memory_space=pl.ANY),
        pl.BlockSpec((None, d, tf), lambda i, j, ie, ij, ist, ins, tl: (ie[i], 0, jf(i, j, ij))),
        pl.BlockSpec((None, d, tf), lambda i, j, ie, ij, ist, ins, tl: (ie[i], 0, n_f + jf(i, j, ij))),
        pl.BlockSpec((None, 1, tf), lambda i, j, ie, ij, ist, ins, tl: (ie[i], 0, jf(i, j, ij))),
        pl.BlockSpec((None, 1, tf), lambda i, j, ie, ij, ist, ins, tl: (ie[i], 0, n_f + jf(i, j, ij))),
        pl.BlockSpec((None, tf, d), lambda i, j, ie, ij, ist, ins, tl: (ie[i], jf(i, j, ij), 0)),
        pl.BlockSpec((None, 1, d), lambda i, j, ie, ij, ist, ins, tl: (ie[i], 0, 0)),
    ]
    return pl.pallas_call(
        functools.partial(_moe_kernel, n_f=n_f),
        grid_spec=pltpu.PrefetchScalarGridSpec(
            num_scalar_prefetch=5,
            grid=(n_items, n_f),
            in_specs=in_specs,
            out_specs=pl.BlockSpec(memory_space=pl.ANY),
            scratch_shapes=[pltpu.VMEM((r_max, d), BF16), pltpu.VMEM((r_max, d), F32),
                            pltpu.VMEM((d, tf), BF16), pltpu.VMEM((d, tf), BF16),
                            pltpu.VMEM((tf, d), BF16),
                            pltpu.SemaphoreType.DMA(()), pltpu.SemaphoreType.DMA(())],
        ),
        out_shape=jax.ShapeDtypeStruct((n_rows, d), F32),
        compiler_params=_cparams(("arbitrary", "arbitrary")),
        name="moe_ffn",
    )(item_e, item_j, item_start, item_nsub, tail_block, x_rows, w_gate_up, w_gate_up,
      b_gate_up.reshape(n_e, 1, two_f), b_gate_up.reshape(n_e, 1, two_f), w_down,
      b_down.reshape(n_e, 1, d))


def _combine_kernel(dest_ref, y_hbm, x1_ref, meta_ref, g_ref, b_ref, o_ref, buf, sem, *, alpha, tm):
    base = pl.program_id(0) * (tm * TOP_K)

    def row_copy(n):
        kk = n % TOP_K
        tt = n // TOP_K
        src = dest_ref[base + n]
        return pltpu.make_async_copy(y_hbm.at[pl.ds(src, 1), :], buf.at[kk, pl.ds(tt, 1), :], sem)

    def issue(n, _):
        row_copy(n).start()
        return 0

    def drain(n, _):
        row_copy(n).wait()
        return 0

    lax.fori_loop(0, tm * TOP_K, issue, 0)
    lax.fori_loop(0, tm * TOP_K, drain, 0)
    meta = meta_ref[...]
    y = jnp.zeros(x1_ref.shape, F32)
    for kk in range(TOP_K):
        y = y + meta[:, TOP_K + kk:TOP_K + kk + 1] * buf[kk]
    o_ref[...] = _layer_norm(alpha * x1_ref[...] + y, g_ref[...], b_ref[...])


def _combine(dest_flat, y_rows, x1, meta, ln_g, ln_b, alpha, tm=128):
    t, d = x1.shape
    tm = min(tm, t)
    row = lambda w: pl.BlockSpec((tm, w), lambda i, dest: (i, 0))
    const = pl.BlockSpec((1, d), lambda i, dest: (0, 0))
    return pl.pallas_call(
        functools.partial(_combine_kernel, alpha=alpha, tm=tm),
        grid_spec=pltpu.PrefetchScalarGridSpec(
            num_scalar_prefetch=1,
            grid=(t // tm,),
            in_specs=[pl.BlockSpec(memory_space=pl.ANY), row(d), row(LANES), const, const],
            out_specs=row(d),
            scratch_shapes=[pltpu.VMEM((TOP_K, tm, d), F32), pltpu.SemaphoreType.DMA(())],
        ),
        out_shape=jax.ShapeDtypeStruct((t, d), F32),
        compiler_params=_cparams(("arbitrary",)),
        name="combine_ln",
    )(dest_flat, y_rows, x1, meta, ln_g.reshape(1, d).astype(F32), ln_b.reshape(1, d).astype(F32))


def _routing_plan(meta, counts_f, t, r_max):
    idx = meta[:, 0:TOP_K].astype(jnp.int32)
    rank = meta[:, 2 * TOP_K:3 * TOP_K].astype(jnp.int32)
    counts = counts_f[0, :N_EXPERTS].astype(jnp.int32)
    n128 = (counts + ROW_BLOCK - 1) // ROW_BLOCK
    padded = n128 * ROW_BLOCK
    pad_start = jnp.cumsum(padded) - padded
    dest = pad_start[idx] + rank
    tk = t * TOP_K
    n_rows = (tk + ROW_BLOCK - 1) // ROW_BLOCK * ROW_BLOCK + N_EXPERTS * ROW_BLOCK
    flat_tok = jnp.arange(tk, dtype=jnp.int32) // TOP_K
    row_tok = jnp.zeros((n_rows,), jnp.int32).at[dest.reshape(tk)].set(flat_tok)

    subs = r_max // ROW_BLOCK
    n_items = N_EXPERTS + (n_rows // ROW_BLOCK - N_EXPERTS) // subs
    items_e = (n128 + subs - 1) // subs
    items_end = jnp.cumsum(items_e)
    total = items_end[-1]
    slot = jnp.arange(n_items, dtype=jnp.int32)
    live = slot < total
    s_eff = jnp.minimum(slot, total - 1)
    e = jnp.minimum(jnp.searchsorted(items_end, s_eff, side='right'), N_EXPERTS - 1).astype(jnp.int32)
    local = s_eff - (items_end[e] - items_e[e])
    item_start = (pad_start[e] + local * r_max).astype(jnp.int32)
    item_nsub = jnp.where(live, jnp.clip(n128[e] - local * subs, 0, subs), 0).astype(jnp.int32)
    tail_block = jnp.sum(n128).reshape(1).astype(jnp.int32)
    return dest, row_tok, e, live, item_start, item_nsub, tail_block


def kernel(x, w_in, ret_gn_gain, w_out, ln1_gain, ln1_bias, w_router, b_router, w_gate_up, b_gate_up,
           w_down, b_down, ln2_gain, ln2_bias):
    b, s, d = x.shape
    t = b * s
    depth = w_in.shape[0]
    alpha = (2 * depth) ** 0.25
    r_max = 1280
    tf = 512
    n_f = (w_gate_up.shape[-1] // 2) // tf
    xt = x.reshape(t, d)
    for layer in range(depth):
        proj = _in_proj(xt.astype(BF16), w_in[layer].astype(BF16))
        sb = _stickbreak(proj, b, s)
        ret = _retention(proj, ret_gn_gain[layer], b, s)
        x1, meta, counts = _outproj(sb, ret, xt, w_out[layer].astype(BF16), ln1_gain[layer],
                                    ln1_bias[layer], w_router[layer], b_router[layer], alpha)
        dest, row_tok, item_e, live, item_start, item_nsub, tail_block = _routing_plan(
            meta, counts, t, r_max)
        item_j = jnp.where(live, -1, n_f - 1).astype(jnp.int32)
        x_rows = _gather_rows(row_tok, x1)
        y_rows = _moe_ffn(item_e, item_j, item_start, item_nsub, tail_block, x_rows, w_gate_up[layer],
                          b_gate_up[layer], w_down[layer], b_down[layer], r_max, tf)
        xt = _combine(dest.reshape(t * TOP_K), y_rows, x1, meta, ln2_gain[layer], ln2_bias[layer], alpha)
    return xt.reshape(b, s, d)
```

```python
import functools
import math

import jax
import jax.numpy as jnp
from jax import lax
from jax.experimental import pallas as pl
from jax.experimental.pallas import tpu as pltpu

F32 = jnp.float32
BF16 = jnp.bfloat16

HEAD_DIM = 128
SB_HEADS = 8
RET_HEADS = 8
CHUNK = 64
ROPE_BASE = 10000.0
N_EXPERTS = 32
TOP_K = 4
SWIGLU_LIMIT = 7.0
SWIGLU_ALPHA = 1.702
LN_EPS = 1e-5
GN_EPS = 1e-5

V7X_VMEM_LIMIT_BYTES = 56 * 1024 * 1024
LANES = 128
ROW_BLOCK = 128
NEG_BIG = -1e30


def _cparams(sem, vmem=V7X_VMEM_LIMIT_BYTES):
    return pltpu.CompilerParams(dimension_semantics=sem, vmem_limit_bytes=vmem)


def _matmul_kernel(x_ref, w_ref, o_ref):
    o_ref[...] = jnp.dot(x_ref[...], w_ref[...], preferred_element_type=F32).astype(o_ref.dtype)


def _in_proj(x_bf, w_bf, tm=1024, tn=1024):
    t, d = x_bf.shape
    n = w_bf.shape[1]
    tm = min(tm, t)
    return pl.pallas_call(
        _matmul_kernel,
        grid=(n // tn, t // tm),
        in_specs=[pl.BlockSpec((tm, d), lambda j, i: (i, 0)),
                  pl.BlockSpec((d, tn), lambda j, i: (0, j))],
        out_specs=pl.BlockSpec((tm, tn), lambda j, i: (i, j)),
        out_shape=jax.ShapeDtypeStruct((t, n), BF16),
        compiler_params=_cparams(("arbitrary", "arbitrary")),
        name="in_proj",
    )(x_bf, w_bf)


def _sb_kernel(q_ref, k_ref, v_ref, o_ref, *, seq, scale, group, sub):
    blk = 128
    nq = seq // blk
    row = lax.broadcasted_iota(jnp.int32, (blk, blk), 0)
    col = lax.broadcasted_iota(jnp.int32, (blk, blk), 1)
    causal = col < row
    r2 = lax.broadcasted_iota(jnp.int32, (2 * blk, 2 * blk), 0)
    c2 = lax.broadcasted_iota(jnp.int32, (2 * blk, 2 * blk), 1)
    cum_rhs = jnp.where((c2 >= blk) | ((r2 & (blk - 1)) >= c2), 1.0, 0.0).astype(BF16)

    n_sub = group // sub
    causal_sub = jnp.concatenate([causal] * sub, axis=0)
    heads_of = lambda s: range(s * sub, (s + 1) * sub)
    cols = lambda g: slice(g * blk, (g + 1) * blk)

    def block(qbs, kb, carries, accs, masked):
        ks = pl.ds(pl.multiple_of(kb * blk, blk), blk)
        zs = [jnp.concatenate(
            [lax.dot_general(qbs[g], k_ref[ks, cols(g)], (((1,), (1,)), ((), ())),
                             preferred_element_type=F32) for g in heads_of(s)], axis=0) * scale
              for s in range(n_sub)]
        sums = []
        for z in zs:
            neg_z = -z
            lnb = jnp.minimum(neg_z, 0.0) - jnp.log(1.0 + jnp.exp(jnp.minimum(z, neg_z)))
            if masked:
                lnb = jnp.where(causal_sub, lnb, 0.0)
            hi = lnb.astype(BF16)
            lo = (lnb - hi.astype(F32)).astype(BF16)
            sums.append(jnp.dot(jnp.concatenate([hi, lo], axis=1), cum_rhs, preferred_element_type=F32))
        new_carries, new_accs = [], []
        for s in range(n_sub):
            incl = sums[s][:, :blk]
            total = sums[s][:, blk:]
            w = jnp.exp(zs[s] + incl + carries[s])
            if masked:
                w = jnp.where(causal_sub, w, 0.0)
            w = w.astype(BF16)
            for n, g in enumerate(heads_of(s)):
                new_accs.append(accs[g] + jnp.dot(w[n * blk:(n + 1) * blk], v_ref[ks, cols(g)],
                                                  preferred_element_type=F32))
            new_carries.append(carries[s] + total)
        return tuple(new_carries), tuple(new_accs)

    def q_body(qi, _):
        qs = pl.ds(pl.multiple_of(qi * blk, blk), blk)
        qbs = [q_ref[qs, cols(g)] for g in range(group)]
        state = block(qbs, qi, (jnp.zeros((sub * blk, blk), F32),) * n_sub,
                      (jnp.zeros((blk, blk), F32),) * group, True)

        def kb_body(t, st):
            return block(qbs, qi - 1 - t, st[0], st[1], False)

        _, accs = lax.fori_loop(0, qi, kb_body, state)
        for g in range(group):
            o_ref[qs, cols(g)] = accs[g].astype(o_ref.dtype)
        return 0

    lax.fori_loop(0, nq, q_body, 0)


def _stickbreak(proj, batch, seq, group=8, sub=4):
    t = batch * seq
    h = SB_HEADS
    n_groups = h // group
    width = group * HEAD_DIM
    kern = functools.partial(_sb_kernel, seq=seq, scale=1.0 / math.sqrt(HEAD_DIM), group=group,
                             sub=sub)
    spec = lambda off: pl.BlockSpec((seq, width), lambda b, hg: (b, off * n_groups + hg))
    return pl.pallas_call(
        kern,
        grid=(batch, n_groups),
        in_specs=[spec(0), spec(1), spec(2)],
        out_specs=pl.BlockSpec((seq, width), lambda b, hg: (b, hg)),
        out_shape=jax.ShapeDtypeStruct((t, h * HEAD_DIM), BF16),
        compiler_params=_cparams(("arbitrary", "arbitrary")),
        name="stickbreak",
    )(proj, proj, proj)


def _ret_kernel(q_ref, k_ref, v_ref, g_ref, cos_ref, sin_ref, intra_ref, qdec_ref, kdec_ref,
                cdec_ref, gain_ref, o_ref, *, seq):
    n_chunks = seq // CHUNK
    half = HEAD_DIM // 2
    intra = intra_ref[...]
    qdec = qdec_ref[...]
    kdec = kdec_ref[...]
    cdec = cdec_ref[...]
    gain = gain_ref[...]
    k_scale = HEAD_DIM ** -0.5

    def chunk(n, state):
        rs = pl.ds(pl.multiple_of(n * CHUNK, CHUNK), CHUNK)
        q = q_ref[rs, :].astype(F32)
        k = k_ref[rs, :].astype(F32)
        v = v_ref[rs, :]
        g = g_ref[rs, :].astype(F32)
        cos = cos_ref[rs, :]
        sin = sin_ref[rs, :]
        qr = q * cos + pltpu.roll(q, half, 1) * sin
        kr = (k * cos + pltpu.roll(k, half, 1) * sin) * k_scale
        scores = lax.dot_general(qr.astype(BF16), kr.astype(BF16), (((1,), (1,)), ((), ())),
                                 preferred_element_type=F32) * intra
        o = jnp.dot(scores.astype(BF16), v, preferred_element_type=F32)
        o = o + jnp.dot((qr * qdec).astype(BF16), state.astype(BF16), preferred_element_type=F32)
        kv = lax.dot_general((kr * kdec).astype(BF16), v, (((0,), (0,)), ((), ())),
                             preferred_element_type=F32)
        state = state * cdec + kv
        mu = jnp.mean(o, axis=-1, keepdims=True)
        var = jnp.mean(jnp.square(o - mu), axis=-1, keepdims=True)
        on = (o - mu) * lax.rsqrt(var + GN_EPS)
        out = on * gain * (g * jax.nn.sigmoid(g))
        o_ref[rs, :] = out.astype(o_ref.dtype)
        return state

    lax.fori_loop(0, n_chunks, chunk, jnp.zeros((HEAD_DIM, HEAD_DIM), F32))


def _retention_tables(seq):
    d = HEAD_DIM
    inv_freq = ROPE_BASE ** (-jnp.arange(0, d, 2, dtype=F32) / d)
    ang = jnp.arange(seq, dtype=F32)[:, None] * inv_freq[None, :]
    cos, sin = jnp.cos(ang), jnp.sin(ang)
    cos_full = jnp.concatenate([cos, cos], axis=-1)
    sin_signed = jnp.concatenate([-sin, sin], axis=-1)
    log_gamma = jnp.log1p(-jnp.exp2(-5.0 - jnp.arange(RET_HEADS, dtype=F32)))
    i = jnp.arange(CHUNK, dtype=F32)
    intra = jnp.exp(log_gamma[:, None, None] * jnp.abs(i[:, None] - i[None, :]))
    k_decay = jnp.exp(log_gamma[:, None] * (CHUNK - 1 - i))
    q_decay = jnp.exp(log_gamma[:, None] * (i + 1.0))
    c_decay = jnp.exp(log_gamma * CHUNK)
    bc = lambda a: jnp.broadcast_to(a[..., None], a.shape + (d,))
    return cos_full, sin_signed, intra, bc(q_decay), bc(k_decay), bc(c_decay[:, None])


def _retention(proj, gn_gain, batch, seq):
    t = batch * seq
    h = RET_HEADS
    base = 3 * SB_HEADS
    cos_full, sin_signed, intra, qdec, kdec, cdec = _retention_tables(seq)
    spec = lambda off: pl.BlockSpec((seq, HEAD_DIM), lambda b, hh: (b, base + off + hh))
    full = pl.BlockSpec((seq, HEAD_DIM), lambda b, hh: (0, 0))
    per_head = lambda r, c: pl.BlockSpec((None, r, c), lambda b, hh: (hh, 0, 0))
    return pl.pallas_call(
        functools.partial(_ret_kernel, seq=seq),
        grid=(batch, h),
        in_specs=[spec(0), spec(h), spec(2 * h), spec(3 * h), full, full,
                  per_head(CHUNK, CHUNK), per_head(CHUNK, HEAD_DIM), per_head(CHUNK, HEAD_DIM),
                  per_head(1, HEAD_DIM),
                  pl.BlockSpec((1, HEAD_DIM), lambda b, hh: (0, hh))],
        out_specs=pl.BlockSpec((seq, HEAD_DIM), lambda b, hh: (b, hh)),
        out_shape=jax.ShapeDtypeStruct((t, h * HEAD_DIM), BF16),
        compiler_params=_cparams(("arbitrary", "arbitrary")),
        name="retention",
    )(proj, proj, proj, proj, cos_full, sin_signed, intra, qdec, kdec, cdec,
      gn_gain.reshape(1, h * HEAD_DIM).astype(F32))


def _layer_norm(hid, gain, bias):
    mu = jnp.mean(hid, axis=-1, keepdims=True)
    cen = hid - mu
    var = jnp.mean(jnp.square(cen), axis=-1, keepdims=True)
    return cen * lax.rsqrt(var + LN_EPS) * gain + bias


def _outproj_kernel(sb_ref, ret_ref, x_ref, w_ref, g_ref, b_ref, wr_ref, br_ref,
                    x1_ref, meta_ref, cnt_ref, *, alpha, sb_width):
    tm = x_ref.shape[0]
    mix = jnp.dot(sb_ref[...], w_ref[:sb_width, :], preferred_element_type=F32)
    mix = mix + jnp.dot(ret_ref[...], w_ref[sb_width:, :], preferred_element_type=F32)
    x1 = _layer_norm(alpha * x_ref[...] + mix, g_ref[...], b_ref[...])
    x1_ref[...] = x1

    logits = jnp.dot(x1, wr_ref[...], preferred_element_type=F32,
                     precision=lax.Precision.HIGHEST) + br_ref[...]
    lane = lax.broadcasted_iota(jnp.int32, (tm, LANES), 1).astype(F32)
    vals = logits
    tops, idxs, hots = [], [], []
    for _ in range(TOP_K):
        m = jnp.max(vals, axis=-1, keepdims=True)
        idx = jnp.min(jnp.where(vals == m, lane, float(LANES)), axis=-1, keepdims=True)
        hot = lane == idx
        vals = jnp.where(hot, NEG_BIG * 2.0, vals)
        tops.append(m)
        idxs.append(idx)
        hots.append(hot)
    exps = [jnp.exp(m - tops[0]) for m in tops]
    denom = exps[0] + exps[1] + exps[2] + exps[3]
    gates = [e / denom for e in exps]

    @pl.when(pl.program_id(0) == 0)
    def _():
        cnt_ref[...] = jnp.zeros_like(cnt_ref)

    multi = jnp.zeros((tm, LANES), F32)
    for hot in hots:
        multi = multi + jnp.where(hot, 1.0, 0.0)
    r = lax.broadcasted_iota(jnp.int32, (tm, tm), 0)
    c = lax.broadcasted_iota(jnp.int32, (tm, tm), 1)
    strict_lower = jnp.where(c < r, 1.0, 0.0).astype(BF16)
    before = jnp.dot(strict_lower, multi.astype(BF16), preferred_element_type=F32) + cnt_ref[...]
    cnt_ref[...] = cnt_ref[...] + jnp.sum(multi, axis=0, keepdims=True)

    meta = jnp.zeros((tm, LANES), F32)
    for kk in range(TOP_K):
        rank = jnp.sum(jnp.where(hots[kk], before, 0.0), axis=-1, keepdims=True)
        meta = jnp.where(lane == kk, idxs[kk], meta)
        meta = jnp.where(lane == TOP_K + kk, gates[kk], meta)
        meta = jnp.where(lane == 2 * TOP_K + kk, rank, meta)
    meta_ref[...] = meta


def _outproj(sb, ret, xt, w_out_bf, ln_g, ln_b, w_router, b_router, alpha, tm=256):
    t, d = xt.shape
    tm = min(tm, t)
    sbw = sb.shape[1]
    wr = jnp.zeros((d, LANES), F32).at[:, :N_EXPERTS].set(w_router.astype(F32))
    br = jnp.full((1, LANES), NEG_BIG, F32).at[0, :N_EXPERTS].set(b_router.astype(F32))
    row = lambda w: pl.BlockSpec((tm, w), lambda i: (i, 0))
    const = lambda r, c: pl.BlockSpec((r, c), lambda i: (0, 0))
    return pl.pallas_call(
        functools.partial(_outproj_kernel, alpha=alpha, sb_width=sbw),
        grid=(t // tm,),
        in_specs=[row(sbw), row(ret.shape[1]), row(d), const(d, d), const(1, d), const(1, d),
                  const(d, LANES), const(1, LANES)],
        out_specs=[row(d), row(LANES), const(1, LANES)],
        out_shape=[jax.ShapeDtypeStruct((t, d), F32),
                   jax.ShapeDtypeStruct((t, LANES), F32),
                   jax.ShapeDtypeStruct((1, LANES), F32)],
        compiler_params=_cparams(("arbitrary",)),
        name="outproj_ln_router",
    )(sb, ret, xt, w_out_bf, ln_g.reshape(1, d).astype(F32), ln_b.reshape(1, d).astype(F32), wr, br)


def _gather_kernel(tok_ref, x_hbm, o_ref, buf, sem, *, rows):
    base = pl.program_id(0) * rows

    def row_copy(r):
        tok = tok_ref[base + r]
        return pltpu.make_async_copy(x_hbm.at[pl.ds(tok, 1), :], buf.at[pl.ds(r, 1), :], sem)

    def issue(r, _):
        row_copy(r).start()
        return 0

    def drain(r, _):
        row_copy(r).wait()
        return 0

    lax.fori_loop(0, rows, issue, 0)
    lax.fori_loop(0, rows, drain, 0)
    o_ref[...] = buf[...].astype(o_ref.dtype)


def _gather_rows(row_tok, x1, rows=256):
    n_rows = row_tok.shape[0]
    d = x1.shape[1]
    return pl.pallas_call(
        functools.partial(_gather_kernel, rows=rows),
        grid_spec=pltpu.PrefetchScalarGridSpec(
            num_scalar_prefetch=1,
            grid=(n_rows // rows,),
            in_specs=[pl.BlockSpec(memory_space=pl.ANY)],
            out_specs=pl.BlockSpec((rows, d), lambda i, tok: (i, 0)),
            scratch_shapes=[pltpu.VMEM((rows, d), F32), pltpu.SemaphoreType.DMA(())],
        ),
        out_shape=jax.ShapeDtypeStruct((n_rows, d), BF16),
        compiler_params=_cparams(("arbitrary",)),
        name="gather_rows",
    )(row_tok, x1)


def _moe_kernel(ie_ref, ij_ref, ist_ref, ins_ref, tail_ref, x_hbm, wg_ref, wu_ref, bg_ref, bu_ref,
                wd_ref, bd_ref, y_hbm, x_vmem, acc, wg_bf, wu_bf, wd_bf, sem_in, sem_out, *, n_f):
    del ie_ref, ij_ref
    i = pl.program_id(0)
    j = pl.program_id(1)
    nsub = ins_ref[i]
    start = ist_ref[i]
    d = acc.shape[1]

    def rows_of(r):
        return pl.ds(pl.multiple_of(r * ROW_BLOCK, ROW_BLOCK), ROW_BLOCK)

    @pl.when((i == 0) & (j == 0))
    def _():
        acc[pl.ds(0, ROW_BLOCK), :] = jnp.zeros((ROW_BLOCK, d), F32)

        def zero_copy(bk):
            dst = pl.ds(pl.multiple_of(bk * ROW_BLOCK, ROW_BLOCK), ROW_BLOCK)
            return pltpu.make_async_copy(acc.at[pl.ds(0, ROW_BLOCK), :], y_hbm.at[dst, :], sem_out)

        def z_start(bk, _):
            zero_copy(bk).start()
            return 0

        def z_wait(bk, _):
            zero_copy(bk).wait()
            return 0

        n_blocks = y_hbm.shape[0] // ROW_BLOCK
        lax.fori_loop(tail_ref[0], n_blocks, z_start, 0)
        lax.fori_loop(tail_ref[0], n_blocks, z_wait, 0)

    def in_copy(r):
        src = pl.ds(pl.multiple_of(start + r * ROW_BLOCK, ROW_BLOCK), ROW_BLOCK)
        return pltpu.make_async_copy(x_hbm.at[src, :], x_vmem.at[rows_of(r), :], sem_in)

    def out_copy(r):
        dst = pl.ds(pl.multiple_of(start + r * ROW_BLOCK, ROW_BLOCK), ROW_BLOCK)
        return pltpu.make_async_copy(acc.at[rows_of(r), :], y_hbm.at[dst, :], sem_out)

    def for_each_sub(fn):
        def body(r, _):
            fn(r)
            return 0
        lax.fori_loop(0, nsub, body, 0)

    @pl.when(nsub > 0)
    def _():
        @pl.when(j == 0)
        def _():
            for_each_sub(lambda r: in_copy(r).start())
            bias_rows = jnp.broadcast_to(bd_ref[...], (ROW_BLOCK, d))

            def init(r):
                acc[rows_of(r), :] = bias_rows
            for_each_sub(init)
            for_each_sub(lambda r: in_copy(r).wait())

        wg_bf[...] = wg_ref[...].astype(BF16)
        wu_bf[...] = wu_ref[...].astype(BF16)
        wd_bf[...] = wd_ref[...].astype(BF16)

        def sub(r):
            xb = x_vmem[rows_of(r), :]
            gate = jnp.dot(xb, wg_bf[...], preferred_element_type=F32) + bg_ref[...]
            up = jnp.dot(xb, wu_bf[...], preferred_element_type=F32) + bu_ref[...]
            gate = jnp.minimum(gate, SWIGLU_LIMIT)
            up = jnp.clip(up, -SWIGLU_LIMIT, SWIGLU_LIMIT)
            act = (up + 1.0) * (gate * jax.nn.sigmoid(SWIGLU_ALPHA * gate))
            acc[rows_of(r), :] += jnp.dot(act.astype(BF16), wd_bf[...], preferred_element_type=F32)
        for_each_sub(sub)

        @pl.when(j == n_f - 1)
        def _():
            for_each_sub(lambda r: out_copy(r).start())
            for_each_sub(lambda r: out_copy(r).wait())


def _moe_ffn(item_e, item_j, item_start, item_nsub, tail_block, x_rows, w_gate_up, b_gate_up, w_down, b_down,
             r_max, tf=512):
    n_rows, d = x_rows.shape
    n_e, _, two_f = w_gate_up.shape
    d_ff = two_f // 2
    n_f = d_ff // tf
    n_items = item_e.shape[0]

    def jf(i, j, ij):
        return jnp.where(ij[i] < 0, j, ij[i])

    in_specs = [
        pl.BlockSpec(memory_space=pl.ANY),
        pl.BlockSpec((None, d, tf), lambda i, j, ie, ij, ist, ins, tl: (ie[i], 0, jf(i, j, ij))),
        pl.BlockSpec((None, d, tf), lambda i, j, ie, ij, ist, ins, tl: (ie[i], 0, n_f + jf(i, j, ij))),
        pl.BlockSpec((None, 1, tf), lambda i, j, ie, ij, ist, ins, tl: (ie[i], 0, jf(i, j, ij))),
        pl.BlockSpec((None, 1, tf), lambda i, j, ie, ij, ist, ins, tl: (ie[i], 0, n_f + jf(i, j, ij))),
        pl.BlockSpec((None, tf, d), lambda i, j, ie, ij, ist, ins, tl: (ie[i], jf(i, j, ij), 0)),
        pl.BlockSpec((None, 1, d), lambda i, j, ie, ij, ist, ins, tl: (ie[i], 0, 0)),
    ]
    return pl.pallas_call(
        functools.partial(_moe_kernel, n_f=n_f),
        grid_spec=pltpu.PrefetchScalarGridSpec(
            num_scalar_prefetch=5,
            grid=(n_items, n_f),
            in_specs=in_specs,
            out_specs=pl.BlockSpec(memory_space=pl.ANY),
            scratch_shapes=[pltpu.VMEM((r_max, d), BF16), pltpu.VMEM((r_max, d), F32),
                            pltpu.VMEM((d, tf), BF16), pltpu.VMEM((d, tf), BF16),
                            pltpu.VMEM((tf, d), BF16),
                            pltpu.SemaphoreType.DMA(()), pltpu.SemaphoreType.DMA(())],
        ),
        out_shape=jax.ShapeDtypeStruct((n_rows, d), F32),
        compiler_params=_cparams(("arbitrary", "arbitrary")),
        name="moe_ffn",
    )(item_e, item_j, item_start, item_nsub, tail_block, x_rows, w_gate_up, w_gate_up,
      b_gate_up.reshape(n_e, 1, two_f), b_gate_up.reshape(n_e, 1, two_f), w_down,
      b_down.reshape(n_e, 1, d))


def _combine_kernel(dest_ref, y_hbm, x1_ref, meta_ref, g_ref, b_ref, o_ref, buf, sem, *, alpha, tm):
    base = pl.program_id(0) * (tm * TOP_K)

    def row_copy(n):
        kk = n % TOP_K
        tt = n // TOP_K
        src = dest_ref[base + n]
        return pltpu.make_async_copy(y_hbm.at[pl.ds(src, 1), :], buf.at[kk, pl.ds(tt, 1), :], sem)

    def issue(n, _):
        row_copy(n).start()
        return 0

    def drain(n, _):
        row_copy(n).wait()
        return 0

    lax.fori_loop(0, tm * TOP_K, issue, 0)
    lax.fori_loop(0, tm * TOP_K, drain, 0)
    meta = meta_ref[...]
    y = jnp.zeros(x1_ref.shape, F32)
    for kk in range(TOP_K):
        y = y + meta[:, TOP_K + kk:TOP_K + kk + 1] * buf[kk]
    o_ref[...] = _layer_norm(alpha * x1_ref[...] + y, g_ref[...], b_ref[...])


def _combine(dest_flat, y_rows, x1, meta, ln_g, ln_b, alpha, tm=128):
    t, d = x1.shape
    tm = min(tm, t)
    row = lambda w: pl.BlockSpec((tm, w), lambda i, dest: (i, 0))
    const = pl.BlockSpec((1, d), lambda i, dest: (0, 0))
    return pl.pallas_call(
        functools.partial(_combine_kernel, alpha=alpha, tm=tm),
        grid_spec=pltpu.PrefetchScalarGridSpec(
            num_scalar_prefetch=1,
            grid=(t // tm,),
            in_specs=[pl.BlockSpec(memory_space=pl.ANY), row(d), row(LANES), const, const],
            out_specs=row(d),
            scratch_shapes=[pltpu.VMEM((TOP_K, tm, d), F32), pltpu.SemaphoreType.DMA(())],
        ),
        out_shape=jax.ShapeDtypeStruct((t, d), F32),
        compiler_params=_cparams(("arbitrary",)),
        name="combine_ln",
    )(dest_flat, y_rows, x1, meta, ln_g.reshape(1, d).astype(F32), ln_b.reshape(1, d).astype(F32))


def _routing_plan(meta, counts_f, t, r_max):
    idx = meta[:, 0:TOP_K].astype(jnp.int32)
    rank = meta[:, 2 * TOP_K:3 * TOP_K].astype(jnp.int32)
    counts = counts_f[0, :N_EXPERTS].astype(jnp.int32)
    n128 = (counts + ROW_BLOCK - 1) // ROW_BLOCK
    padded = n128 * ROW_BLOCK
    pad_start = jnp.cumsum(padded) - padded
    dest = pad_start[idx] + rank
    tk = t * TOP_K
    n_rows = (tk + ROW_BLOCK - 1) // ROW_BLOCK * ROW_BLOCK + N_EXPERTS * ROW_BLOCK
    flat_tok = jnp.arange(tk, dtype=jnp.int32) // TOP_K
    row_tok = jnp.zeros((n_rows,), jnp.int32).at[dest.reshape(tk)].set(flat_tok)

    subs = r_max // ROW_BLOCK
    n_items = N_EXPERTS + (n_rows // ROW_BLOCK - N_EXPERTS) // subs
    items_e = (n128 + subs - 1) // subs
    items_end = jnp.cumsum(items_e)
    total = items_end[-1]
    slot = jnp.arange(n_items, dtype=jnp.int32)
    live = slot < total
    s_eff = jnp.minimum(slot, total - 1)
    e = jnp.minimum(jnp.searchsorted(items_end, s_eff, side='right'), N_EXPERTS - 1).astype(jnp.int32)
    local = s_eff - (items_end[e] - items_e[e])
    item_start = (pad_start[e] + local * r_max).astype(jnp.int32)
    item_nsub = jnp.where(live, jnp.clip(n128[e] - local * subs, 0, subs), 0).astype(jnp.int32)
    tail_block = jnp.sum(n128).reshape(1).astype(jnp.int32)
    return dest, row_tok, e, live, item_start, item_nsub, tail_block


def kernel(x, w_in, ret_gn_gain, w_out, ln1_gain, ln1_bias, w_router, b_router, w_gate_up, b_gate_up,
           w_down, b_down, ln2_gain, ln2_bias):
    b, s, d = x.shape
    t = b * s
    depth = w_in.shape[0]
    alpha = (2 * depth) ** 0.25
    r_max = 1280
    tf = 512
    n_f = (w_gate_up.shape[-1] // 2) // tf
    xt = x.reshape(t, d)
    for layer in range(depth):
        proj = _in_proj(xt.astype(BF16), w_in[layer].astype(BF16))
        sb = _stickbreak(proj, b, s)
        ret = _retention(proj, ret_gn_gain[layer], b, s)
        x1, meta, counts = _outproj(sb, ret, xt, w_out[layer].astype(BF16), ln1_gain[layer],
                                    ln1_bias[layer], w_router[layer], b_router[layer], alpha)
        dest, row_tok, item_e, live, item_start, item_nsub, tail_block = _routing_plan(
            meta, counts, t, r_max)
        item_j = jnp.where(live, -1, n_f - 1).astype(jnp.int32)
        x_rows = _gather_rows(row_tok, x1)
        y_rows = _moe_ffn(item_e, item_j, item_start, item_nsub, tail_block, x_rows, w_gate_up[layer],
                          b_gate_up[layer], w_down[layer], b_down[layer], r_max, tf)
        xt = _combine(dest.reshape(t * TOP_K), y_rows, x1, meta, ln2_gain[layer], ln2_bias[layer], alpha)
    return xt.reshape(b, s, d)
```

```python
import functools
import math

import jax
import jax.numpy as jnp
from jax import lax
from jax.experimental import pallas as pl
from jax.experimental.pallas import tpu as pltpu

F32 = jnp.float32
BF16 = jnp.bfloat16

HEAD_DIM = 128
SB_HEADS = 8
RET_HEADS = 8
CHUNK = 64
ROPE_BASE = 10000.0
N_EXPERTS = 32
TOP_K = 4
SWIGLU_LIMIT = 7.0
SWIGLU_ALPHA = 1.702
LN_EPS = 1e-5
GN_EPS = 1e-5

V7X_VMEM_LIMIT_BYTES = 56 * 1024 * 1024
LANES = 128
ROW_BLOCK = 128
NEG_BIG = -1e30


def _cparams(sem, vmem=V7X_VMEM_LIMIT_BYTES):
    return pltpu.CompilerParams(dimension_semantics=sem, vmem_limit_bytes=vmem)


def _matmul_kernel(x_ref, w_ref, o_ref):
    o_ref[...] = jnp.dot(x_ref[...], w_ref[...], preferred_element_type=F32).astype(o_ref.dtype)


def _in_proj(x_bf, w_bf, tm=1024, tn=1024):
    t, d = x_bf.shape
    n = w_bf.shape[1]
    tm = min(tm, t)
    return pl.pallas_call(
        _matmul_kernel,
        grid=(n // tn, t // tm),
        in_specs=[pl.BlockSpec((tm, d), lambda j, i: (i, 0)),
                  pl.BlockSpec((d, tn), lambda j, i: (0, j))],
        out_specs=pl.BlockSpec((tm, tn), lambda j, i: (i, j)),
        out_shape=jax.ShapeDtypeStruct((t, n), BF16),
        compiler_params=_cparams(("arbitrary", "arbitrary")),
        name="in_proj",
    )(x_bf, w_bf)


def _sb_kernel(q_ref, k_ref, v_ref, o_ref, *, seq, scale, group, sub):
    blk = 128
    nq = seq // blk
    row = lax.broadcasted_iota(jnp.int32, (blk, blk), 0)
    col = lax.broadcasted_iota(jnp.int32, (blk, blk), 1)
    causal = col < row
    r2 = lax.broadcasted_iota(jnp.int32, (2 * blk, 2 * blk), 0)
    c2 = lax.broadcasted_iota(jnp.int32, (2 * blk, 2 * blk), 1)
    cum_rhs = jnp.where((c2 >= blk) | ((r2 & (blk - 1)) >= c2), 1.0, 0.0).astype(BF16)

    n_sub = group // sub
    causal_sub = jnp.concatenate([causal] * sub, axis=0)
    heads_of = lambda s: range(s * sub, (s + 1) * sub)
    cols = lambda g: slice(g * blk, (g + 1) * blk)

    def block(qbs, kb, carries, accs, masked):
        ks = pl.ds(pl.multiple_of(kb * blk, blk), blk)
        zs = [jnp.concatenate(
            [lax.dot_general(qbs[g], k_ref[ks, cols(g)], (((1,), (1,)), ((), ())),
                             preferred_element_type=F32) for g in heads_of(s)], axis=0) * scale
              for s in range(n_sub)]
        sums = []
        for z in zs:
            neg_z = -z
            lnb = jnp.minimum(neg_z, 0.0) - jnp.log(1.0 + jnp.exp(jnp.minimum(z, neg_z)))
            if masked:
                lnb = jnp.where(causal_sub, lnb, 0.0)
            hi = lnb.astype(BF16)
            lo = (lnb - hi.astype(F32)).astype(BF16)
            sums.append(jnp.dot(jnp.concatenate([hi, lo], axis=1), cum_rhs, preferred_element_type=F32))
        new_carries, new_accs = [], []
        for s in range(n_sub):
            incl = sums[s][:, :blk]
            total = sums[s][:, blk:]
            w = jnp.exp(zs[s] + incl + carries[s])
            if masked:
                w = jnp.where(causal_sub, w, 0.0)
            w = w.astype(BF16)
            for n, g in enumerate(heads_of(s)):
                new_accs.append(accs[g] + jnp.dot(w[n * blk:(n + 1) * blk], v_ref[ks, cols(g)],
                                                  preferred_element_type=F32))
            new_carries.append(carries[s] + total)
        return tuple(new_carries), tuple(new_accs)

    def q_body(qi, _):
        qs = pl.ds(pl.multiple_of(qi * blk, blk), blk)
        qbs = [q_ref[qs, cols(g)] for g in range(group)]
        state = block(qbs, qi, (jnp.zeros((sub * blk, blk), F32),) * n_sub,
                      (jnp.zeros((blk, blk), F32),) * group, True)

        def kb_body(t, st):
            return block(qbs, qi - 1 - t, st[0], st[1], False)

        _, accs = lax.fori_loop(0, qi, kb_body, state)
        for g in range(group):
            o_ref[qs, cols(g)] = accs[g].astype(o_ref.dtype)
        return 0

    lax.fori_loop(0, nq, q_body, 0)


def _stickbreak(proj, batch, seq, group=8, sub=4):
    t = batch * seq
    h = SB_HEADS
    n_groups = h // group
    width = group * HEAD_DIM
    kern = functools.partial(_sb_kernel, seq=seq, scale=1.0 / math.sqrt(HEAD_DIM), group=group,
                             sub=sub)
    spec = lambda off: pl.BlockSpec((seq, width), lambda b, hg: (b, off * n_groups + hg))
    return pl.pallas_call(
        kern,
        grid=(batch, n_groups),
        in_specs=[spec(0), spec(1), spec(2)],
        out_specs=pl.BlockSpec((seq, width), lambda b, hg: (b, hg)),
        out_shape=jax.ShapeDtypeStruct((t, h * HEAD_DIM), BF16),
        compiler_params=_cparams(("arbitrary", "arbitrary")),
        name="stickbreak",
    )(proj, proj, proj)


def _ret_kernel(q_ref, k_ref, v_ref, g_ref, cos_ref, sin_ref, intra_ref, qdec_ref, kdec_ref,
                cdec_ref, gain_ref, o_ref, *, seq):
    n_chunks = seq // CHUNK
    half = HEAD_DIM // 2
    intra = intra_ref[...]
    qdec = qdec_ref[...]
    kdec = kdec_ref[...]
    cdec = cdec_ref[...]
    gain = gain_ref[...]
    k_scale = HEAD_DIM ** -0.5

    def chunk(n, state):
        rs = pl.ds(pl.multiple_of(n * CHUNK, CHUNK), CHUNK)
        q = q_ref[rs, :].astype(F32)
        k = k_ref[rs, :].astype(F32)
        v = v_ref[rs, :]
        g = g_ref[rs, :].astype(F32)
        cos = cos_ref[rs, :]
        sin = sin_ref[rs, :]
        qr = q * cos + pltpu.roll(q, half, 1) * sin
        kr = (k * cos + pltpu.roll(k, half, 1) * sin) * k_scale
        scores = lax.dot_general(qr.astype(BF16), kr.astype(BF16), (((1,), (1,)), ((), ())),
                                 preferred_element_type=F32) * intra
        o = jnp.dot(scores.astype(BF16), v, preferred_element_type=F32)
        o = o + jnp.dot((qr * qdec).astype(BF16), state.astype(BF16), preferred_element_type=F32)
        kv = lax.dot_general((kr * kdec).astype(BF16), v, (((0,), (0,)), ((), ())),
                             preferred_element_type=F32)
        state = state * cdec + kv
        mu = jnp.mean(o, axis=-1, keepdims=True)
        var = jnp.mean(jnp.square(o - mu), axis=-1, keepdims=True)
        on = (o - mu) * lax.rsqrt(var + GN_EPS)
        out = on * gain * (g * jax.nn.sigmoid(g))
        o_ref[rs, :] = out.astype(o_ref.dtype)
        return state

    lax.fori_loop(0, n_chunks, chunk, jnp.zeros((HEAD_DIM, HEAD_DIM), F32))


def _retention_tables(seq):
    d = HEAD_DIM
    inv_freq = ROPE_BASE ** (-jnp.arange(0, d, 2, dtype=F32) / d)
    ang = jnp.arange(seq, dtype=F32)[:, None] * inv_freq[None, :]
    cos, sin = jnp.cos(ang), jnp.sin(ang)
    cos_full = jnp.concatenate([cos, cos], axis=-1)
    sin_signed = jnp.concatenate([-sin, sin], axis=-1)
    log_gamma = jnp.log1p(-jnp.exp2(-5.0 - jnp.arange(RET_HEADS, dtype=F32)))
    i = jnp.arange(CHUNK, dtype=F32)
    intra = jnp.exp(log_gamma[:, None, None] * jnp.abs(i[:, None] - i[None, :]))
    k_decay = jnp.exp(log_gamma[:, None] * (CHUNK - 1 - i))
    q_decay = jnp.exp(log_gamma[:, None] * (i + 1.0))
    c_decay = jnp.exp(log_gamma * CHUNK)
    bc = lambda a: jnp.broadcast_to(a[..., None], a.shape + (d,))
    return cos_full, sin_signed, intra, bc(q_decay), bc(k_decay), bc(c_decay[:, None])


def _retention(proj, gn_gain, batch, seq):
    t = batch * seq
    h = RET_HEADS
    base = 3 * SB_HEADS
    cos_full, sin_signed, intra, qdec, kdec, cdec = _retention_tables(seq)
    spec = lambda off: pl.BlockSpec((seq, HEAD_DIM), lambda b, hh: (b, base + off + hh))
    full = pl.BlockSpec((seq, HEAD_DIM), lambda b, hh: (0, 0))
    per_head = lambda r, c: pl.BlockSpec((None, r, c), lambda b, hh: (hh, 0, 0))
    return pl.pallas_call(
        functools.partial(_ret_kernel, seq=seq),
        grid=(batch, h),
        in_specs=[spec(0), spec(h), spec(2 * h), spec(3 * h), full, full,
                  per_head(CHUNK, CHUNK), per_head(CHUNK, HEAD_DIM), per_head(CHUNK, HEAD_DIM),
                  per_head(1, HEAD_DIM),
                  pl.BlockSpec((1, HEAD_DIM), lambda b, hh: (0, hh))],
        out_specs=pl.BlockSpec((seq, HEAD_DIM), lambda b, hh: (b, hh)),
        out_shape=jax.ShapeDtypeStruct((t, h * HEAD_DIM), BF16),
        compiler_params=_cparams(("arbitrary", "arbitrary")),
        name="retention",
    )(proj, proj, proj, proj, cos_full, sin_signed, intra, qdec, kdec, cdec,
      gn_gain.reshape(1, h * HEAD_DIM).astype(F32))


def _layer_norm(hid, gain, bias):
    mu = jnp.mean(hid, axis=-1, keepdims=True)
    cen = hid - mu
    var = jnp.mean(jnp.square(cen), axis=-1, keepdims=True)
    return cen * lax.rsqrt(var + LN_EPS) * gain + bias


def _to_slabs(slab_ref, base, rows, value):
    chunks = value.shape[1] // LANES
    for c in range(chunks):
        slab_ref[pl.ds(base + c, rows, stride=chunks), :] = value[:, c * LANES:(c + 1) * LANES]


def _from_slabs(slab_ref, base, rows, chunks):
    return jnp.concatenate(
        [slab_ref[pl.ds(base + c, rows, stride=chunks), :] for c in range(chunks)], axis=1)


def _outproj_kernel(sb_ref, ret_ref, x_ref, w_ref, g_ref, b_ref, wr_ref, br_ref,
                    x1_ref, x1s_ref, meta_ref, cnt_ref, *, alpha, sb_width):
    tm = x_ref.shape[0]
    mix = jnp.dot(sb_ref[...], w_ref[:sb_width, :], preferred_element_type=F32)
    mix = mix + jnp.dot(ret_ref[...], w_ref[sb_width:, :], preferred_element_type=F32)
    x1 = _layer_norm(alpha * x_ref[...] + mix, g_ref[...], b_ref[...])
    x1_ref[...] = x1
    _to_slabs(x1s_ref, 0, tm, x1)

    logits = jnp.dot(x1, wr_ref[...], preferred_element_type=F32,
                     precision=lax.Precision.HIGHEST) + br_ref[...]
    lane = lax.broadcasted_iota(jnp.int32, (tm, LANES), 1).astype(F32)
    vals = logits
    tops, idxs, hots = [], [], []
    for _ in range(TOP_K):
        m = jnp.max(vals, axis=-1, keepdims=True)
        idx = jnp.min(jnp.where(vals == m, lane, float(LANES)), axis=-1, keepdims=True)
        hot = lane == idx
        vals = jnp.where(hot, NEG_BIG * 2.0, vals)
        tops.append(m)
        idxs.append(idx)
        hots.append(hot)
    exps = [jnp.exp(m - tops[0]) for m in tops]
    denom = exps[0] + exps[1] + exps[2] + exps[3]
    gates = [e / denom for e in exps]

    @pl.when(pl.program_id(0) == 0)
    def _():
        cnt_ref[...] = jnp.zeros_like(cnt_ref)

    multi = jnp.zeros((tm, LANES), F32)
    for hot in hots:
        multi = multi + jnp.where(hot, 1.0, 0.0)
    r = lax.broadcasted_iota(jnp.int32, (tm, tm), 0)
    c = lax.broadcasted_iota(jnp.int32, (tm, tm), 1)
    strict_lower = jnp.where(c < r, 1.0, 0.0).astype(BF16)
    before = jnp.dot(strict_lower, multi.astype(BF16), preferred_element_type=F32) + cnt_ref[...]
    cnt_ref[...] = cnt_ref[...] + jnp.sum(multi, axis=0, keepdims=True)

    meta = jnp.zeros((tm, LANES), F32)
    for kk in range(TOP_K):
        rank = jnp.sum(jnp.where(hots[kk], before, 0.0), axis=-1, keepdims=True)
        meta = jnp.where(lane == kk, idxs[kk], meta)
        meta = jnp.where(lane == TOP_K + kk, gates[kk], meta)
        meta = jnp.where(lane == 2 * TOP_K + kk, rank, meta)
    meta_ref[...] = meta


def _outproj(sb, ret, xt, w_out_bf, ln_g, ln_b, w_router, b_router, alpha, tm=256):
    t, d = xt.shape
    tm = min(tm, t)
    sbw = sb.shape[1]
    wr = jnp.zeros((d, LANES), F32).at[:, :N_EXPERTS].set(w_router.astype(F32))
    br = jnp.full((1, LANES), NEG_BIG, F32).at[0, :N_EXPERTS].set(b_router.astype(F32))
    row = lambda w: pl.BlockSpec((tm, w), lambda i: (i, 0))
    const = lambda r, c: pl.BlockSpec((r, c), lambda i: (0, 0))
    return pl.pallas_call(
        functools.partial(_outproj_kernel, alpha=alpha, sb_width=sbw),
        grid=(t // tm,),
        in_specs=[row(sbw), row(ret.shape[1]), row(d), const(d, d), const(1, d), const(1, d),
                  const(d, LANES), const(1, LANES)],
        out_specs=[row(d), pl.BlockSpec((tm * (d // LANES), LANES), lambda i: (i, 0)), row(LANES),
                   const(1, LANES)],
        out_shape=[jax.ShapeDtypeStruct((t, d), F32),
                   jax.ShapeDtypeStruct((t * (d // LANES), LANES), F32),
                   jax.ShapeDtypeStruct((t, LANES), F32),
                   jax.ShapeDtypeStruct((1, LANES), F32)],
        compiler_params=_cparams(("arbitrary",)),
        name="outproj_ln_router",
    )(sb, ret, xt, w_out_bf, ln_g.reshape(1, d).astype(F32), ln_b.reshape(1, d).astype(F32), wr, br)


DMA_UNROLL = 8


def _scatter_kernel(dest_ref, zb_ref, x_hbm, o_hbm, zbuf, zsem, sem, *, per_step, chunks):
    i = pl.program_id(0)
    block_rows = ROW_BLOCK * chunks

    def zero_copy(n):
        dst = pl.ds(pl.multiple_of(zb_ref[n] * block_rows, block_rows), block_rows)
        return pltpu.make_async_copy(zbuf, o_hbm.at[dst, :], zsem)

    def when_listed(n, fn):
        @pl.when(zb_ref[n] >= 0)
        def _():
            fn(n)

    def row_copy(n, slot):
        src = pl.ds(pl.multiple_of((n // TOP_K) * chunks, chunks), chunks)
        dst = pl.ds(pl.multiple_of(dest_ref[n] * chunks, chunks), chunks)
        return pltpu.make_async_copy(x_hbm.at[src, :], o_hbm.at[dst, :], sem.at[slot])

    def loop(n, fn):
        def body(m, _):
            fn(m)
            return 0
        lax.fori_loop(0, n, body, 0, unroll=DMA_UNROLL)

    def wait_step(slot):
        loop(per_step, lambda n: row_copy(0, slot).wait())

    @pl.when(i == 0)
    def _():
        zbuf[...] = jnp.zeros_like(zbuf)
        n_zero = zb_ref.shape[0]
        loop(n_zero, lambda n: when_listed(n, lambda m: zero_copy(m).start()))
        loop(n_zero, lambda n: when_listed(n, lambda m: zero_copy(m).wait()))

    slot = i % 2
    loop(per_step, lambda n: row_copy(i * per_step + n, slot).start())

    @pl.when(i > 0)
    def _():
        wait_step(1 - slot)

    @pl.when(i == pl.num_programs(0) - 1)
    def _():
        wait_step(slot)


def _scatter_rows(dest_flat, zero_blocks, x1_slabs, n_rows, chunks, per_step=1024):
    n_assign = dest_flat.shape[0]
    per_step = min(per_step, n_assign)
    return pl.pallas_call(
        functools.partial(_scatter_kernel, per_step=per_step, chunks=chunks),
        grid_spec=pltpu.PrefetchScalarGridSpec(
            num_scalar_prefetch=2,
            grid=(n_assign // per_step,),
            in_specs=[pl.BlockSpec(memory_space=pl.ANY)],
            out_specs=pl.BlockSpec(memory_space=pl.ANY),
            scratch_shapes=[pltpu.VMEM((ROW_BLOCK * chunks, LANES), F32),
                            pltpu.SemaphoreType.DMA(()), pltpu.SemaphoreType.DMA((2,))],
        ),
        out_shape=jax.ShapeDtypeStruct((n_rows * chunks, LANES), F32),
        compiler_params=_cparams(("arbitrary",)),
        name="scatter_rows",
    )(dest_flat, zero_blocks, x1_slabs)


def _moe_kernel(ie_ref, ij_ref, ist_ref, ins_ref, tail_ref, x_hbm, wg_ref, wu_ref, bg_ref, bu_ref,
                wd_ref, bd_ref, y_hbm, x_vmem, acc, wg_bf, wu_bf, wd_bf, stage, sem_in, sem_out, *, n_f):
    del ie_ref, ij_ref
    i = pl.program_id(0)
    j = pl.program_id(1)
    nsub = ins_ref[i]
    start = ist_ref[i]
    d = acc.shape[1]
    chunks = d // LANES
    block_rows = ROW_BLOCK * chunks

    def rows_of(r):
        return pl.ds(pl.multiple_of(r * ROW_BLOCK, ROW_BLOCK), ROW_BLOCK)

    def slabs_of(block):
        return pl.ds(pl.multiple_of(block * block_rows, block_rows), block_rows)

    def for_each(lo, hi, fn):
        def body(r, _):
            fn(r)
            return 0
        lax.fori_loop(lo, hi, body, 0)

    @pl.when((i == 0) & (j == 0))
    def _():
        stage[0] = jnp.zeros(stage.shape[1:], F32)

        def zero_copy(bk):
            return pltpu.make_async_copy(stage.at[0], y_hbm.at[slabs_of(bk), :], sem_out.at[0])

        n_blocks = y_hbm.shape[0] // block_rows
        for_each(tail_ref[0], n_blocks, lambda bk: zero_copy(bk).start())
        for_each(tail_ref[0], n_blocks, lambda bk: zero_copy(bk).wait())

    first_block = start // ROW_BLOCK

    def in_copy(r, slot):
        return pltpu.make_async_copy(x_hbm.at[slabs_of(first_block + r), :], stage.at[slot],
                                     sem_in.at[slot])

    def out_copy(r, slot):
        return pltpu.make_async_copy(stage.at[slot], y_hbm.at[slabs_of(first_block + r), :],
                                     sem_out.at[slot])

    def hidden(r):
        xb = x_vmem[rows_of(r), :]
        gate = jnp.dot(xb, wg_bf[...], preferred_element_type=F32) + bg_ref[...]
        up = jnp.dot(xb, wu_bf[...], preferred_element_type=F32) + bu_ref[...]
        gate = jnp.minimum(gate, SWIGLU_LIMIT)
        up = jnp.clip(up, -SWIGLU_LIMIT, SWIGLU_LIMIT)
        act = (up + 1.0) * (gate * jax.nn.sigmoid(SWIGLU_ALPHA * gate))
        return jnp.dot(act.astype(BF16), wd_bf[...], preferred_element_type=F32)

    @pl.when(nsub > 0)
    def _():
        @pl.when(j == 0)
        def _():
            in_copy(0, 0).start()
            bias_rows = jnp.broadcast_to(bd_ref[...], (ROW_BLOCK, d))

            def load(r):
                slot = r % 2

                @pl.when(r + 1 < nsub)
                def _():
                    in_copy(r + 1, 1 - slot).start()
                in_copy(r, slot).wait()
                x_vmem[rows_of(r), :] = _from_slabs(stage.at[slot], 0, ROW_BLOCK, chunks).astype(BF16)
                acc[rows_of(r), :] = bias_rows
            for_each(0, nsub, load)

        wg_bf[...] = wg_ref[...].astype(BF16)
        wu_bf[...] = wu_ref[...].astype(BF16)
        wd_bf[...] = wd_ref[...].astype(BF16)

        @pl.when(j < n_f - 1)
        def _():
            def sub(r):
                acc[rows_of(r), :] += hidden(r)
            for_each(0, nsub, sub)

        @pl.when(j == n_f - 1)
        def _():
            def sub(r):
                slot = r % 2
                final = acc[rows_of(r), :] + hidden(r)

                @pl.when(r >= 2)
                def _():
                    out_copy(r - 2, slot).wait()
                _to_slabs(stage.at[slot], 0, ROW_BLOCK, final)
                out_copy(r, slot).start()
            for_each(0, nsub, sub)

            @pl.when(nsub >= 2)
            def _():
                out_copy(nsub - 2, nsub % 2).wait()
            out_copy(nsub - 1, (nsub - 1) % 2).wait()


def _moe_ffn(item_e, item_j, item_start, item_nsub, tail_block, x_rows, w_gate_up, b_gate_up, w_down, b_down,
             r_max, tf=512):
    n_e, d, two_f = w_gate_up.shape
    chunks = d // LANES
    d_ff = two_f // 2
    n_f = d_ff // tf
    n_items = item_e.shape[0]

    def jf(i, j, ij):
        return jnp.where(ij[i] < 0, j, ij[i])

    in_specs = [
        pl.BlockSpec(memory_space=pl.ANY),
        pl.BlockSpec((None, d, tf), lambda i, j, ie, ij, ist, ins, tl: (ie[i], 0, jf(i, j, ij))),
        pl.BlockSpec((None, d, tf), lambda i, j, ie, ij, ist, ins, tl: (ie[i], 0, n_f + jf(i, j, ij))),
        pl.BlockSpec((None, 1, tf), lambda i, j, ie, ij, ist, ins, tl: (ie[i], 0, jf(i, j, ij))),
        pl.BlockSpec((None, 1, tf), lambda i, j, ie, ij, ist, ins, tl: (ie[i], 0, n_f + jf(i, j, ij))),
        pl.BlockSpec((None, tf, d), lambda i, j, ie, ij, ist, ins, tl: (ie[i], jf(i, j, ij), 0)),
        pl.BlockSpec((None, 1, d), lambda i, j, ie, ij, ist, ins, tl: (ie[i], 0, 0)),
    ]
    return pl.pallas_call(
        functools.partial(_moe_kernel, n_f=n_f),
        grid_spec=pltpu.PrefetchScalarGridSpec(
            num_scalar_prefetch=5,
            grid=(n_items, n_f),
            in_specs=in_specs,
            out_specs=pl.BlockSpec(memory_space=pl.ANY),
            scratch_shapes=[pltpu.VMEM((r_max, d), BF16), pltpu.VMEM((r_max, d), F32),
                            pltpu.VMEM((d, tf), BF16), pltpu.VMEM((d, tf), BF16),
                            pltpu.VMEM((tf, d), BF16),
                            pltpu.VMEM((2, ROW_BLOCK * chunks, LANES), F32),
                            pltpu.SemaphoreType.DMA((2,)), pltpu.SemaphoreType.DMA((2,))],
        ),
        out_shape=jax.ShapeDtypeStruct(x_rows.shape, F32),
        compiler_params=_cparams(("arbitrary", "arbitrary")),
        name="moe_ffn",
    )(item_e, item_j, item_start, item_nsub, tail_block, x_rows, w_gate_up, w_gate_up,
      b_gate_up.reshape(n_e, 1, two_f), b_gate_up.reshape(n_e, 1, two_f), w_down,
      b_down.reshape(n_e, 1, d))


def _combine_kernel(dest_ref, y_hbm, x1_ref, meta_ref, g_ref, b_ref, o_ref, buf, sem, *, alpha, tm, chunks):
    i = pl.program_id(0)
    per_step = tm * TOP_K

    def row_copy(step, slot, n):
        src = pl.ds(pl.multiple_of(dest_ref[step * per_step + n] * chunks, chunks), chunks)
        dst = pl.ds(pl.multiple_of(((n % TOP_K) * tm + n // TOP_K) * chunks, chunks), chunks)
        return pltpu.make_async_copy(y_hbm.at[src, :], buf.at[slot, dst, :], sem.at[slot])

    def issue_tile(step, slot):
        def body(n, _):
            row_copy(step, slot, n).start()
            return 0
        lax.fori_loop(0, per_step, body, 0, unroll=DMA_UNROLL)

    def wait_tile(slot):
        def body(n, _):
            pltpu.make_async_copy(y_hbm.at[pl.ds(0, chunks), :], buf.at[slot, pl.ds(0, chunks), :],
                                  sem.at[slot]).wait()
            return 0
        lax.fori_loop(0, per_step, body, 0, unroll=DMA_UNROLL)

    @pl.when(i == 0)
    def _():
        issue_tile(0, 0)

    @pl.when(i + 1 < pl.num_programs(0))
    def _():
        issue_tile(i + 1, (i + 1) % 2)

    slot = i % 2
    wait_tile(slot)
    meta = meta_ref[...]
    y = jnp.zeros(x1_ref.shape, F32)
    for kk in range(TOP_K):
        rows = _from_slabs(buf.at[slot], kk * tm * chunks, tm, chunks)
        y = y + meta[:, TOP_K + kk:TOP_K + kk + 1] * rows
    o_ref[...] = _layer_norm(alpha * x1_ref[...] + y, g_ref[...], b_ref[...])


def _combine(dest_flat, y_rows, x1, meta, ln_g, ln_b, alpha, tm=128):
    t, d = x1.shape
    tm = min(tm, t)
    chunks = d // LANES
    row = lambda w: pl.BlockSpec((tm, w), lambda i, dest: (i, 0))
    const = pl.BlockSpec((1, d), lambda i, dest: (0, 0))
    return pl.pallas_call(
        functools.partial(_combine_kernel, alpha=alpha, tm=tm, chunks=chunks),
        grid_spec=pltpu.PrefetchScalarGridSpec(
            num_scalar_prefetch=1,
            grid=(t // tm,),
            in_specs=[pl.BlockSpec(memory_space=pl.ANY), row(d), row(LANES), const, const],
            out_specs=row(d),
            scratch_shapes=[pltpu.VMEM((2, TOP_K * tm * chunks, LANES), F32),
                            pltpu.SemaphoreType.DMA((2,))],
        ),
        out_shape=jax.ShapeDtypeStruct((t, d), F32),
        compiler_params=_cparams(("arbitrary",)),
        name="combine_ln",
    )(dest_flat, y_rows, x1, meta, ln_g.reshape(1, d).astype(F32), ln_b.reshape(1, d).astype(F32))


def _routing_plan(meta, counts_f, t, r_max):
    idx = meta[:, 0:TOP_K].astype(jnp.int32)
    rank = meta[:, 2 * TOP_K:3 * TOP_K].astype(jnp.int32)
    counts = counts_f[0, :N_EXPERTS].astype(jnp.int32)
    n128 = (counts + ROW_BLOCK - 1) // ROW_BLOCK
    padded = n128 * ROW_BLOCK
    pad_start = jnp.cumsum(padded) - padded
    dest = pad_start[idx] + rank
    tk = t * TOP_K
    n_rows = (tk + ROW_BLOCK - 1) // ROW_BLOCK * ROW_BLOCK + N_EXPERTS * ROW_BLOCK
    n_blocks = n_rows // ROW_BLOCK
    block_end = jnp.cumsum(n128)
    tail = block_end[-1] + jnp.arange(N_EXPERTS, dtype=jnp.int32)
    zero_blocks = jnp.concatenate([jnp.where(n128 > 0, block_end - 1, -1),
                                   jnp.where(tail < n_blocks, tail, -1)]).astype(jnp.int32)

    subs = r_max // ROW_BLOCK
    n_items = N_EXPERTS + (n_rows // ROW_BLOCK - N_EXPERTS) // subs
    items_e = (n128 + subs - 1) // subs
    items_end = jnp.cumsum(items_e)
    total = items_end[-1]
    slot = jnp.arange(n_items, dtype=jnp.int32)
    live = slot < total
    s_eff = jnp.minimum(slot, total - 1)
    e = jnp.minimum(jnp.searchsorted(items_end, s_eff, side='right'), N_EXPERTS - 1).astype(jnp.int32)
    local = s_eff - (items_end[e] - items_e[e])
    item_start = (pad_start[e] + local * r_max).astype(jnp.int32)
    item_nsub = jnp.where(live, jnp.clip(n128[e] - local * subs, 0, subs), 0).astype(jnp.int32)
    tail_block = jnp.sum(n128).reshape(1).astype(jnp.int32)
    return dest, zero_blocks, n_rows, e, live, item_start, item_nsub, tail_block


def kernel(x, w_in, ret_gn_gain, w_out, ln1_gain, ln1_bias, w_router, b_router, w_gate_up, b_gate_up,
           w_down, b_down, ln2_gain, ln2_bias):
    b, s, d = x.shape
    t = b * s
    depth = w_in.shape[0]
    alpha = (2 * depth) ** 0.25
    r_max = 1280
    tf = 512
    n_f = (w_gate_up.shape[-1] // 2) // tf
    xt = x.reshape(t, d)
    for layer in range(depth):
        proj = _in_proj(xt.astype(BF16), w_in[layer].astype(BF16))
        sb = _stickbreak(proj, b, s)
        ret = _retention(proj, ret_gn_gain[layer], b, s)
        x1, x1_slabs, meta, counts = _outproj(sb, ret, xt, w_out[layer].astype(BF16), ln1_gain[layer],
                                              ln1_bias[layer], w_router[layer], b_router[layer], alpha)
        dest, zero_blocks, n_rows, item_e, live, item_start, item_nsub, tail_block = _routing_plan(
            meta, counts, t, r_max)
        dest_flat = dest.reshape(t * TOP_K)
        item_j = jnp.where(live, -1, n_f - 1).astype(jnp.int32)
        x_rows = _scatter_rows(dest_flat, zero_blocks, x1_slabs, n_rows, d // LANES)
        y_rows = _moe_ffn(item_e, item_j, item_start, item_nsub, tail_block, x_rows, w_gate_up[layer],
                          b_gate_up[layer], w_down[layer], b_down[layer], r_max, tf)
        xt = _combine(dest_flat, y_rows, x1, meta, ln2_gain[layer], ln2_bias[layer], alpha)
    return xt.reshape(b, s, d)
```

```python
import functools
import math

import jax
import jax.numpy as jnp
from jax import lax
from jax.experimental import pallas as pl
from jax.experimental.pallas import tpu as pltpu

F32 = jnp.float32
BF16 = jnp.bfloat16

HEAD_DIM = 128
SB_HEADS = 8
RET_HEADS = 8
CHUNK = 64
ROPE_BASE = 10000.0
N_EXPERTS = 32
TOP_K = 4
SWIGLU_LIMIT = 7.0
SWIGLU_ALPHA = 1.702
LN_EPS = 1e-5
GN_EPS = 1e-5

V7X_VMEM_LIMIT_BYTES = 56 * 1024 * 1024
LANES = 128
ROW_BLOCK = 128
NEG_BIG = -1e30


def _cparams(sem, vmem=V7X_VMEM_LIMIT_BYTES):
    return pltpu.CompilerParams(dimension_semantics=sem, vmem_limit_bytes=vmem)


def _matmul_kernel(x_ref, w_ref, o_ref):
    o_ref[...] = jnp.dot(x_ref[...], w_ref[...], preferred_element_type=F32).astype(o_ref.dtype)


def _in_proj(x_bf, w_bf, tm=1024, tn=1024):
    t, d = x_bf.shape
    n = w_bf.shape[1]
    tm = min(tm, t)
    return pl.pallas_call(
        _matmul_kernel,
        grid=(n // tn, t // tm),
        in_specs=[pl.BlockSpec((tm, d), lambda j, i: (i, 0)),
                  pl.BlockSpec((d, tn), lambda j, i: (0, j))],
        out_specs=pl.BlockSpec((tm, tn), lambda j, i: (i, j)),
        out_shape=jax.ShapeDtypeStruct((t, n), BF16),
        compiler_params=_cparams(("arbitrary", "arbitrary")),
        name="in_proj",
    )(x_bf, w_bf)


def _sb_kernel(q_ref, k_ref, v_ref, o_ref, *, seq, scale, group, sub):
    blk = 128
    nq = seq // blk
    row = lax.broadcasted_iota(jnp.int32, (blk, blk), 0)
    col = lax.broadcasted_iota(jnp.int32, (blk, blk), 1)
    causal = col < row
    r2 = lax.broadcasted_iota(jnp.int32, (2 * blk, 2 * blk), 0)
    c2 = lax.broadcasted_iota(jnp.int32, (2 * blk, 2 * blk), 1)
    cum_rhs = jnp.where((c2 >= blk) | ((r2 & (blk - 1)) >= c2), 1.0, 0.0).astype(BF16)

    n_sub = group // sub
    causal_sub = jnp.concatenate([causal] * sub, axis=0)
    heads_of = lambda s: range(s * sub, (s + 1) * sub)
    cols = lambda g: slice(g * blk, (g + 1) * blk)

    def block(qbs, kb, carries, accs, masked):
        ks = pl.ds(pl.multiple_of(kb * blk, blk), blk)
        zs = [jnp.concatenate(
            [lax.dot_general(qbs[g], k_ref[ks, cols(g)], (((1,), (1,)), ((), ())),
                             preferred_element_type=F32) for g in heads_of(s)], axis=0) * scale
              for s in range(n_sub)]
        sums = []
        for z in zs:
            neg_z = -z
            lnb = jnp.minimum(neg_z, 0.0) - jnp.log(1.0 + jnp.exp(jnp.minimum(z, neg_z)))
            if masked:
                lnb = jnp.where(causal_sub, lnb, 0.0)
            hi = lnb.astype(BF16)
            lo = (lnb - hi.astype(F32)).astype(BF16)
            sums.append(jnp.dot(jnp.concatenate([hi, lo], axis=1), cum_rhs, preferred_element_type=F32))
        new_carries, new_accs = [], []
        for s in range(n_sub):
            incl = sums[s][:, :blk]
            total = sums[s][:, blk:]
            w = jnp.exp(zs[s] + incl + carries[s])
            if masked:
                w = jnp.where(causal_sub, w, 0.0)
            w = w.astype(BF16)
            for n, g in enumerate(heads_of(s)):
                new_accs.append(accs[g] + jnp.dot(w[n * blk:(n + 1) * blk], v_ref[ks, cols(g)],
                                                  preferred_element_type=F32))
            new_carries.append(carries[s] + total)
        return tuple(new_carries), tuple(new_accs)

    def q_body(qi, _):
        qs = pl.ds(pl.multiple_of(qi * blk, blk), blk)
        qbs = [q_ref[qs, cols(g)] for g in range(group)]
        state = block(qbs, qi, (jnp.zeros((sub * blk, blk), F32),) * n_sub,
                      (jnp.zeros((blk, blk), F32),) * group, True)

        def kb_body(t, st):
            return block(qbs, qi - 1 - t, st[0], st[1], False)

        _, accs = lax.fori_loop(0, qi, kb_body, state)
        for g in range(group):
            o_ref[qs, cols(g)] = accs[g].astype(o_ref.dtype)
        return 0

    lax.fori_loop(0, nq, q_body, 0)


def _stickbreak(proj, batch, seq, group=8, sub=4):
    t = batch * seq
    h = SB_HEADS
    n_groups = h // group
    width = group * HEAD_DIM
    kern = functools.partial(_sb_kernel, seq=seq, scale=1.0 / math.sqrt(HEAD_DIM), group=group,
                             sub=sub)
    spec = lambda off: pl.BlockSpec((seq, width), lambda b, hg: (b, off * n_groups + hg))
    return pl.pallas_call(
        kern,
        grid=(batch, n_groups),
        in_specs=[spec(0), spec(1), spec(2)],
        out_specs=pl.BlockSpec((seq, width), lambda b, hg: (b, hg)),
        out_shape=jax.ShapeDtypeStruct((t, h * HEAD_DIM), BF16),
        compiler_params=_cparams(("arbitrary", "arbitrary")),
        name="stickbreak",
    )(proj, proj, proj)


def _ret_kernel(q_ref, k_ref, v_ref, g_ref, cos_ref, sin_ref, intra_ref, qdec_ref, kdec_ref,
                cdec_ref, gain_ref, o_ref, *, seq):
    n_chunks = seq // CHUNK
    half = HEAD_DIM // 2
    intra = intra_ref[...]
    qdec = qdec_ref[...]
    kdec = kdec_ref[...]
    cdec = cdec_ref[...]
    gain = gain_ref[...]
    k_scale = HEAD_DIM ** -0.5

    def chunk(n, state):
        rs = pl.ds(pl.multiple_of(n * CHUNK, CHUNK), CHUNK)
        q = q_ref[rs, :].astype(F32)
        k = k_ref[rs, :].astype(F32)
        v = v_ref[rs, :]
        g = g_ref[rs, :].astype(F32)
        cos = cos_ref[rs, :]
        sin = sin_ref[rs, :]
        qr = q * cos + pltpu.roll(q, half, 1) * sin
        kr = (k * cos + pltpu.roll(k, half, 1) * sin) * k_scale
        scores = lax.dot_general(qr.astype(BF16), kr.astype(BF16), (((1,), (1,)), ((), ())),
                                 preferred_element_type=F32) * intra
        o = jnp.dot(scores.astype(BF16), v, preferred_element_type=F32)
        o = o + jnp.dot((qr * qdec).astype(BF16), state.astype(BF16), preferred_element_type=F32)
        kv = lax.dot_general((kr * kdec).astype(BF16), v, (((0,), (0,)), ((), ())),
                             preferred_element_type=F32)
        state = state * cdec + kv
        mu = jnp.mean(o, axis=-1, keepdims=True)
        var = jnp.mean(jnp.square(o - mu), axis=-1, keepdims=True)
        on = (o - mu) * lax.rsqrt(var + GN_EPS)
        out = on * gain * (g * jax.nn.sigmoid(g))
        o_ref[rs, :] = out.astype(o_ref.dtype)
        return state

    lax.fori_loop(0, n_chunks, chunk, jnp.zeros((HEAD_DIM, HEAD_DIM), F32))


def _retention_tables(seq):
    d = HEAD_DIM
    inv_freq = ROPE_BASE ** (-jnp.arange(0, d, 2, dtype=F32) / d)
    ang = jnp.arange(seq, dtype=F32)[:, None] * inv_freq[None, :]
    cos, sin = jnp.cos(ang), jnp.sin(ang)
    cos_full = jnp.concatenate([cos, cos], axis=-1)
    sin_signed = jnp.concatenate([-sin, sin], axis=-1)
    log_gamma = jnp.log1p(-jnp.exp2(-5.0 - jnp.arange(RET_HEADS, dtype=F32)))
    i = jnp.arange(CHUNK, dtype=F32)
    intra = jnp.exp(log_gamma[:, None, None] * jnp.abs(i[:, None] - i[None, :]))
    k_decay = jnp.exp(log_gamma[:, None] * (CHUNK - 1 - i))
    q_decay = jnp.exp(log_gamma[:, None] * (i + 1.0))
    c_decay = jnp.exp(log_gamma * CHUNK)
    bc = lambda a: jnp.broadcast_to(a[..., None], a.shape + (d,))
    return cos_full, sin_signed, intra, bc(q_decay), bc(k_decay), bc(c_decay[:, None])


def _retention(proj, gn_gain, batch, seq):
    t = batch * seq
    h = RET_HEADS
    base = 3 * SB_HEADS
    cos_full, sin_signed, intra, qdec, kdec, cdec = _retention_tables(seq)
    spec = lambda off: pl.BlockSpec((seq, HEAD_DIM), lambda b, hh: (b, base + off + hh))
    full = pl.BlockSpec((seq, HEAD_DIM), lambda b, hh: (0, 0))
    per_head = lambda r, c: pl.BlockSpec((None, r, c), lambda b, hh: (hh, 0, 0))
    return pl.pallas_call(
        functools.partial(_ret_kernel, seq=seq),
        grid=(batch, h),
        in_specs=[spec(0), spec(h), spec(2 * h), spec(3 * h), full, full,
                  per_head(CHUNK, CHUNK), per_head(CHUNK, HEAD_DIM), per_head(CHUNK, HEAD_DIM),
                  per_head(1, HEAD_DIM),
                  pl.BlockSpec((1, HEAD_DIM), lambda b, hh: (0, hh))],
        out_specs=pl.BlockSpec((seq, HEAD_DIM), lambda b, hh: (b, hh)),
        out_shape=jax.ShapeDtypeStruct((t, h * HEAD_DIM), BF16),
        compiler_params=_cparams(("arbitrary", "arbitrary")),
        name="retention",
    )(proj, proj, proj, proj, cos_full, sin_signed, intra, qdec, kdec, cdec,
      gn_gain.reshape(1, h * HEAD_DIM).astype(F32))


def _layer_norm(hid, gain, bias):
    mu = jnp.mean(hid, axis=-1, keepdims=True)
    cen = hid - mu
    var = jnp.mean(jnp.square(cen), axis=-1, keepdims=True)
    return cen * lax.rsqrt(var + LN_EPS) * gain + bias


def _to_slabs(slab_ref, base, rows, value):
    chunks = value.shape[1] // LANES
    for c in range(chunks):
        slab_ref[pl.ds(base + c, rows, stride=chunks), :] = value[:, c * LANES:(c + 1) * LANES]


def _from_slabs(slab_ref, base, rows, chunks):
    return jnp.concatenate(
        [slab_ref[pl.ds(base + c, rows, stride=chunks), :] for c in range(chunks)], axis=1)


def _outproj_kernel(sb_ref, ret_ref, x_ref, w_ref, g_ref, b_ref, wr_ref, br_ref,
                    x1_ref, x1s_ref, meta_ref, cnt_ref, *, alpha, sb_width):
    tm = x_ref.shape[0]
    mix = jnp.dot(sb_ref[...], w_ref[:sb_width, :], preferred_element_type=F32)
    mix = mix + jnp.dot(ret_ref[...], w_ref[sb_width:, :], preferred_element_type=F32)
    x1 = _layer_norm(alpha * x_ref[...] + mix, g_ref[...], b_ref[...])
    x1_ref[...] = x1
    _to_slabs(x1s_ref, 0, tm, x1)

    logits = jnp.dot(x1, wr_ref[...], preferred_element_type=F32,
                     precision=lax.Precision.HIGHEST) + br_ref[...]
    lane = lax.broadcasted_iota(jnp.int32, (tm, LANES), 1).astype(F32)
    vals = logits
    tops, idxs, hots = [], [], []
    for _ in range(TOP_K):
        m = jnp.max(vals, axis=-1, keepdims=True)
        idx = jnp.min(jnp.where(vals == m, lane, float(LANES)), axis=-1, keepdims=True)
        hot = lane == idx
        vals = jnp.where(hot, NEG_BIG * 2.0, vals)
        tops.append(m)
        idxs.append(idx)
        hots.append(hot)
    exps = [jnp.exp(m - tops[0]) for m in tops]
    denom = exps[0] + exps[1] + exps[2] + exps[3]
    gates = [e / denom for e in exps]

    @pl.when(pl.program_id(0) == 0)
    def _():
        cnt_ref[...] = jnp.zeros_like(cnt_ref)

    multi = jnp.zeros((tm, LANES), F32)
    for hot in hots:
        multi = multi + jnp.where(hot, 1.0, 0.0)
    r = lax.broadcasted_iota(jnp.int32, (tm, tm), 0)
    c = lax.broadcasted_iota(jnp.int32, (tm, tm), 1)
    strict_lower = jnp.where(c < r, 1.0, 0.0).astype(BF16)
    before = jnp.dot(strict_lower, multi.astype(BF16), preferred_element_type=F32) + cnt_ref[...]
    cnt_ref[...] = cnt_ref[...] + jnp.sum(multi, axis=0, keepdims=True)

    meta = jnp.zeros((tm, LANES), F32)
    for kk in range(TOP_K):
        rank = jnp.sum(jnp.where(hots[kk], before, 0.0), axis=-1, keepdims=True)
        meta = jnp.where(lane == kk, idxs[kk], meta)
        meta = jnp.where(lane == TOP_K + kk, gates[kk], meta)
        meta = jnp.where(lane == 2 * TOP_K + kk, rank, meta)
    meta_ref[...] = meta


def _outproj(sb, ret, xt, w_out_bf, ln_g, ln_b, w_router, b_router, alpha, tm=256):
    t, d = xt.shape
    tm = min(tm, t)
    sbw = sb.shape[1]
    wr = jnp.zeros((d, LANES), F32).at[:, :N_EXPERTS].set(w_router.astype(F32))
    br = jnp.full((1, LANES), NEG_BIG, F32).at[0, :N_EXPERTS].set(b_router.astype(F32))
    row = lambda w: pl.BlockSpec((tm, w), lambda i: (i, 0))
    const = lambda r, c: pl.BlockSpec((r, c), lambda i: (0, 0))
    return pl.pallas_call(
        functools.partial(_outproj_kernel, alpha=alpha, sb_width=sbw),
        grid=(t // tm,),
        in_specs=[row(sbw), row(ret.shape[1]), row(d), const(d, d), const(1, d), const(1, d),
                  const(d, LANES), const(1, LANES)],
        out_specs=[row(d), pl.BlockSpec((tm * (d // LANES), LANES), lambda i: (i, 0)), row(LANES),
                   const(1, LANES)],
        out_shape=[jax.ShapeDtypeStruct((t, d), F32),
                   jax.ShapeDtypeStruct((t * (d // LANES), LANES), F32),
                   jax.ShapeDtypeStruct((t, LANES), F32),
                   jax.ShapeDtypeStruct((1, LANES), F32)],
        compiler_params=_cparams(("arbitrary",)),
        name="outproj_ln_router",
    )(sb, ret, xt, w_out_bf, ln_g.reshape(1, d).astype(F32), ln_b.reshape(1, d).astype(F32), wr, br)


DMA_UNROLL = 8


def _wait_slabs(hbm, buf, sem, slot, count, chunks):
    def body(n, _):
        pltpu.make_async_copy(hbm.at[pl.ds(0, chunks), :], buf.at[slot, pl.ds(0, chunks), :],
                              sem.at[slot]).wait()
        return 0
    lax.fori_loop(0, count, body, 0, unroll=DMA_UNROLL)


def _gather_kernel(tok_ref, x_hbm, o_ref, buf, sem, *, rows, chunks):
    i = pl.program_id(0)

    def issue_tile(step, slot):
        def body(r, _):
            src = pl.ds(pl.multiple_of(tok_ref[step * rows + r] * chunks, chunks), chunks)
            dst = pl.ds(pl.multiple_of(r * chunks, chunks), chunks)
            pltpu.make_async_copy(x_hbm.at[src, :], buf.at[slot, dst, :], sem.at[slot]).start()
            return 0
        lax.fori_loop(0, rows, body, 0, unroll=DMA_UNROLL)

    @pl.when(i == 0)
    def _():
        issue_tile(0, 0)

    @pl.when(i + 1 < pl.num_programs(0))
    def _():
        issue_tile(i + 1, (i + 1) % 2)

    slot = i % 2
    _wait_slabs(x_hbm, buf, sem, slot, rows, chunks)
    o_ref[...] = _from_slabs(buf.at[slot], 0, rows, chunks).astype(o_ref.dtype)


def _gather_rows(row_tok, x1_slabs, chunks, rows=256):
    n_rows = row_tok.shape[0]
    return pl.pallas_call(
        functools.partial(_gather_kernel, rows=rows, chunks=chunks),
        grid_spec=pltpu.PrefetchScalarGridSpec(
            num_scalar_prefetch=1,
            grid=(n_rows // rows,),
            in_specs=[pl.BlockSpec(memory_space=pl.ANY)],
            out_specs=pl.BlockSpec((rows, chunks * LANES), lambda i, tok: (i, 0)),
            scratch_shapes=[pltpu.VMEM((2, rows * chunks, LANES), F32), pltpu.SemaphoreType.DMA((2,))],
        ),
        out_shape=jax.ShapeDtypeStruct((n_rows, chunks * LANES), BF16),
        compiler_params=_cparams(("arbitrary",)),
        name="gather_rows",
    )(row_tok, x1_slabs)


def _moe_kernel(ie_ref, ij_ref, ist_ref, ins_ref, tail_ref, x_hbm, wg_ref, wu_ref, bg_ref, bu_ref,
                wd_ref, bd_ref, y_hbm, x_vmem, acc, wg_bf, wu_bf, wd_bf, stage, sem_in, sem_out, *, n_f):
    del ie_ref, ij_ref
    i = pl.program_id(0)
    j = pl.program_id(1)
    nsub = ins_ref[i]
    start = ist_ref[i]
    d = acc.shape[1]
    chunks = d // LANES
    block_rows = ROW_BLOCK * chunks

    def rows_of(r):
        return pl.ds(pl.multiple_of(r * ROW_BLOCK, ROW_BLOCK), ROW_BLOCK)

    def slabs_of(block):
        return pl.ds(pl.multiple_of(block * block_rows, block_rows), block_rows)

    def for_each(lo, hi, fn):
        def body(r, _):
            fn(r)
            return 0
        lax.fori_loop(lo, hi, body, 0)

    @pl.when((i == 0) & (j == 0))
    def _():
        stage[0] = jnp.zeros(stage.shape[1:], F32)

        def zero_copy(bk):
            return pltpu.make_async_copy(stage.at[0], y_hbm.at[slabs_of(bk), :], sem_out.at[0])

        n_blocks = y_hbm.shape[0] // block_rows
        for_each(tail_ref[0], n_blocks, lambda bk: zero_copy(bk).start())
        for_each(tail_ref[0], n_blocks, lambda bk: zero_copy(bk).wait())

    first_block = start // ROW_BLOCK

    def in_copy(r):
        src = pl.ds(pl.multiple_of(start + r * ROW_BLOCK, ROW_BLOCK), ROW_BLOCK)
        return pltpu.make_async_copy(x_hbm.at[src, :], x_vmem.at[rows_of(r), :], sem_in)

    def out_copy(r, slot):
        return pltpu.make_async_copy(stage.at[slot], y_hbm.at[slabs_of(first_block + r), :],
                                     sem_out.at[slot])

    def hidden(r):
        xb = x_vmem[rows_of(r), :]
        gate = jnp.dot(xb, wg_bf[...], preferred_element_type=F32) + bg_ref[...]
        up = jnp.dot(xb, wu_bf[...], preferred_element_type=F32) + bu_ref[...]
        gate = jnp.minimum(gate, SWIGLU_LIMIT)
        up = jnp.clip(up, -SWIGLU_LIMIT, SWIGLU_LIMIT)
        act = (up + 1.0) * (gate * jax.nn.sigmoid(SWIGLU_ALPHA * gate))
        return jnp.dot(act.astype(BF16), wd_bf[...], preferred_element_type=F32)

    @pl.when(nsub > 0)
    def _():
        @pl.when(j == 0)
        def _():
            for_each(0, nsub, lambda r: in_copy(r).start())
            bias_rows = jnp.broadcast_to(bd_ref[...], (ROW_BLOCK, d))

            def init(r):
                acc[rows_of(r), :] = bias_rows
            for_each(0, nsub, init)
            for_each(0, nsub, lambda r: in_copy(r).wait())

        wg_bf[...] = wg_ref[...].astype(BF16)
        wu_bf[...] = wu_ref[...].astype(BF16)
        wd_bf[...] = wd_ref[...].astype(BF16)

        @pl.when(j < n_f - 1)
        def _():
            def sub(r):
                acc[rows_of(r), :] += hidden(r)
            for_each(0, nsub, sub)

        @pl.when(j == n_f - 1)
        def _():
            def sub(r):
                slot = r % 2
                final = acc[rows_of(r), :] + hidden(r)

                @pl.when(r >= 2)
                def _():
                    out_copy(r - 2, slot).wait()
                _to_slabs(stage.at[slot], 0, ROW_BLOCK, final)
                out_copy(r, slot).start()
            for_each(0, nsub, sub)

            @pl.when(nsub >= 2)
            def _():
                out_copy(nsub - 2, nsub % 2).wait()
            out_copy(nsub - 1, (nsub - 1) % 2).wait()


def _moe_ffn(item_e, item_j, item_start, item_nsub, tail_block, x_rows, w_gate_up, b_gate_up, w_down, b_down,
             r_max, tf=512):
    n_e, d, two_f = w_gate_up.shape
    chunks = d // LANES
    d_ff = two_f // 2
    n_f = d_ff // tf
    n_items = item_e.shape[0]

    def jf(i, j, ij):
        return jnp.where(ij[i] < 0, j, ij[i])

    in_specs = [
        pl.BlockSpec(memory_space=pl.ANY),
        pl.BlockSpec((None, d, tf), lambda i, j, ie, ij, ist, ins, tl: (ie[i], 0, jf(i, j, ij))),
        pl.BlockSpec((None, d, tf), lambda i, j, ie, ij, ist, ins, tl: (ie[i], 0, n_f + jf(i, j, ij))),
        pl.BlockSpec((None, 1, tf), lambda i, j, ie, ij, ist, ins, tl: (ie[i], 0, jf(i, j, ij))),
        pl.BlockSpec((None, 1, tf), lambda i, j, ie, ij, ist, ins, tl: (ie[i], 0, n_f + jf(i, j, ij))),
        pl.BlockSpec((None, tf, d), lambda i, j, ie, ij, ist, ins, tl: (ie[i], jf(i, j, ij), 0)),
        pl.BlockSpec((None, 1, d), lambda i, j, ie, ij, ist, ins, tl: (ie[i], 0, 0)),
    ]
    return pl.pallas_call(
        functools.partial(_moe_kernel, n_f=n_f),
        grid_spec=pltpu.PrefetchScalarGridSpec(
            num_scalar_prefetch=5,
            grid=(n_items, n_f),
            in_specs=in_specs,
            out_specs=pl.BlockSpec(memory_space=pl.ANY),
            scratch_shapes=[pltpu.VMEM((r_max, d), BF16), pltpu.VMEM((r_max, d), F32),
                            pltpu.VMEM((d, tf), BF16), pltpu.VMEM((d, tf), BF16),
                            pltpu.VMEM((tf, d), BF16),
                            pltpu.VMEM((2, ROW_BLOCK * chunks, LANES), F32),
                            pltpu.SemaphoreType.DMA(()), pltpu.SemaphoreType.DMA((2,))],
        ),
        out_shape=jax.ShapeDtypeStruct((x_rows.shape[0] * chunks, LANES), F32),
        compiler_params=_cparams(("arbitrary", "arbitrary")),
        name="moe_ffn",
    )(item_e, item_j, item_start, item_nsub, tail_block, x_rows, w_gate_up, w_gate_up,
      b_gate_up.reshape(n_e, 1, two_f), b_gate_up.reshape(n_e, 1, two_f), w_down,
      b_down.reshape(n_e, 1, d))


def _combine_kernel(dest_ref, y_hbm, x1_ref, meta_ref, g_ref, b_ref, o_ref, buf, sem, *, alpha, tm, chunks):
    i = pl.program_id(0)
    per_step = tm * TOP_K

    def issue_tile(step, slot):
        def body(tt, _):
            for kk in range(TOP_K):
                src = pl.ds(pl.multiple_of(dest_ref[step * per_step + tt * TOP_K + kk] * chunks, chunks),
                            chunks)
                dst = pl.ds(pl.multiple_of((kk * tm + tt) * chunks, chunks), chunks)
                pltpu.make_async_copy(y_hbm.at[src, :], buf.at[slot, dst, :], sem.at[slot]).start()
            return 0
        lax.fori_loop(0, tm, body, 0, unroll=DMA_UNROLL // TOP_K)

    @pl.when(i == 0)
    def _():
        issue_tile(0, 0)

    @pl.when(i + 1 < pl.num_programs(0))
    def _():
        issue_tile(i + 1, (i + 1) % 2)

    slot = i % 2
    _wait_slabs(y_hbm, buf, sem, slot, per_step, chunks)
    meta = meta_ref[...]
    y = jnp.zeros(x1_ref.shape, F32)
    for kk in range(TOP_K):
        rows = _from_slabs(buf.at[slot], kk * tm * chunks, tm, chunks)
        y = y + meta[:, TOP_K + kk:TOP_K + kk + 1] * rows
    o_ref[...] = _layer_norm(alpha * x1_ref[...] + y, g_ref[...], b_ref[...])


def _combine(dest_flat, y_rows, x1, meta, ln_g, ln_b, alpha, tm=128):
    t, d = x1.shape
    tm = min(tm, t)
    chunks = d // LANES
    row = lambda w: pl.BlockSpec((tm, w), lambda i, dest: (i, 0))
    const = pl.BlockSpec((1, d), lambda i, dest: (0, 0))
    return pl.pallas_call(
        functools.partial(_combine_kernel, alpha=alpha, tm=tm, chunks=chunks),
        grid_spec=pltpu.PrefetchScalarGridSpec(
            num_scalar_prefetch=1,
            grid=(t // tm,),
            in_specs=[pl.BlockSpec(memory_space=pl.ANY), row(d), row(LANES), const, const],
            out_specs=row(d),
            scratch_shapes=[pltpu.VMEM((2, TOP_K * tm * chunks, LANES), F32),
                            pltpu.SemaphoreType.DMA((2,))],
        ),
        out_shape=jax.ShapeDtypeStruct((t, d), F32),
        compiler_params=_cparams(("arbitrary",)),
        name="combine_ln",
    )(dest_flat, y_rows, x1, meta, ln_g.reshape(1, d).astype(F32), ln_b.reshape(1, d).astype(F32))


def _routing_plan(meta, counts_f, t, r_max):
    idx = meta[:, 0:TOP_K].astype(jnp.int32)
    rank = meta[:, 2 * TOP_K:3 * TOP_K].astype(jnp.int32)
    counts = counts_f[0, :N_EXPERTS].astype(jnp.int32)
    n128 = (counts + ROW_BLOCK - 1) // ROW_BLOCK
    padded = n128 * ROW_BLOCK
    pad_start = jnp.cumsum(padded) - padded
    dest = pad_start[idx] + rank
    tk = t * TOP_K
    n_rows = (tk + ROW_BLOCK - 1) // ROW_BLOCK * ROW_BLOCK + N_EXPERTS * ROW_BLOCK
    flat_tok = jnp.arange(tk, dtype=jnp.int32) // TOP_K
    row_tok = jnp.zeros((n_rows,), jnp.int32).at[dest.reshape(tk)].set(flat_tok)

    subs = r_max // ROW_BLOCK
    n_items = N_EXPERTS + (n_rows // ROW_BLOCK - N_EXPERTS) // subs
    items_e = (n128 + subs - 1) // subs
    items_end = jnp.cumsum(items_e)
    total = items_end[-1]
    slot = jnp.arange(n_items, dtype=jnp.int32)
    live = slot < total
    s_eff = jnp.minimum(slot, total - 1)
    e = jnp.minimum(jnp.searchsorted(items_end, s_eff, side='right'), N_EXPERTS - 1).astype(jnp.int32)
    local = s_eff - (items_end[e] - items_e[e])
    item_start = (pad_start[e] + local * r_max).astype(jnp.int32)
    item_nsub = jnp.where(live, jnp.clip(n128[e] - local * subs, 0, subs), 0).astype(jnp.int32)
    tail_block = jnp.sum(n128).reshape(1).astype(jnp.int32)
    return dest, row_tok, e, live, item_start, item_nsub, tail_block


def kernel(x, w_in, ret_gn_gain, w_out, ln1_gain, ln1_bias, w_router, b_router, w_gate_up, b_gate_up,
           w_down, b_down, ln2_gain, ln2_bias):
    b, s, d = x.shape
    t = b * s
    depth = w_in.shape[0]
    alpha = (2 * depth) ** 0.25
    r_max = 1280
    tf = 512
    n_f = (w_gate_up.shape[-1] // 2) // tf
    xt = x.reshape(t, d)
    for layer in range(depth):
        proj = _in_proj(xt.astype(BF16), w_in[layer].astype(BF16))
        sb = _stickbreak(proj, b, s)
        ret = _retention(proj, ret_gn_gain[layer], b, s)
        x1, x1_slabs, meta, counts = _outproj(sb, ret, xt, w_out[layer].astype(BF16), ln1_gain[layer],
                                              ln1_bias[layer], w_router[layer], b_router[layer], alpha)
        dest, row_tok, item_e, live, item_start, item_nsub, tail_block = _routing_plan(
            meta, counts, t, r_max)
        dest_flat = dest.reshape(t * TOP_K)
        item_j = jnp.where(live, -1, n_f - 1).astype(jnp.int32)
        x_rows = _gather_rows(row_tok, x1_slabs, d // LANES)
        y_rows = _moe_ffn(item_e, item_j, item_start, item_nsub, tail_block, x_rows, w_gate_up[layer],
                          b_gate_up[layer], w_down[layer], b_down[layer], r_max, tf)
        xt = _combine(dest_flat, y_rows, x1, meta, ln2_gain[layer], ln2_bias[layer], alpha)
    return xt.reshape(b, s, d)
```

```python
import functools
import math

import jax
import jax.numpy as jnp
from jax import lax
from jax.experimental import pallas as pl
from jax.experimental.pallas import tpu as pltpu

F32 = jnp.float32
BF16 = jnp.bfloat16

HEAD_DIM = 128
SB_HEADS = 8
RET_HEADS = 8
CHUNK = 64
ROPE_BASE = 10000.0
N_EXPERTS = 32
TOP_K = 4
SWIGLU_LIMIT = 7.0
SWIGLU_ALPHA = 1.702
LN_EPS = 1e-5
GN_EPS = 1e-5

V7X_VMEM_LIMIT_BYTES = 56 * 1024 * 1024
LANES = 128
ROW_BLOCK = 128
NEG_BIG = -1e30
EXP_UNDERFLOW = -105.0


def _cparams(sem, vmem=V7X_VMEM_LIMIT_BYTES):
    return pltpu.CompilerParams(dimension_semantics=sem, vmem_limit_bytes=vmem)


def _matmul_kernel(x_ref, w_ref, o_ref):
    o_ref[...] = jnp.dot(x_ref[...], w_ref[...], preferred_element_type=F32).astype(o_ref.dtype)


def _in_proj(x_bf, w_bf, tm=1024, tn=1024):
    t, d = x_bf.shape
    n = w_bf.shape[1]
    tm = min(tm, t)
    return pl.pallas_call(
        _matmul_kernel,
        grid=(n // tn, t // tm),
        in_specs=[pl.BlockSpec((tm, d), lambda j, i: (i, 0)),
                  pl.BlockSpec((d, tn), lambda j, i: (0, j))],
        out_specs=pl.BlockSpec((tm, tn), lambda j, i: (i, j)),
        out_shape=jax.ShapeDtypeStruct((t, n), BF16),
        compiler_params=_cparams(("arbitrary", "arbitrary")),
        name="in_proj",
    )(x_bf, w_bf)


def _sb_kernel(q_ref, k_ref, v_ref, o_ref, *, seq, scale, group, sub):
    blk = 128
    nq = seq // blk
    row = lax.broadcasted_iota(jnp.int32, (blk, blk), 0)
    col = lax.broadcasted_iota(jnp.int32, (blk, blk), 1)
    causal = col < row
    r2 = lax.broadcasted_iota(jnp.int32, (2 * blk, 2 * blk), 0)
    c2 = lax.broadcasted_iota(jnp.int32, (2 * blk, 2 * blk), 1)
    cum_rhs = jnp.where((c2 >= blk) | ((r2 & (blk - 1)) >= c2), 1.0, 0.0).astype(BF16)

    n_sub = group // sub
    causal_sub = jnp.concatenate([causal] * sub, axis=0)
    heads_of = lambda s: range(s * sub, (s + 1) * sub)
    cols = lambda g: slice(g * blk, (g + 1) * blk)

    def block(qbs, kb, carries, accs, masked):
        ks = pl.ds(pl.multiple_of(kb * blk, blk), blk)
        zs = [jnp.concatenate(
            [lax.dot_general(qbs[g], k_ref[ks, cols(g)], (((1,), (1,)), ((), ())),
                             preferred_element_type=F32) for g in heads_of(s)], axis=0) * scale
              for s in range(n_sub)]
        sums = []
        for z in zs:
            neg_z = -z
            lnb = jnp.minimum(neg_z, 0.0) - jnp.log(1.0 + jnp.exp(jnp.minimum(z, neg_z)))
            if masked:
                lnb = jnp.where(causal_sub, lnb, 0.0)
            hi = lnb.astype(BF16)
            lo = (lnb - hi.astype(F32)).astype(BF16)
            sums.append(jnp.dot(jnp.concatenate([hi, lo], axis=1), cum_rhs, preferred_element_type=F32))
        new_carries, new_accs = [], []
        for s in range(n_sub):
            incl = sums[s][:, :blk]
            total = sums[s][:, blk:]
            w = jnp.exp(zs[s] + incl + carries[s])
            if masked:
                w = jnp.where(causal_sub, w, 0.0)
            w = w.astype(BF16)
            for n, g in enumerate(heads_of(s)):
                new_accs.append(accs[g] + jnp.dot(w[n * blk:(n + 1) * blk], v_ref[ks, cols(g)],
                                                  preferred_element_type=F32))
            new_carries.append(carries[s] + total)
        return tuple(new_carries), tuple(new_accs)

    def q_body(qi, _):
        qs = pl.ds(pl.multiple_of(qi * blk, blk), blk)
        qbs = [q_ref[qs, cols(g)] for g in range(group)]
        state = block(qbs, qi, (jnp.zeros((sub * blk, blk), F32),) * n_sub,
                      (jnp.zeros((blk, blk), F32),) * group, True)

        def some_weight_left(carries):
            top = functools.reduce(jnp.maximum, carries)
            return (jnp.max(top) > EXP_UNDERFLOW).astype(jnp.int32)

        def live(st):
            t, more, _, _ = st
            return jnp.logical_and(t < qi, more > 0)

        def kb_body(st):
            t, _, carries, accs = st
            carries, accs = block(qbs, qi - 1 - t, carries, accs, False)
            return t + 1, some_weight_left(carries), carries, accs

        _, _, _, accs = lax.while_loop(live, kb_body, (0, 1, state[0], state[1]))
        for g in range(group):
            o_ref[qs, cols(g)] = accs[g].astype(o_ref.dtype)
        return 0

    lax.fori_loop(0, nq, q_body, 0)


def _stickbreak(proj, batch, seq, group=8, sub=4):
    t = batch * seq
    h = SB_HEADS
    n_groups = h // group
    width = group * HEAD_DIM
    kern = functools.partial(_sb_kernel, seq=seq, scale=1.0 / math.sqrt(HEAD_DIM), group=group,
                             sub=sub)
    spec = lambda off: pl.BlockSpec((seq, width), lambda b, hg: (b, off * n_groups + hg))
    return pl.pallas_call(
        kern,
        grid=(batch, n_groups),
        in_specs=[spec(0), spec(1), spec(2)],
        out_specs=pl.BlockSpec((seq, width), lambda b, hg: (b, hg)),
        out_shape=jax.ShapeDtypeStruct((t, h * HEAD_DIM), BF16),
        compiler_params=_cparams(("arbitrary", "arbitrary")),
        name="stickbreak",
    )(proj, proj, proj)


def _ret_kernel(q_ref, k_ref, v_ref, g_ref, cos_ref, sin_ref, intra_ref, qdec_ref, kdec_ref,
                cdec_ref, gain_ref, o_ref, *, seq, heads):
    n_chunks = seq // CHUNK
    half = HEAD_DIM // 2
    k_scale = HEAD_DIM ** -0.5
    cols = lambda h: slice(h * HEAD_DIM, (h + 1) * HEAD_DIM)
    contract_last = (((1,), (1,)), ((), ()))
    contract_rows = (((0,), (0,)), ((), ()))

    def chunk(n, states):
        rs = pl.ds(pl.multiple_of(n * CHUNK, CHUNK), CHUNK)
        cos = cos_ref[rs, :]
        sin = sin_ref[rs, :]
        qrs, krs, crosses, kvs = [], [], [], []
        for h in range(heads):
            q = q_ref[rs, cols(h)].astype(F32)
            k = k_ref[rs, cols(h)].astype(F32)
            qr = q * cos + pltpu.roll(q, half, 1) * sin
            kr = (k * cos + pltpu.roll(k, half, 1) * sin) * k_scale
            qrs.append(qr.astype(BF16))
            krs.append(kr.astype(BF16))
            crosses.append(jnp.dot((qr * qdec_ref[h]).astype(BF16), states[h].astype(BF16),
                                   preferred_element_type=F32))
            kvs.append(lax.dot_general((kr * kdec_ref[h]).astype(BF16), v_ref[rs, cols(h)], contract_rows,
                                       preferred_element_type=F32))
        scores = [lax.dot_general(qrs[h], krs[h], contract_last, preferred_element_type=F32) * intra_ref[h]
                  for h in range(heads)]
        outs = [crosses[h] + jnp.dot(scores[h].astype(BF16), v_ref[rs, cols(h)], preferred_element_type=F32)
                for h in range(heads)]
        new_states = []
        for h in range(heads):
            o = outs[h]
            mu = jnp.mean(o, axis=-1, keepdims=True)
            var = jnp.mean(jnp.square(o - mu), axis=-1, keepdims=True)
            on = (o - mu) * lax.rsqrt(var + GN_EPS)
            g = g_ref[rs, cols(h)].astype(F32)
            out = on * gain_ref[:, cols(h)] * (g * jax.nn.sigmoid(g))
            o_ref[rs, cols(h)] = out.astype(o_ref.dtype)
            new_states.append(states[h] * cdec_ref[h] + kvs[h])
        return tuple(new_states)

    lax.fori_loop(0, n_chunks, chunk, (jnp.zeros((HEAD_DIM, HEAD_DIM), F32),) * heads)


def _retention_tables(seq):
    d = HEAD_DIM
    inv_freq = ROPE_BASE ** (-jnp.arange(0, d, 2, dtype=F32) / d)
    ang = jnp.arange(seq, dtype=F32)[:, None] * inv_freq[None, :]
    cos, sin = jnp.cos(ang), jnp.sin(ang)
    cos_full = jnp.concatenate([cos, cos], axis=-1)
    sin_signed = jnp.concatenate([-sin, sin], axis=-1)
    log_gamma = jnp.log1p(-jnp.exp2(-5.0 - jnp.arange(RET_HEADS, dtype=F32)))
    i = jnp.arange(CHUNK, dtype=F32)
    intra = jnp.exp(log_gamma[:, None, None] * jnp.abs(i[:, None] - i[None, :]))
    k_decay = jnp.exp(log_gamma[:, None] * (CHUNK - 1 - i))
    q_decay = jnp.exp(log_gamma[:, None] * (i + 1.0))
    c_decay = jnp.exp(log_gamma * CHUNK)
    bc = lambda a: jnp.broadcast_to(a[..., None], a.shape + (d,))
    return cos_full, sin_signed, intra, bc(q_decay), bc(k_decay), bc(c_decay[:, None])


def _retention(proj, gn_gain, batch, seq):
    t = batch * seq
    h = RET_HEADS
    base = 3 * SB_HEADS
    cos_full, sin_signed, intra, qdec, kdec, cdec = _retention_tables(seq)
    width = h * HEAD_DIM
    group0 = base // h
    spec = lambda off: pl.BlockSpec((seq, width), lambda b: (b, group0 + off))
    full = pl.BlockSpec((seq, HEAD_DIM), lambda b: (0, 0))
    table = lambda r, c: pl.BlockSpec((h, r, c), lambda b: (0, 0, 0))
    return pl.pallas_call(
        functools.partial(_ret_kernel, seq=seq, heads=h),
        grid=(batch,),
        in_specs=[spec(0), spec(1), spec(2), spec(3), full, full,
                  table(CHUNK, CHUNK), table(CHUNK, HEAD_DIM), table(CHUNK, HEAD_DIM),
                  table(1, HEAD_DIM),
                  pl.BlockSpec((1, width), lambda b: (0, 0))],
        out_specs=pl.BlockSpec((seq, width), lambda b: (b, 0)),
        out_shape=jax.ShapeDtypeStruct((t, width), BF16),
        compiler_params=_cparams(("arbitrary",)),
        name="retention",
    )(proj, proj, proj, proj, cos_full, sin_signed, intra, qdec, kdec, cdec,
      gn_gain.reshape(1, h * HEAD_DIM).astype(F32))


def _layer_norm(hid, gain, bias):
    mu = jnp.mean(hid, axis=-1, keepdims=True)
    cen = hid - mu
    var = jnp.mean(jnp.square(cen), axis=-1, keepdims=True)
    return cen * lax.rsqrt(var + LN_EPS) * gain + bias


def _to_slabs(slab_ref, base, rows, value):
    chunks = value.shape[1] // LANES
    for c in range(chunks):
        slab_ref[pl.ds(base + c, rows, stride=chunks), :] = value[:, c * LANES:(c + 1) * LANES]


def _from_slabs(slab_ref, base, rows, chunks):
    return jnp.concatenate(
        [slab_ref[pl.ds(base + c, rows, stride=chunks), :] for c in range(chunks)], axis=1)


def _outproj_kernel(sb_ref, ret_ref, x_ref, w_ref, g_ref, b_ref, wr_ref, br_ref,
                    x1_ref, x1s_ref, meta_ref, cnt_ref, *, alpha, sb_width):
    tm = x_ref.shape[0]
    mix = jnp.dot(sb_ref[...], w_ref[:sb_width, :], preferred_element_type=F32)
    mix = mix + jnp.dot(ret_ref[...], w_ref[sb_width:, :], preferred_element_type=F32)
    x1 = _layer_norm(alpha * x_ref[...] + mix, g_ref[...], b_ref[...])
    x1_ref[...] = x1
    _to_slabs(x1s_ref, 0, tm, x1)

    logits = jnp.dot(x1, wr_ref[...], preferred_element_type=F32,
                     precision=lax.Precision.HIGHEST) + br_ref[...]
    lane = lax.broadcasted_iota(jnp.int32, (tm, LANES), 1).astype(F32)
    vals = logits
    tops, idxs, hots = [], [], []
    for _ in range(TOP_K):
        m = jnp.max(vals, axis=-1, keepdims=True)
        idx = jnp.min(jnp.where(vals == m, lane, float(LANES)), axis=-1, keepdims=True)
        hot = lane == idx
        vals = jnp.where(hot, NEG_BIG * 2.0, vals)
        tops.append(m)
        idxs.append(idx)
        hots.append(hot)
    exps = [jnp.exp(m - tops[0]) for m in tops]
    denom = exps[0] + exps[1] + exps[2] + exps[3]
    gates = [e / denom for e in exps]

    @pl.when(pl.program_id(0) == 0)
    def _():
        cnt_ref[...] = jnp.zeros_like(cnt_ref)

    multi = jnp.zeros((tm, LANES), F32)
    for hot in hots:
        multi = multi + jnp.where(hot, 1.0, 0.0)
    r = lax.broadcasted_iota(jnp.int32, (tm, tm), 0)
    c = lax.broadcasted_iota(jnp.int32, (tm, tm), 1)
    strict_lower = jnp.where(c < r, 1.0, 0.0).astype(BF16)
    before = jnp.dot(strict_lower, multi.astype(BF16), preferred_element_type=F32) + cnt_ref[...]
    cnt_ref[...] = cnt_ref[...] + jnp.sum(multi, axis=0, keepdims=True)

    meta = jnp.zeros((tm, LANES), F32)
    for kk in range(TOP_K):
        rank = jnp.sum(jnp.where(hots[kk], before, 0.0), axis=-1, keepdims=True)
        meta = jnp.where(lane == kk, idxs[kk], meta)
        meta = jnp.where(lane == TOP_K + kk, gates[kk], meta)
        meta = jnp.where(lane == 2 * TOP_K + kk, rank, meta)
    meta_ref[...] = meta


def _outproj(sb, ret, xt, w_out_bf, ln_g, ln_b, w_router, b_router, alpha, tm=256):
    t, d = xt.shape
    tm = min(tm, t)
    sbw = sb.shape[1]
    wr = jnp.zeros((d, LANES), F32).at[:, :N_EXPERTS].set(w_router.astype(F32))
    br = jnp.full((1, LANES), NEG_BIG, F32).at[0, :N_EXPERTS].set(b_router.astype(F32))
    row = lambda w: pl.BlockSpec((tm, w), lambda i: (i, 0))
    const = lambda r, c: pl.BlockSpec((r, c), lambda i: (0, 0))
    return pl.pallas_call(
        functools.partial(_outproj_kernel, alpha=alpha, sb_width=sbw),
        grid=(t // tm,),
        in_specs=[row(sbw), row(ret.shape[1]), row(d), const(d, d), const(1, d), const(1, d),
                  const(d, LANES), const(1, LANES)],
        out_specs=[row(d), pl.BlockSpec((tm * (d // LANES), LANES), lambda i: (i, 0)), row(LANES),
                   const(1, LANES)],
        out_shape=[jax.ShapeDtypeStruct((t, d), F32),
                   jax.ShapeDtypeStruct((t * (d // LANES), LANES), F32),
                   jax.ShapeDtypeStruct((t, LANES), F32),
                   jax.ShapeDtypeStruct((1, LANES), F32)],
        compiler_params=_cparams(("arbitrary",)),
        name="outproj_ln_router",
    )(sb, ret, xt, w_out_bf, ln_g.reshape(1, d).astype(F32), ln_b.reshape(1, d).astype(F32), wr, br)


DMA_UNROLL = 8


def _wait_slabs(hbm, buf, sem, slot, count, chunks):
    def body(n, _):
        pltpu.make_async_copy(hbm.at[pl.ds(0, chunks), :], buf.at[slot, pl.ds(0, chunks), :],
                              sem.at[slot]).wait()
        return 0
    lax.fori_loop(0, count, body, 0, unroll=DMA_UNROLL)


def _gather_kernel(tok_ref, x_hbm, o_ref, buf, sem, *, rows, chunks):
    i = pl.program_id(0)

    def issue_tile(step, slot):
        def body(r, _):
            src = pl.ds(pl.multiple_of(tok_ref[step * rows + r] * chunks, chunks), chunks)
            dst = pl.ds(pl.multiple_of(r * chunks, chunks), chunks)
            pltpu.make_async_copy(x_hbm.at[src, :], buf.at[slot, dst, :], sem.at[slot]).start()
            return 0
        lax.fori_loop(0, rows, body, 0, unroll=DMA_UNROLL)

    @pl.when(i == 0)
    def _():
        issue_tile(0, 0)

    @pl.when(i + 1 < pl.num_programs(0))
    def _():
        issue_tile(i + 1, (i + 1) % 2)

    slot = i % 2
    _wait_slabs(x_hbm, buf, sem, slot, rows, chunks)
    o_ref[...] = _from_slabs(buf.at[slot], 0, rows, chunks).astype(o_ref.dtype)


def _gather_rows(row_tok, x1_slabs, chunks, rows=256):
    n_rows = row_tok.shape[0]
    return pl.pallas_call(
        functools.partial(_gather_kernel, rows=rows, chunks=chunks),
        grid_spec=pltpu.PrefetchScalarGridSpec(
            num_scalar_prefetch=1,
            grid=(n_rows // rows,),
            in_specs=[pl.BlockSpec(memory_space=pl.ANY)],
            out_specs=pl.BlockSpec((rows, chunks * LANES), lambda i, tok: (i, 0)),
            scratch_shapes=[pltpu.VMEM((2, rows * chunks, LANES), F32), pltpu.SemaphoreType.DMA((2,))],
        ),
        out_shape=jax.ShapeDtypeStruct((n_rows, chunks * LANES), BF16),
        compiler_params=_cparams(("arbitrary",)),
        name="gather_rows",
    )(row_tok, x1_slabs)


def _moe_kernel(ie_ref, ij_ref, ist_ref, ins_ref, tail_ref, x_hbm, wg_ref, wu_ref, bg_ref, bu_ref,
                wd_ref, bd_ref, y_hbm, x_vmem, acc, wg_bf, wu_bf, wd_bf, stage, sem_in, sem_out, *, n_f):
    del ie_ref, ij_ref
    i = pl.program_id(0)
    j = pl.program_id(1)
    nsub = ins_ref[i]
    start = ist_ref[i]
    d = acc.shape[1]
    chunks = d // LANES
    block_rows = ROW_BLOCK * chunks

    def rows_of(r):
        return pl.ds(pl.multiple_of(r * ROW_BLOCK, ROW_BLOCK), ROW_BLOCK)

    def slabs_of(block):
        return pl.ds(pl.multiple_of(block * block_rows, block_rows), block_rows)

    def for_each(lo, hi, fn):
        def body(r, _):
            fn(r)
            return 0
        lax.fori_loop(lo, hi, body, 0)

    @pl.when((i == 0) & (j == 0))
    def _():
        stage[0] = jnp.zeros(stage.shape[1:], F32)

        def zero_copy(bk):
            return pltpu.make_async_copy(stage.at[0], y_hbm.at[slabs_of(bk), :], sem_out.at[0])

        n_blocks = y_hbm.shape[0] // block_rows
        for_each(tail_ref[0], n_blocks, lambda bk: zero_copy(bk).start())
        for_each(tail_ref[0], n_blocks, lambda bk: zero_copy(bk).wait())

    first_block = start // ROW_BLOCK

    def in_copy(r):
        src = pl.ds(pl.multiple_of(start + r * ROW_BLOCK, ROW_BLOCK), ROW_BLOCK)
        return pltpu.make_async_copy(x_hbm.at[src, :], x_vmem.at[rows_of(r), :], sem_in)

    def out_copy(r, slot):
        return pltpu.make_async_copy(stage.at[slot], y_hbm.at[slabs_of(first_block + r), :],
                                     sem_out.at[slot])

    def span(r, n_blocks):
        return pl.ds(pl.multiple_of(r * ROW_BLOCK, ROW_BLOCK), n_blocks * ROW_BLOCK)

    def hidden(rows):
        xb = x_vmem[rows, :]
        gate = jnp.dot(xb, wg_bf[...], preferred_element_type=F32) + bg_ref[...]
        up = jnp.dot(xb, wu_bf[...], preferred_element_type=F32) + bu_ref[...]
        gate = jnp.minimum(gate, SWIGLU_LIMIT)
        up = jnp.clip(up, -SWIGLU_LIMIT, SWIGLU_LIMIT)
        act = (up + 1.0) * (gate * jax.nn.sigmoid(SWIGLU_ALPHA * gate))
        return jnp.dot(act.astype(BF16), wd_bf[...], preferred_element_type=F32)

    @pl.when(nsub > 0)
    def _():
        @pl.when(j == 0)
        def _():
            for_each(0, nsub, lambda r: in_copy(r).start())
            bias_rows = jnp.broadcast_to(bd_ref[...], (ROW_BLOCK, d))

            def init(r):
                acc[rows_of(r), :] = bias_rows
            for_each(0, nsub, init)
            for_each(0, nsub, lambda r: in_copy(r).wait())

        wg_bf[...] = wg_ref[...].astype(BF16)
        wu_bf[...] = wu_ref[...].astype(BF16)
        wd_bf[...] = wd_ref[...].astype(BF16)

        n_pairs = nsub // 2
        odd = nsub % 2 == 1

        @pl.when(j < n_f - 1)
        def _():
            def pair(p):
                acc[span(2 * p, 2), :] += hidden(span(2 * p, 2))
            for_each(0, n_pairs, pair)

            @pl.when(odd)
            def _():
                acc[span(nsub - 1, 1), :] += hidden(span(nsub - 1, 1))

        @pl.when(j == n_f - 1)
        def _():
            def write_out(r, slot, final, reuse):
                @pl.when(reuse)
                def _():
                    out_copy(r, slot).wait()
                _to_slabs(stage.at[slot], 0, ROW_BLOCK, final)
                out_copy(r, slot).start()

            def pair(p):
                final = acc[span(2 * p, 2), :] + hidden(span(2 * p, 2))
                write_out(2 * p, 0, final[:ROW_BLOCK], p > 0)
                write_out(2 * p + 1, 1, final[ROW_BLOCK:], p > 0)
            for_each(0, n_pairs, pair)

            @pl.when(odd)
            def _():
                final = acc[span(nsub - 1, 1), :] + hidden(span(nsub - 1, 1))
                write_out(nsub - 1, 0, final, n_pairs > 0)

            out_copy(0, 0).wait()

            @pl.when(n_pairs > 0)
            def _():
                out_copy(0, 1).wait()


def _moe_ffn(item_e, item_j, item_start, item_nsub, tail_block, x_rows, w_gate_up, b_gate_up, w_down, b_down,
             r_max, tf=512):
    n_e, d, two_f = w_gate_up.shape
    chunks = d // LANES
    d_ff = two_f // 2
    n_f = d_ff // tf
    n_items = item_e.shape[0]

    def jf(i, j, ij):
        return jnp.where(ij[i] < 0, j, ij[i])

    in_specs = [
        pl.BlockSpec(memory_space=pl.ANY),
        pl.BlockSpec((None, d, tf), lambda i, j, ie, ij, ist, ins, tl: (ie[i], 0, jf(i, j, ij))),
        pl.BlockSpec((None, d, tf), lambda i, j, ie, ij, ist, ins, tl: (ie[i], 0, n_f + jf(i, j, ij))),
        pl.BlockSpec((None, 1, tf), lambda i, j, ie, ij, ist, ins, tl: (ie[i], 0, jf(i, j, ij))),
        pl.BlockSpec((None, 1, tf), lambda i, j, ie, ij, ist, ins, tl: (ie[i], 0, n_f + jf(i, j, ij))),
        pl.BlockSpec((None, tf, d), lambda i, j, ie, ij, ist, ins, tl: (ie[i], jf(i, j, ij), 0)),
        pl.BlockSpec((None, 1, d), lambda i, j, ie, ij, ist, ins, tl: (ie[i], 0, 0)),
    ]
    return pl.pallas_call(
        functools.partial(_moe_kernel, n_f=n_f),
        grid_spec=pltpu.PrefetchScalarGridSpec(
            num_scalar_prefetch=5,
            grid=(n_items, n_f),
            in_specs=in_specs,
            out_specs=pl.BlockSpec(memory_space=pl.ANY),
            scratch_shapes=[pltpu.VMEM((r_max, d), BF16), pltpu.VMEM((r_max, d), F32),
                            pltpu.VMEM((d, tf), BF16), pltpu.VMEM((d, tf), BF16),
                            pltpu.VMEM((tf, d), BF16),
                            pltpu.VMEM((2, ROW_BLOCK * chunks, LANES), F32),
                            pltpu.SemaphoreType.DMA(()), pltpu.SemaphoreType.DMA((2,))],
        ),
        out_shape=jax.ShapeDtypeStruct((x_rows.shape[0] * chunks, LANES), F32),
        compiler_params=_cparams(("arbitrary", "arbitrary")),
        name="moe_ffn",
    )(item_e, item_j, item_start, item_nsub, tail_block, x_rows, w_gate_up, w_gate_up,
      b_gate_up.reshape(n_e, 1, two_f), b_gate_up.reshape(n_e, 1, two_f), w_down,
      b_down.reshape(n_e, 1, d))


def _combine_kernel(dest_ref, y_hbm, x1_ref, meta_ref, g_ref, b_ref, o_ref, buf, sem, *, alpha, tm, chunks):
    i = pl.program_id(0)
    per_step = tm * TOP_K

    def issue_tile(step, slot):
        def body(tt, _):
            for kk in range(TOP_K):
                src = pl.ds(pl.multiple_of(dest_ref[step * per_step + tt * TOP_K + kk] * chunks, chunks),
                            chunks)
                dst = pl.ds(pl.multiple_of((kk * tm + tt) * chunks, chunks), chunks)
                pltpu.make_async_copy(y_hbm.at[src, :], buf.at[slot, dst, :], sem.at[slot]).start()
            return 0
        lax.fori_loop(0, tm, body, 0, unroll=DMA_UNROLL // TOP_K)

    @pl.when(i == 0)
    def _():
        issue_tile(0, 0)

    @pl.when(i + 1 < pl.num_programs(0))
    def _():
        issue_tile(i + 1, (i + 1) % 2)

    slot = i % 2
    _wait_slabs(y_hbm, buf, sem, slot, per_step, chunks)
    meta = meta_ref[...]
    y = jnp.zeros(x1_ref.shape, F32)
    for kk in range(TOP_K):
        rows = _from_slabs(buf.at[slot], kk * tm * chunks, tm, chunks)
        y = y + meta[:, TOP_K + kk:TOP_K + kk + 1] * rows
    o_ref[...] = _layer_norm(alpha * x1_ref[...] + y, g_ref[...], b_ref[...])


def _combine(dest_flat, y_rows, x1, meta, ln_g, ln_b, alpha, tm=128):
    t, d = x1.shape
    tm = min(tm, t)
    chunks = d // LANES
    row = lambda w: pl.BlockSpec((tm, w), lambda i, dest: (i, 0))
    const = pl.BlockSpec((1, d), lambda i, dest: (0, 0))
    return pl.pallas_call(
        functools.partial(_combine_kernel, alpha=alpha, tm=tm, chunks=chunks),
        grid_spec=pltpu.PrefetchScalarGridSpec(
            num_scalar_prefetch=1,
            grid=(t // tm,),
            in_specs=[pl.BlockSpec(memory_space=pl.ANY), row(d), row(LANES), const, const],
            out_specs=row(d),
            scratch_shapes=[pltpu.VMEM((2, TOP_K * tm * chunks, LANES), F32),
                            pltpu.SemaphoreType.DMA((2,))],
        ),
        out_shape=jax.ShapeDtypeStruct((t, d), F32),
        compiler_params=_cparams(("arbitrary",)),
        name="combine_ln",
    )(dest_flat, y_rows, x1, meta, ln_g.reshape(1, d).astype(F32), ln_b.reshape(1, d).astype(F32))


def _routing_plan(meta, counts_f, t, r_max):
    idx = meta[:, 0:TOP_K].astype(jnp.int32)
    rank = meta[:, 2 * TOP_K:3 * TOP_K].astype(jnp.int32)
    counts = counts_f[0, :N_EXPERTS].astype(jnp.int32)
    n128 = (counts + ROW_BLOCK - 1) // ROW_BLOCK
    padded = n128 * ROW_BLOCK
    pad_start = jnp.cumsum(padded) - padded
    dest = pad_start[idx] + rank
    tk = t * TOP_K
    n_rows = (tk + ROW_BLOCK - 1) // ROW_BLOCK * ROW_BLOCK + N_EXPERTS * ROW_BLOCK
    flat_tok = jnp.arange(tk, dtype=jnp.int32) // TOP_K
    row_tok = jnp.zeros((n_rows,), jnp.int32).at[dest.reshape(tk)].set(flat_tok)

    subs = r_max // ROW_BLOCK
    n_items = N_EXPERTS + (n_rows // ROW_BLOCK - N_EXPERTS) // subs
    items_e = (n128 + subs - 1) // subs
    items_end = jnp.cumsum(items_e)
    total = items_end[-1]
    slot = jnp.arange(n_items, dtype=jnp.int32)
    live = slot < total
    s_eff = jnp.minimum(slot, total - 1)
    e = jnp.minimum(jnp.searchsorted(items_end, s_eff, side='right'), N_EXPERTS - 1).astype(jnp.int32)
    local = s_eff - (items_end[e] - items_e[e])
    item_start = (pad_start[e] + local * r_max).astype(jnp.int32)
    item_nsub = jnp.where(live, jnp.clip(n128[e] - local * subs, 0, subs), 0).astype(jnp.int32)
    tail_block = jnp.sum(n128).reshape(1).astype(jnp.int32)
    return dest, row_tok, e, live, item_start, item_nsub, tail_block


def kernel(x, w_in, ret_gn_gain, w_out, ln1_gain, ln1_bias, w_router, b_router, w_gate_up, b_gate_up,
           w_down, b_down, ln2_gain, ln2_bias):
    b, s, d = x.shape
    t = b * s
    depth = w_in.shape[0]
    alpha = (2 * depth) ** 0.25
    r_max = 1280
    tf = 512
    n_f = (w_gate_up.shape[-1] // 2) // tf
    xt = x.reshape(t, d)
    for layer in range(depth):
        proj = _in_proj(xt.astype(BF16), w_in[layer].astype(BF16))
        sb = _stickbreak(proj, b, s)
        ret = _retention(proj, ret_gn_gain[layer], b, s)
        x1, x1_slabs, meta, counts = _outproj(sb, ret, xt, w_out[layer].astype(BF16), ln1_gain[layer],
                                              ln1_bias[layer], w_router[layer], b_router[layer], alpha)
        dest, row_tok, item_e, live, item_start, item_nsub, tail_block = _routing_plan(
            meta, counts, t, r_max)
        dest_flat = dest.reshape(t * TOP_K)
        item_j = jnp.where(live, -1, n_f - 1).astype(jnp.int32)
        x_rows = _gather_rows(row_tok, x1_slabs, d // LANES)
        y_rows = _moe_ffn(item_e, item_j, item_start, item_nsub, tail_block, x_rows, w_gate_up[layer],
                          b_gate_up[layer], w_down[layer], b_down[layer], r_max, tf)
        xt = _combine(dest_flat, y_rows, x1, meta, ln2_gain[layer], ln2_bias[layer], alpha)
    return xt.reshape(b, s, d)
```

```python
import functools
import math

import jax
import jax.numpy as jnp
from jax import lax
from jax.experimental import pallas as pl
from jax.experimental.pallas import tpu as pltpu

F32 = jnp.float32
BF16 = jnp.bfloat16

HEAD_DIM = 128
SB_HEADS = 8
RET_HEADS = 8
CHUNK = 64
ROPE_BASE = 10000.0
N_EXPERTS = 32
TOP_K = 4
SWIGLU_LIMIT = 7.0
SWIGLU_ALPHA = 1.702
LN_EPS = 1e-5
GN_EPS = 1e-5

V7X_VMEM_LIMIT_BYTES = 56 * 1024 * 1024
LANES = 128
ROW_BLOCK = 128
NEG_BIG = -1e30
EXP_UNDERFLOW = -105.0


def _cparams(sem, vmem=V7X_VMEM_LIMIT_BYTES):
    return pltpu.CompilerParams(dimension_semantics=sem, vmem_limit_bytes=vmem)


def _matmul_kernel(x_ref, w_ref, o_ref):
    o_ref[...] = jnp.dot(x_ref[...], w_ref[...], preferred_element_type=F32).astype(o_ref.dtype)


def _in_proj(x_bf, w_bf, tm=1024, tn=1024):
    t, d = x_bf.shape
    n = w_bf.shape[1]
    tm = min(tm, t)
    return pl.pallas_call(
        _matmul_kernel,
        grid=(n // tn, t // tm),
        in_specs=[pl.BlockSpec((tm, d), lambda j, i: (i, 0)),
                  pl.BlockSpec((d, tn), lambda j, i: (0, j))],
        out_specs=pl.BlockSpec((tm, tn), lambda j, i: (i, j)),
        out_shape=jax.ShapeDtypeStruct((t, n), BF16),
        compiler_params=_cparams(("arbitrary", "arbitrary")),
        name="in_proj",
    )(x_bf, w_bf)


def _sb_kernel(q_ref, k_ref, v_ref, o_ref, *, seq, scale, group, sub):
    blk = 128
    nq = seq // blk
    row = lax.broadcasted_iota(jnp.int32, (blk, blk), 0)
    col = lax.broadcasted_iota(jnp.int32, (blk, blk), 1)
    causal = col < row
    r2 = lax.broadcasted_iota(jnp.int32, (2 * blk, 2 * blk), 0)
    c2 = lax.broadcasted_iota(jnp.int32, (2 * blk, 2 * blk), 1)
    cum_rhs = jnp.where((c2 >= blk) | ((r2 & (blk - 1)) >= c2), 1.0, 0.0).astype(BF16)

    n_sub = group // sub
    causal_sub = jnp.concatenate([causal] * sub, axis=0)
    heads_of = lambda s: range(s * sub, (s + 1) * sub)
    cols = lambda g: slice(g * blk, (g + 1) * blk)

    def block(qbs, kb, carries, accs, masked):
        ks = pl.ds(pl.multiple_of(kb * blk, blk), blk)
        zs = [jnp.concatenate(
            [lax.dot_general(qbs[g], k_ref[ks, cols(g)], (((1,), (1,)), ((), ())),
                             preferred_element_type=F32) for g in heads_of(s)], axis=0) * scale
              for s in range(n_sub)]
        sums = []
        for z in zs:
            neg_z = -z
            lnb = jnp.minimum(neg_z, 0.0) - jnp.log(1.0 + jnp.exp(jnp.minimum(z, neg_z)))
            if masked:
                lnb = jnp.where(causal_sub, lnb, 0.0)
            hi = lnb.astype(BF16)
            lo = (lnb - hi.astype(F32)).astype(BF16)
            sums.append(jnp.dot(jnp.concatenate([hi, lo], axis=1), cum_rhs, preferred_element_type=F32))
        new_carries, new_accs = [], []
        for s in range(n_sub):
            incl = sums[s][:, :blk]
            total = sums[s][:, blk:]
            w = jnp.exp(zs[s] + incl + carries[s])
            if masked:
                w = jnp.where(causal_sub, w, 0.0)
            w = w.astype(BF16)
            for n, g in enumerate(heads_of(s)):
                new_accs.append(accs[g] + jnp.dot(w[n * blk:(n + 1) * blk], v_ref[ks, cols(g)],
                                                  preferred_element_type=F32))
            new_carries.append(carries[s] + total)
        return tuple(new_carries), tuple(new_accs)

    def q_body(qi, _):
        qs = pl.ds(pl.multiple_of(qi * blk, blk), blk)
        qbs = [q_ref[qs, cols(g)] for g in range(group)]
        state = block(qbs, qi, (jnp.zeros((sub * blk, blk), F32),) * n_sub,
                      (jnp.zeros((blk, blk), F32),) * group, True)

        def some_weight_left(carries):
            top = functools.reduce(jnp.maximum, carries)
            return (jnp.max(top) > EXP_UNDERFLOW).astype(jnp.int32)

        def live(st):
            t, more, _, _ = st
            return jnp.logical_and(t < qi, more > 0)

        def kb_body(st):
            t, _, carries, accs = st
            carries, accs = block(qbs, qi - 1 - t, carries, accs, False)
            return t + 1, some_weight_left(carries), carries, accs

        _, _, _, accs = lax.while_loop(live, kb_body, (0, 1, state[0], state[1]))
        for g in range(group):
            o_ref[qs, cols(g)] = accs[g].astype(o_ref.dtype)
        return 0

    lax.fori_loop(0, nq, q_body, 0)


def _stickbreak(proj, batch, seq, group=8, sub=4):
    t = batch * seq
    h = SB_HEADS
    n_groups = h // group
    width = group * HEAD_DIM
    kern = functools.partial(_sb_kernel, seq=seq, scale=1.0 / math.sqrt(HEAD_DIM), group=group,
                             sub=sub)
    spec = lambda off: pl.BlockSpec((seq, width), lambda b, hg: (b, off * n_groups + hg))
    return pl.pallas_call(
        kern,
        grid=(batch, n_groups),
        in_specs=[spec(0), spec(1), spec(2)],
        out_specs=pl.BlockSpec((seq, width), lambda b, hg: (b, hg)),
        out_shape=jax.ShapeDtypeStruct((t, h * HEAD_DIM), BF16),
        compiler_params=_cparams(("arbitrary", "arbitrary")),
        name="stickbreak",
    )(proj, proj, proj)


def _ret_kernel(q_ref, k_ref, v_ref, g_ref, cos_ref, sin_ref, intra_ref, qdec_ref, kdec_ref,
                cdec_ref, gain_ref, o_ref, *, seq, heads):
    n_chunks = seq // CHUNK
    half = HEAD_DIM // 2
    k_scale = HEAD_DIM ** -0.5
    cols = lambda h: slice(h * HEAD_DIM, (h + 1) * HEAD_DIM)
    contract_last = (((1,), (1,)), ((), ()))
    contract_rows = (((0,), (0,)), ((), ()))

    def chunk(n, states):
        rs = pl.ds(pl.multiple_of(n * CHUNK, CHUNK), CHUNK)
        cos = cos_ref[rs, :]
        sin = sin_ref[rs, :]
        qrs, krs, crosses, kvs = [], [], [], []
        for h in range(heads):
            q = q_ref[rs, cols(h)].astype(F32)
            k = k_ref[rs, cols(h)].astype(F32)
            qr = q * cos + pltpu.roll(q, half, 1) * sin
            kr = (k * cos + pltpu.roll(k, half, 1) * sin) * k_scale
            qrs.append(qr.astype(BF16))
            krs.append(kr.astype(BF16))
            crosses.append(jnp.dot((qr * qdec_ref[h]).astype(BF16), states[h].astype(BF16),
                                   preferred_element_type=F32))
            kvs.append(lax.dot_general((kr * kdec_ref[h]).astype(BF16), v_ref[rs, cols(h)], contract_rows,
                                       preferred_element_type=F32))
        scores = [lax.dot_general(qrs[h], krs[h], contract_last, preferred_element_type=F32) * intra_ref[h]
                  for h in range(heads)]
        outs = [crosses[h] + jnp.dot(scores[h].astype(BF16), v_ref[rs, cols(h)], preferred_element_type=F32)
                for h in range(heads)]
        new_states = []
        for h in range(heads):
            o = outs[h]
            mu = jnp.mean(o, axis=-1, keepdims=True)
            var = jnp.mean(jnp.square(o - mu), axis=-1, keepdims=True)
            on = (o - mu) * lax.rsqrt(var + GN_EPS)
            g = g_ref[rs, cols(h)].astype(F32)
            out = on * gain_ref[:, cols(h)] * (g * jax.nn.sigmoid(g))
            o_ref[rs, cols(h)] = out.astype(o_ref.dtype)
            new_states.append(states[h] * cdec_ref[h] + kvs[h])
        return tuple(new_states)

    lax.fori_loop(0, n_chunks, chunk, (jnp.zeros((HEAD_DIM, HEAD_DIM), F32),) * heads)


def _retention_tables(seq):
    d = HEAD_DIM
    inv_freq = ROPE_BASE ** (-jnp.arange(0, d, 2, dtype=F32) / d)
    ang = jnp.arange(seq, dtype=F32)[:, None] * inv_freq[None, :]
    cos, sin = jnp.cos(ang), jnp.sin(ang)
    cos_full = jnp.concatenate([cos, cos], axis=-1)
    sin_signed = jnp.concatenate([-sin, sin], axis=-1)
    log_gamma = jnp.log1p(-jnp.exp2(-5.0 - jnp.arange(RET_HEADS, dtype=F32)))
    i = jnp.arange(CHUNK, dtype=F32)
    intra = jnp.exp(log_gamma[:, None, None] * jnp.abs(i[:, None] - i[None, :]))
    k_decay = jnp.exp(log_gamma[:, None] * (CHUNK - 1 - i))
    q_decay = jnp.exp(log_gamma[:, None] * (i + 1.0))
    c_decay = jnp.exp(log_gamma * CHUNK)
    bc = lambda a: jnp.broadcast_to(a[..., None], a.shape + (d,))
    return cos_full, sin_signed, intra, bc(q_decay), bc(k_decay), bc(c_decay[:, None])


def _retention(proj, gn_gain, batch, seq):
    t = batch * seq
    h = RET_HEADS
    base = 3 * SB_HEADS
    cos_full, sin_signed, intra, qdec, kdec, cdec = _retention_tables(seq)
    width = h * HEAD_DIM
    group0 = base // h
    spec = lambda off: pl.BlockSpec((seq, width), lambda b: (b, group0 + off))
    full = pl.BlockSpec((seq, HEAD_DIM), lambda b: (0, 0))
    table = lambda r, c: pl.BlockSpec((h, r, c), lambda b: (0, 0, 0))
    return pl.pallas_call(
        functools.partial(_ret_kernel, seq=seq, heads=h),
        grid=(batch,),
        in_specs=[spec(0), spec(1), spec(2), spec(3), full, full,
                  table(CHUNK, CHUNK), table(CHUNK, HEAD_DIM), table(CHUNK, HEAD_DIM),
                  table(1, HEAD_DIM),
                  pl.BlockSpec((1, width), lambda b: (0, 0))],
        out_specs=pl.BlockSpec((seq, width), lambda b: (b, 0)),
        out_shape=jax.ShapeDtypeStruct((t, width), BF16),
        compiler_params=_cparams(("arbitrary",)),
        name="retention",
    )(proj, proj, proj, proj, cos_full, sin_signed, intra, qdec, kdec, cdec,
      gn_gain.reshape(1, h * HEAD_DIM).astype(F32))


def _layer_norm(hid, gain, bias):
    mu = jnp.mean(hid, axis=-1, keepdims=True)
    cen = hid - mu
    var = jnp.mean(jnp.square(cen), axis=-1, keepdims=True)
    return cen * lax.rsqrt(var + LN_EPS) * gain + bias


def _to_slabs(slab_ref, base, rows, value):
    chunks = value.shape[1] // LANES
    for c in range(chunks):
        slab_ref[pl.ds(base + c, rows, stride=chunks), :] = value[:, c * LANES:(c + 1) * LANES]


def _from_slabs(slab_ref, base, rows, chunks):
    return jnp.concatenate(
        [slab_ref[pl.ds(base + c, rows, stride=chunks), :] for c in range(chunks)], axis=1)


def _outproj_kernel(sb_ref, ret_ref, x_ref, w_ref, g_ref, b_ref, wr_ref, br_ref,
                    x1_ref, x1s_ref, meta_ref, cnt_ref, *, alpha, sb_width):
    tm = x_ref.shape[0]
    mix = jnp.dot(sb_ref[...], w_ref[:sb_width, :], preferred_element_type=F32)
    mix = mix + jnp.dot(ret_ref[...], w_ref[sb_width:, :], preferred_element_type=F32)
    x1 = _layer_norm(alpha * x_ref[...] + mix, g_ref[...], b_ref[...])
    x1_ref[...] = x1
    _to_slabs(x1s_ref, 0, tm, x1)

    logits = jnp.dot(x1, wr_ref[...], preferred_element_type=F32,
                     precision=lax.Precision.HIGHEST) + br_ref[...]
    lane = lax.broadcasted_iota(jnp.int32, (tm, LANES), 1).astype(F32)
    vals = logits
    tops, idxs, hots = [], [], []
    for _ in range(TOP_K):
        m = jnp.max(vals, axis=-1, keepdims=True)
        idx = jnp.min(jnp.where(vals == m, lane, float(LANES)), axis=-1, keepdims=True)
        hot = lane == idx
        vals = jnp.where(hot, NEG_BIG * 2.0, vals)
        tops.append(m)
        idxs.append(idx)
        hots.append(hot)
    exps = [jnp.exp(m - tops[0]) for m in tops]
    denom = exps[0] + exps[1] + exps[2] + exps[3]
    gates = [e / denom for e in exps]

    @pl.when(pl.program_id(0) == 0)
    def _():
        cnt_ref[...] = jnp.zeros_like(cnt_ref)

    multi = jnp.zeros((tm, LANES), F32)
    for hot in hots:
        multi = multi + jnp.where(hot, 1.0, 0.0)
    r = lax.broadcasted_iota(jnp.int32, (tm, tm), 0)
    c = lax.broadcasted_iota(jnp.int32, (tm, tm), 1)
    strict_lower = jnp.where(c < r, 1.0, 0.0).astype(BF16)
    before = jnp.dot(strict_lower, multi.astype(BF16), preferred_element_type=F32) + cnt_ref[...]
    cnt_ref[...] = cnt_ref[...] + jnp.sum(multi, axis=0, keepdims=True)

    meta = jnp.zeros((tm, LANES), F32)
    for kk in range(TOP_K):
        rank = jnp.sum(jnp.where(hots[kk], before, 0.0), axis=-1, keepdims=True)
        meta = jnp.where(lane == kk, idxs[kk], meta)
        meta = jnp.where(lane == TOP_K + kk, gates[kk], meta)
        meta = jnp.where(lane == 2 * TOP_K + kk, rank, meta)
    meta_ref[...] = meta


def _outproj(sb, ret, xt, w_out_bf, ln_g, ln_b, w_router, b_router, alpha, tm=256):
    t, d = xt.shape
    tm = min(tm, t)
    sbw = sb.shape[1]
    wr = jnp.zeros((d, LANES), F32).at[:, :N_EXPERTS].set(w_router.astype(F32))
    br = jnp.full((1, LANES), NEG_BIG, F32).at[0, :N_EXPERTS].set(b_router.astype(F32))
    row = lambda w: pl.BlockSpec((tm, w), lambda i: (i, 0))
    const = lambda r, c: pl.BlockSpec((r, c), lambda i: (0, 0))
    return pl.pallas_call(
        functools.partial(_outproj_kernel, alpha=alpha, sb_width=sbw),
        grid=(t // tm,),
        in_specs=[row(sbw), row(ret.shape[1]), row(d), const(d, d), const(1, d), const(1, d),
                  const(d, LANES), const(1, LANES)],
        out_specs=[row(d), pl.BlockSpec((tm * (d // LANES), LANES), lambda i: (i, 0)), row(LANES),
                   const(1, LANES)],
        out_shape=[jax.ShapeDtypeStruct((t, d), F32),
                   jax.ShapeDtypeStruct((t * (d // LANES), LANES), F32),
                   jax.ShapeDtypeStruct((t, LANES), F32),
                   jax.ShapeDtypeStruct((1, LANES), F32)],
        compiler_params=_cparams(("arbitrary",)),
        name="outproj_ln_router",
    )(sb, ret, xt, w_out_bf, ln_g.reshape(1, d).astype(F32), ln_b.reshape(1, d).astype(F32), wr, br)


DMA_UNROLL = 8


def _wait_slabs(hbm, buf, sem, slot, count, chunks):
    def body(n, _):
        pltpu.make_async_copy(hbm.at[pl.ds(0, chunks), :], buf.at[slot, pl.ds(0, chunks), :],
                              sem.at[slot]).wait()
        return 0
    lax.fori_loop(0, count, body, 0, unroll=DMA_UNROLL)


def _gather_kernel(tok_ref, x_hbm, o_ref, buf, sem, *, rows, chunks):
    i = pl.program_id(0)

    def issue_tile(step, slot):
        def body(r, _):
            src = pl.ds(pl.multiple_of(tok_ref[step * rows + r] * chunks, chunks), chunks)
            dst = pl.ds(pl.multiple_of(r * chunks, chunks), chunks)
            pltpu.make_async_copy(x_hbm.at[src, :], buf.at[slot, dst, :], sem.at[slot]).start()
            return 0
        lax.fori_loop(0, rows, body, 0, unroll=DMA_UNROLL)

    @pl.when(i == 0)
    def _():
        issue_tile(0, 0)

    @pl.when(i + 1 < pl.num_programs(0))
    def _():
        issue_tile(i + 1, (i + 1) % 2)

    slot = i % 2
    _wait_slabs(x_hbm, buf, sem, slot, rows, chunks)
    o_ref[...] = _from_slabs(buf.at[slot], 0, rows, chunks).astype(o_ref.dtype)


def _gather_rows(row_tok, x1_slabs, chunks, rows=256):
    n_rows = row_tok.shape[0]
    return pl.pallas_call(
        functools.partial(_gather_kernel, rows=rows, chunks=chunks),
        grid_spec=pltpu.PrefetchScalarGridSpec(
            num_scalar_prefetch=1,
            grid=(n_rows // rows,),
            in_specs=[pl.BlockSpec(memory_space=pl.ANY)],
            out_specs=pl.BlockSpec((rows, chunks * LANES), lambda i, tok: (i, 0)),
            scratch_shapes=[pltpu.VMEM((2, rows * chunks, LANES), F32), pltpu.SemaphoreType.DMA((2,))],
        ),
        out_shape=jax.ShapeDtypeStruct((n_rows, chunks * LANES), BF16),
        compiler_params=_cparams(("arbitrary",)),
        name="gather_rows",
    )(row_tok, x1_slabs)


def _moe_kernel(ie_ref, ij_ref, ist_ref, ins_ref, tail_ref, live_ref, x_hbm, wgu_hbm, wd_hbm, bg_ref, bu_ref,
                bd_ref, y_hbm, x_vmem, acc, wg_f, wu_f, wd_f, wg_bf, wu_bf, wd_bf, stage, sem_in, sem_out,
                sem_w, *, n_f):
    del ij_ref
    i = pl.program_id(0)
    j = pl.program_id(1)
    nsub = ins_ref[i]
    start = ist_ref[i]
    d = acc.shape[1]
    tf = wg_bf.shape[1]
    d_ff = n_f * tf
    chunks = d // LANES
    block_rows = ROW_BLOCK * chunks

    def weight_copies(e, jj, slot):
        col_g = pl.ds(pl.multiple_of(jj * tf, tf), tf)
        col_u = pl.ds(pl.multiple_of(d_ff + jj * tf, tf), tf)
        copies = []
        for half in range(2):
            rows_k = pl.ds(half * (d // 2), d // 2)
            rows_f = pl.ds(pl.multiple_of(jj * tf + half * (tf // 2), tf // 2), tf // 2)
            dst_f = pl.ds(half * (tf // 2), tf // 2)
            copies += [
                pltpu.make_async_copy(wgu_hbm.at[e, rows_k, col_g], wg_f.at[slot, rows_k, :], sem_w.at[slot]),
                pltpu.make_async_copy(wgu_hbm.at[e, rows_k, col_u], wu_f.at[slot, rows_k, :], sem_w.at[slot]),
                pltpu.make_async_copy(wd_hbm.at[e, rows_f, :], wd_f.at[slot, dst_f, :], sem_w.at[slot]),
            ]
        return copies

    def start_weights(e, jj, slot):
        for n, c in enumerate(weight_copies(e, jj, slot)):
            c.start(priority=n % 2)

    step = i * n_f + j
    w_slot = step % 2

    @pl.when(step == 0)
    def _():
        start_weights(ie_ref[0], 0, 0)

    @pl.when(step + 1 < live_ref[0])
    def _():
        last = j == n_f - 1
        start_weights(ie_ref[jnp.where(last, i + 1, i)], jnp.where(last, 0, j + 1), 1 - w_slot)

    def rows_of(r):
        return pl.ds(pl.multiple_of(r * ROW_BLOCK, ROW_BLOCK), ROW_BLOCK)

    def slabs_of(block):
        return pl.ds(pl.multiple_of(block * block_rows, block_rows), block_rows)

    def for_each(lo, hi, fn):
        def body(r, _):
            fn(r)
            return 0
        lax.fori_loop(lo, hi, body, 0)

    @pl.when((i == 0) & (j == 0))
    def _():
        stage[0] = jnp.zeros(stage.shape[1:], F32)

        def zero_copy(bk):
            return pltpu.make_async_copy(stage.at[0], y_hbm.at[slabs_of(bk), :], sem_out.at[0])

        n_blocks = y_hbm.shape[0] // block_rows
        for_each(tail_ref[0], n_blocks, lambda bk: zero_copy(bk).start())
        for_each(tail_ref[0], n_blocks, lambda bk: zero_copy(bk).wait())

    first_block = start // ROW_BLOCK

    def in_copy(r):
        src = pl.ds(pl.multiple_of(start + r * ROW_BLOCK, ROW_BLOCK), ROW_BLOCK)
        return pltpu.make_async_copy(x_hbm.at[src, :], x_vmem.at[rows_of(r), :], sem_in)

    def out_copy(r, slot):
        return pltpu.make_async_copy(stage.at[slot], y_hbm.at[slabs_of(first_block + r), :],
                                     sem_out.at[slot])

    def span(r, n_blocks):
        return pl.ds(pl.multiple_of(r * ROW_BLOCK, ROW_BLOCK), n_blocks * ROW_BLOCK)

    def hidden(rows):
        xb = x_vmem[rows, :]
        gate = jnp.dot(xb, wg_bf[...], preferred_element_type=F32) + bg_ref[...]
        up = jnp.dot(xb, wu_bf[...], preferred_element_type=F32) + bu_ref[...]
        gate = jnp.minimum(gate, SWIGLU_LIMIT)
        up = jnp.clip(up, -SWIGLU_LIMIT, SWIGLU_LIMIT)
        act = (up + 1.0) * (gate * jax.nn.sigmoid(SWIGLU_ALPHA * gate))
        return jnp.dot(act.astype(BF16), wd_bf[...], preferred_element_type=F32)

    @pl.when(nsub > 0)
    def _():
        @pl.when(j == 0)
        def _():
            for_each(0, nsub, lambda r: in_copy(r).start())
            bias_rows = jnp.broadcast_to(bd_ref[...], (ROW_BLOCK, d))

            def init(r):
                acc[rows_of(r), :] = bias_rows
            for_each(0, nsub, init)
            for_each(0, nsub, lambda r: in_copy(r).wait())

        for c in weight_copies(ie_ref[i], j, w_slot):
            c.wait()
        wg_bf[...] = wg_f[w_slot].astype(BF16)
        wu_bf[...] = wu_f[w_slot].astype(BF16)
        wd_bf[...] = wd_f[w_slot].astype(BF16)

        n_pairs = nsub // 2
        odd = nsub % 2 == 1

        @pl.when(j < n_f - 1)
        def _():
            def pair(p):
                acc[span(2 * p, 2), :] += hidden(span(2 * p, 2))
            for_each(0, n_pairs, pair)

            @pl.when(odd)
            def _():
                acc[span(nsub - 1, 1), :] += hidden(span(nsub - 1, 1))

        @pl.when(j == n_f - 1)
        def _():
            def write_out(r, slot, final, reuse):
                @pl.when(reuse)
                def _():
                    out_copy(r, slot).wait()
                _to_slabs(stage.at[slot], 0, ROW_BLOCK, final)
                out_copy(r, slot).start()

            def pair(p):
                final = acc[span(2 * p, 2), :] + hidden(span(2 * p, 2))
                write_out(2 * p, 0, final[:ROW_BLOCK], p > 0)
                write_out(2 * p + 1, 1, final[ROW_BLOCK:], p > 0)
            for_each(0, n_pairs, pair)

            @pl.when(odd)
            def _():
                final = acc[span(nsub - 1, 1), :] + hidden(span(nsub - 1, 1))
                write_out(nsub - 1, 0, final, n_pairs > 0)

            out_copy(0, 0).wait()

            @pl.when(n_pairs > 0)
            def _():
                out_copy(0, 1).wait()


def _moe_ffn(item_e, item_j, item_start, item_nsub, tail_block, live_steps, x_rows, w_gate_up, b_gate_up,
             w_down, b_down, r_max, tf=512):
    n_e, d, two_f = w_gate_up.shape
    chunks = d // LANES
    d_ff = two_f // 2
    n_f = d_ff // tf
    n_items = item_e.shape[0]

    def jf(i, j, ij):
        return jnp.where(ij[i] < 0, j, ij[i])

    hbm = pl.BlockSpec(memory_space=pl.ANY)
    in_specs = [
        hbm, hbm, hbm,
        pl.BlockSpec((None, 1, tf), lambda i, j, ie, ij, ist, ins, tl, lv: (ie[i], 0, jf(i, j, ij))),
        pl.BlockSpec((None, 1, tf), lambda i, j, ie, ij, ist, ins, tl, lv: (ie[i], 0, n_f + jf(i, j, ij))),
        pl.BlockSpec((None, 1, d), lambda i, j, ie, ij, ist, ins, tl, lv: (ie[i], 0, 0)),
    ]
    return pl.pallas_call(
        functools.partial(_moe_kernel, n_f=n_f),
        grid_spec=pltpu.PrefetchScalarGridSpec(
            num_scalar_prefetch=6,
            grid=(n_items, n_f),
            in_specs=in_specs,
            out_specs=pl.BlockSpec(memory_space=pl.ANY),
            scratch_shapes=[pltpu.VMEM((r_max, d), BF16), pltpu.VMEM((r_max, d), F32),
                            pltpu.VMEM((2, d, tf), F32), pltpu.VMEM((2, d, tf), F32),
                            pltpu.VMEM((2, tf, d), F32),
                            pltpu.VMEM((d, tf), BF16), pltpu.VMEM((d, tf), BF16),
                            pltpu.VMEM((tf, d), BF16),
                            pltpu.VMEM((2, ROW_BLOCK * chunks, LANES), F32),
                            pltpu.SemaphoreType.DMA(()), pltpu.SemaphoreType.DMA((2,)),
                            pltpu.SemaphoreType.DMA((2,))],
        ),
        out_shape=jax.ShapeDtypeStruct((x_rows.shape[0] * chunks, LANES), F32),
        compiler_params=_cparams(("arbitrary", "arbitrary")),
        name="moe_ffn",
    )(item_e, item_j, item_start, item_nsub, tail_block, live_steps, x_rows, w_gate_up, w_down,
      b_gate_up.reshape(n_e, 1, two_f), b_gate_up.reshape(n_e, 1, two_f), b_down.reshape(n_e, 1, d))


def _combine_kernel(dest_ref, y_hbm, x1_ref, meta_ref, g_ref, b_ref, o_ref, buf, sem, *, alpha, tm, chunks):
    i = pl.program_id(0)
    per_step = tm * TOP_K

    def issue_tile(step, slot):
        def body(tt, _):
            for kk in range(TOP_K):
                src = pl.ds(pl.multiple_of(dest_ref[step * per_step + tt * TOP_K + kk] * chunks, chunks),
                            chunks)
                dst = pl.ds(pl.multiple_of((kk * tm + tt) * chunks, chunks), chunks)
                pltpu.make_async_copy(y_hbm.at[src, :], buf.at[slot, dst, :], sem.at[slot]).start()
            return 0
        lax.fori_loop(0, tm, body, 0, unroll=DMA_UNROLL // TOP_K)

    @pl.when(i == 0)
    def _():
        issue_tile(0, 0)

    @pl.when(i + 1 < pl.num_programs(0))
    def _():
        issue_tile(i + 1, (i + 1) % 2)

    slot = i % 2
    _wait_slabs(y_hbm, buf, sem, slot, per_step, chunks)
    meta = meta_ref[...]
    y = jnp.zeros(x1_ref.shape, F32)
    for kk in range(TOP_K):
        rows = _from_slabs(buf.at[slot], kk * tm * chunks, tm, chunks)
        y = y + meta[:, TOP_K + kk:TOP_K + kk + 1] * rows
    o_ref[...] = _layer_norm(alpha * x1_ref[...] + y, g_ref[...], b_ref[...])


def _combine(dest_flat, y_rows, x1, meta, ln_g, ln_b, alpha, tm=128):
    t, d = x1.shape
    tm = min(tm, t)
    chunks = d // LANES
    row = lambda w: pl.BlockSpec((tm, w), lambda i, dest: (i, 0))
    const = pl.BlockSpec((1, d), lambda i, dest: (0, 0))
    return pl.pallas_call(
        functools.partial(_combine_kernel, alpha=alpha, tm=tm, chunks=chunks),
        grid_spec=pltpu.PrefetchScalarGridSpec(
            num_scalar_prefetch=1,
            grid=(t // tm,),
            in_specs=[pl.BlockSpec(memory_space=pl.ANY), row(d), row(LANES), const, const],
            out_specs=row(d),
            scratch_shapes=[pltpu.VMEM((2, TOP_K * tm * chunks, LANES), F32),
                            pltpu.SemaphoreType.DMA((2,))],
        ),
        out_shape=jax.ShapeDtypeStruct((t, d), F32),
        compiler_params=_cparams(("arbitrary",)),
        name="combine_ln",
    )(dest_flat, y_rows, x1, meta, ln_g.reshape(1, d).astype(F32), ln_b.reshape(1, d).astype(F32))


def _routing_plan(meta, counts_f, t, r_max):
    idx = meta[:, 0:TOP_K].astype(jnp.int32)
    rank = meta[:, 2 * TOP_K:3 * TOP_K].astype(jnp.int32)
    counts = counts_f[0, :N_EXPERTS].astype(jnp.int32)
    n128 = (counts + ROW_BLOCK - 1) // ROW_BLOCK
    padded = n128 * ROW_BLOCK
    pad_start = jnp.cumsum(padded) - padded
    dest = pad_start[idx] + rank
    tk = t * TOP_K
    n_rows = (tk + ROW_BLOCK - 1) // ROW_BLOCK * ROW_BLOCK + N_EXPERTS * ROW_BLOCK
    flat_tok = jnp.arange(tk, dtype=jnp.int32) // TOP_K
    row_tok = jnp.zeros((n_rows,), jnp.int32).at[dest.reshape(tk)].set(flat_tok)

    subs = r_max // ROW_BLOCK
    n_items = N_EXPERTS + (n_rows // ROW_BLOCK - N_EXPERTS) // subs
    items_e = (n128 + subs - 1) // subs
    items_end = jnp.cumsum(items_e)
    total = items_end[-1]
    slot = jnp.arange(n_items, dtype=jnp.int32)
    live = slot < total
    s_eff = jnp.minimum(slot, total - 1)
    e = jnp.minimum(jnp.searchsorted(items_end, s_eff, side='right'), N_EXPERTS - 1).astype(jnp.int32)
    local = s_eff - (items_end[e] - items_e[e])
    item_start = (pad_start[e] + local * r_max).astype(jnp.int32)
    item_nsub = jnp.where(live, jnp.clip(n128[e] - local * subs, 0, subs), 0).astype(jnp.int32)
    tail_block = jnp.sum(n128).reshape(1).astype(jnp.int32)
    return dest, row_tok, e, live, item_start, item_nsub, tail_block


def kernel(x, w_in, ret_gn_gain, w_out, ln1_gain, ln1_bias, w_router, b_router, w_gate_up, b_gate_up,
           w_down, b_down, ln2_gain, ln2_bias):
    b, s, d = x.shape
    t = b * s
    depth = w_in.shape[0]
    alpha = (2 * depth) ** 0.25
    r_max = 1280
    tf = 512
    n_f = (w_gate_up.shape[-1] // 2) // tf
    xt = x.reshape(t, d)
    for layer in range(depth):
        proj = _in_proj(xt.astype(BF16), w_in[layer].astype(BF16))
        sb = _stickbreak(proj, b, s)
        ret = _retention(proj, ret_gn_gain[layer], b, s)
        x1, x1_slabs, meta, counts = _outproj(sb, ret, xt, w_out[layer].astype(BF16), ln1_gain[layer],
                                              ln1_bias[layer], w_router[layer], b_router[layer], alpha)
        dest, row_tok, item_e, live, item_start, item_nsub, tail_block = _routing_plan(
            meta, counts, t, r_max)
        dest_flat = dest.reshape(t * TOP_K)
        item_j = jnp.where(live, -1, n_f - 1).astype(jnp.int32)
        live_steps = (jnp.sum(live.astype(jnp.int32)) * n_f).reshape(1)
        x_rows = _gather_rows(row_tok, x1_slabs, d // LANES)
        y_rows = _moe_ffn(item_e, item_j, item_start, item_nsub, tail_block, live_steps, x_rows,
                          w_gate_up[layer], b_gate_up[layer], w_down[layer], b_down[layer], r_max, tf)
        xt = _combine(dest_flat, y_rows, x1, meta, ln2_gain[layer], ln2_bias[layer], alpha)
    return xt.reshape(b, s, d)
```

```python
import functools
import math

import jax
import jax.numpy as jnp
from jax import lax
from jax.experimental import pallas as pl
from jax.experimental.pallas import tpu as pltpu

F32 = jnp.float32
BF16 = jnp.bfloat16

HEAD_DIM = 128
SB_HEADS = 8
RET_HEADS = 8
CHUNK = 64
ROPE_BASE = 10000.0
N_EXPERTS = 32
TOP_K = 4
SWIGLU_LIMIT = 7.0
SWIGLU_ALPHA = 1.702
LN_EPS = 1e-5
GN_EPS = 1e-5

V7X_VMEM_LIMIT_BYTES = 56 * 1024 * 1024
LANES = 128
ROW_BLOCK = 128
NEG_BIG = -1e30
EXP_UNDERFLOW = -105.0


def _cparams(sem, vmem=V7X_VMEM_LIMIT_BYTES):
    return pltpu.CompilerParams(dimension_semantics=sem, vmem_limit_bytes=vmem)


def _matmul_kernel(x_ref, w_ref, o_ref):
    o_ref[...] = jnp.dot(x_ref[...], w_ref[...], preferred_element_type=F32).astype(o_ref.dtype)


def _in_proj(x_bf, w_bf, tm=1024, tn=1024):
    t, d = x_bf.shape
    n = w_bf.shape[1]
    tm = min(tm, t)
    return pl.pallas_call(
        _matmul_kernel,
        grid=(n // tn, t // tm),
        in_specs=[pl.BlockSpec((tm, d), lambda j, i: (i, 0)),
                  pl.BlockSpec((d, tn), lambda j, i: (0, j))],
        out_specs=pl.BlockSpec((tm, tn), lambda j, i: (i, j)),
        out_shape=jax.ShapeDtypeStruct((t, n), BF16),
        compiler_params=_cparams(("arbitrary", "arbitrary")),
        name="in_proj",
    )(x_bf, w_bf)


def _sb_kernel(q_ref, k_ref, v_ref, o_ref, *, seq, scale, group, sub):
    blk = 128
    nq = seq // blk
    row = lax.broadcasted_iota(jnp.int32, (blk, blk), 0)
    col = lax.broadcasted_iota(jnp.int32, (blk, blk), 1)
    causal = col < row
    r2 = lax.broadcasted_iota(jnp.int32, (2 * blk, 2 * blk), 0)
    c2 = lax.broadcasted_iota(jnp.int32, (2 * blk, 2 * blk), 1)
    cum_rhs = jnp.where((c2 >= blk) | ((r2 & (blk - 1)) >= c2), 1.0, 0.0).astype(BF16)

    n_sub = group // sub
    causal_sub = jnp.concatenate([causal] * sub, axis=0)
    heads_of = lambda s: range(s * sub, (s + 1) * sub)
    cols = lambda g: slice(g * blk, (g + 1) * blk)

    def block(qbs, kb, carries, accs, masked):
        ks = pl.ds(pl.multiple_of(kb * blk, blk), blk)
        zs = [jnp.concatenate(
            [lax.dot_general(qbs[g], k_ref[ks, cols(g)], (((1,), (1,)), ((), ())),
                             preferred_element_type=F32) for g in heads_of(s)], axis=0) * scale
              for s in range(n_sub)]
        sums = []
        for z in zs:
            neg_z = -z
            lnb = jnp.minimum(neg_z, 0.0) - jnp.log(1.0 + jnp.exp(jnp.minimum(z, neg_z)))
            if masked:
                lnb = jnp.where(causal_sub, lnb, 0.0)
            hi = lnb.astype(BF16)
            lo = (lnb - hi.astype(F32)).astype(BF16)
            sums.append(jnp.dot(jnp.concatenate([hi, lo], axis=1), cum_rhs, preferred_element_type=F32))
        new_carries, new_accs = [], []
        for s in range(n_sub):
            incl = sums[s][:, :blk]
            total = sums[s][:, blk:]
            w = jnp.exp(zs[s] + incl + carries[s])
            if masked:
                w = jnp.where(causal_sub, w, 0.0)
            w = w.astype(BF16)
            for n, g in enumerate(heads_of(s)):
                new_accs.append(accs[g] + jnp.dot(w[n * blk:(n + 1) * blk], v_ref[ks, cols(g)],
                                                  preferred_element_type=F32))
            new_carries.append(carries[s] + total)
        return tuple(new_carries), tuple(new_accs)

    def q_body(qi, _):
        qs = pl.ds(pl.multiple_of(qi * blk, blk), blk)
        qbs = [q_ref[qs, cols(g)] for g in range(group)]
        state = block(qbs, qi, (jnp.zeros((sub * blk, blk), F32),) * n_sub,
                      (jnp.zeros((blk, blk), F32),) * group, True)

        def some_weight_left(carries):
            top = functools.reduce(jnp.maximum, carries)
            return (jnp.max(top) > EXP_UNDERFLOW).astype(jnp.int32)

        def live(st):
            t, more, _, _ = st
            return jnp.logical_and(t < qi, more > 0)

        def kb_body(st):
            t, _, carries, accs = st
            carries, accs = block(qbs, qi - 1 - t, carries, accs, False)
            return t + 1, some_weight_left(carries), carries, accs

        _, _, _, accs = lax.while_loop(live, kb_body, (0, 1, state[0], state[1]))
        for g in range(group):
            o_ref[qs, cols(g)] = accs[g].astype(o_ref.dtype)
        return 0

    lax.fori_loop(0, nq, q_body, 0)


def _stickbreak(proj, batch, seq, group=8, sub=4):
    t = batch * seq
    h = SB_HEADS
    n_groups = h // group
    width = group * HEAD_DIM
    kern = functools.partial(_sb_kernel, seq=seq, scale=1.0 / math.sqrt(HEAD_DIM), group=group,
                             sub=sub)
    spec = lambda off: pl.BlockSpec((seq, width), lambda b, hg: (b, off * n_groups + hg))
    return pl.pallas_call(
        kern,
        grid=(batch, n_groups),
        in_specs=[spec(0), spec(1), spec(2)],
        out_specs=pl.BlockSpec((seq, width), lambda b, hg: (b, hg)),
        out_shape=jax.ShapeDtypeStruct((t, h * HEAD_DIM), BF16),
        compiler_params=_cparams(("arbitrary", "arbitrary")),
        name="stickbreak",
    )(proj, proj, proj)


def _ret_kernel(q_ref, k_ref, v_ref, g_ref, cos_ref, sin_ref, intra_ref, qdec_ref, kdec_ref,
                cdec_ref, gain_ref, o_ref, *, seq, heads):
    n_chunks = seq // CHUNK
    half = HEAD_DIM // 2
    k_scale = HEAD_DIM ** -0.5
    cols = lambda h: slice(h * HEAD_DIM, (h + 1) * HEAD_DIM)
    contract_last = (((1,), (1,)), ((), ()))
    contract_rows = (((0,), (0,)), ((), ()))

    def chunk(n, states):
        rs = pl.ds(pl.multiple_of(n * CHUNK, CHUNK), CHUNK)
        cos = cos_ref[rs, :]
        sin = sin_ref[rs, :]
        qrs, krs, crosses, kvs = [], [], [], []
        for h in range(heads):
            q = q_ref[rs, cols(h)].astype(F32)
            k = k_ref[rs, cols(h)].astype(F32)
            qr = q * cos + pltpu.roll(q, half, 1) * sin
            kr = (k * cos + pltpu.roll(k, half, 1) * sin) * k_scale
            qrs.append(qr.astype(BF16))
            krs.append(kr.astype(BF16))
            crosses.append(jnp.dot((qr * qdec_ref[h]).astype(BF16), states[h].astype(BF16),
                                   preferred_element_type=F32))
            kvs.append(lax.dot_general((kr * kdec_ref[h]).astype(BF16), v_ref[rs, cols(h)], contract_rows,
                                       preferred_element_type=F32))
        scores = [lax.dot_general(qrs[h], krs[h], contract_last, preferred_element_type=F32) * intra_ref[h]
                  for h in range(heads)]
        outs = [crosses[h] + jnp.dot(scores[h].astype(BF16), v_ref[rs, cols(h)], preferred_element_type=F32)
                for h in range(heads)]
        new_states = []
        for h in range(heads):
            o = outs[h]
            mu = jnp.mean(o, axis=-1, keepdims=True)
            var = jnp.mean(jnp.square(o - mu), axis=-1, keepdims=True)
            on = (o - mu) * lax.rsqrt(var + GN_EPS)
            g = g_ref[rs, cols(h)].astype(F32)
            out = on * gain_ref[:, cols(h)] * (g * jax.nn.sigmoid(g))
            o_ref[rs, cols(h)] = out.astype(o_ref.dtype)
            new_states.append(states[h] * cdec_ref[h] + kvs[h])
        return tuple(new_states)

    lax.fori_loop(0, n_chunks, chunk, (jnp.zeros((HEAD_DIM, HEAD_DIM), F32),) * heads)


def _retention_tables(seq):
    d = HEAD_DIM
    inv_freq = ROPE_BASE ** (-jnp.arange(0, d, 2, dtype=F32) / d)
    ang = jnp.arange(seq, dtype=F32)[:, None] * inv_freq[None, :]
    cos, sin = jnp.cos(ang), jnp.sin(ang)
    cos_full = jnp.concatenate([cos, cos], axis=-1)
    sin_signed = jnp.concatenate([-sin, sin], axis=-1)
    log_gamma = jnp.log1p(-jnp.exp2(-5.0 - jnp.arange(RET_HEADS, dtype=F32)))
    i = jnp.arange(CHUNK, dtype=F32)
    intra = jnp.exp(log_gamma[:, None, None] * jnp.abs(i[:, None] - i[None, :]))
    k_decay = jnp.exp(log_gamma[:, None] * (CHUNK - 1 - i))
    q_decay = jnp.exp(log_gamma[:, None] * (i + 1.0))
    c_decay = jnp.exp(log_gamma * CHUNK)
    bc = lambda a: jnp.broadcast_to(a[..., None], a.shape + (d,))
    return cos_full, sin_signed, intra, bc(q_decay), bc(k_decay), bc(c_decay[:, None])


def _retention(proj, gn_gain, batch, seq):
    t = batch * seq
    h = RET_HEADS
    base = 3 * SB_HEADS
    cos_full, sin_signed, intra, qdec, kdec, cdec = _retention_tables(seq)
    width = h * HEAD_DIM
    group0 = base // h
    spec = lambda off: pl.BlockSpec((seq, width), lambda b: (b, group0 + off))
    full = pl.BlockSpec((seq, HEAD_DIM), lambda b: (0, 0))
    table = lambda r, c: pl.BlockSpec((h, r, c), lambda b: (0, 0, 0))
    return pl.pallas_call(
        functools.partial(_ret_kernel, seq=seq, heads=h),
        grid=(batch,),
        in_specs=[spec(0), spec(1), spec(2), spec(3), full, full,
                  table(CHUNK, CHUNK), table(CHUNK, HEAD_DIM), table(CHUNK, HEAD_DIM),
                  table(1, HEAD_DIM),
                  pl.BlockSpec((1, width), lambda b: (0, 0))],
        out_specs=pl.BlockSpec((seq, width), lambda b: (b, 0)),
        out_shape=jax.ShapeDtypeStruct((t, width), BF16),
        compiler_params=_cparams(("arbitrary",)),
        name="retention",
    )(proj, proj, proj, proj, cos_full, sin_signed, intra, qdec, kdec, cdec,
      gn_gain.reshape(1, h * HEAD_DIM).astype(F32))


def _layer_norm(hid, gain, bias):
    mu = jnp.mean(hid, axis=-1, keepdims=True)
    cen = hid - mu
    var = jnp.mean(jnp.square(cen), axis=-1, keepdims=True)
    return cen * lax.rsqrt(var + LN_EPS) * gain + bias


def _to_slabs(slab_ref, base, rows, value):
    chunks = value.shape[1] // LANES
    for c in range(chunks):
        slab_ref[pl.ds(base + c, rows, stride=chunks), :] = value[:, c * LANES:(c + 1) * LANES]


def _from_slabs(slab_ref, base, rows, chunks):
    return jnp.concatenate(
        [slab_ref[pl.ds(base + c, rows, stride=chunks), :] for c in range(chunks)], axis=1)


def _outproj_kernel(sb_ref, ret_ref, x_ref, w_ref, g_ref, b_ref, wr_ref, br_ref,
                    x1_ref, x1s_ref, meta_ref, cnt_ref, *, alpha, sb_width):
    tm = x_ref.shape[0]
    mix = jnp.dot(sb_ref[...], w_ref[:sb_width, :], preferred_element_type=F32)
    mix = mix + jnp.dot(ret_ref[...], w_ref[sb_width:, :], preferred_element_type=F32)
    x1 = _layer_norm(alpha * x_ref[...] + mix, g_ref[...], b_ref[...])
    x1_ref[...] = x1
    _to_slabs(x1s_ref, 0, tm, x1)

    logits = jnp.dot(x1.astype(BF16), wr_ref[...], preferred_element_type=F32) + br_ref[...]
    lane = lax.broadcasted_iota(jnp.int32, (tm, LANES), 1).astype(F32)
    vals = logits
    tops, idxs, hots = [], [], []
    for _ in range(TOP_K):
        m = jnp.max(vals, axis=-1, keepdims=True)
        idx = jnp.min(jnp.where(vals == m, lane, float(LANES)), axis=-1, keepdims=True)
        hot = lane == idx
        vals = jnp.where(hot, NEG_BIG * 2.0, vals)
        tops.append(m)
        idxs.append(idx)
        hots.append(hot)
    exps = [jnp.exp(m - tops[0]) for m in tops]
    denom = exps[0] + exps[1] + exps[2] + exps[3]
    gates = [e / denom for e in exps]

    @pl.when(pl.program_id(0) == 0)
    def _():
        cnt_ref[...] = jnp.zeros_like(cnt_ref)

    multi = jnp.zeros((tm, LANES), F32)
    for hot in hots:
        multi = multi + jnp.where(hot, 1.0, 0.0)
    r = lax.broadcasted_iota(jnp.int32, (tm, tm), 0)
    c = lax.broadcasted_iota(jnp.int32, (tm, tm), 1)
    strict_lower = jnp.where(c < r, 1.0, 0.0).astype(BF16)
    before = jnp.dot(strict_lower, multi.astype(BF16), preferred_element_type=F32) + cnt_ref[...]
    cnt_ref[...] = cnt_ref[...] + jnp.sum(multi, axis=0, keepdims=True)

    meta = jnp.zeros((tm, LANES), F32)
    for kk in range(TOP_K):
        rank = jnp.sum(jnp.where(hots[kk], before, 0.0), axis=-1, keepdims=True)
        meta = jnp.where(lane == kk, idxs[kk], meta)
        meta = jnp.where(lane == TOP_K + kk, gates[kk], meta)
        meta = jnp.where(lane == 2 * TOP_K + kk, rank, meta)
    meta_ref[...] = meta


def _outproj(sb, ret, xt, w_out_bf, ln_g, ln_b, w_router, b_router, alpha, tm=256):
    t, d = xt.shape
    tm = min(tm, t)
    sbw = sb.shape[1]
    wr = jnp.zeros((d, LANES), BF16).at[:, :N_EXPERTS].set(w_router.astype(BF16))
    br = jnp.full((1, LANES), NEG_BIG, F32).at[0, :N_EXPERTS].set(b_router.astype(F32))
    row = lambda w: pl.BlockSpec((tm, w), lambda i: (i, 0))
    const = lambda r, c: pl.BlockSpec((r, c), lambda i: (0, 0))
    return pl.pallas_call(
        functools.partial(_outproj_kernel, alpha=alpha, sb_width=sbw),
        grid=(t // tm,),
        in_specs=[row(sbw), row(ret.shape[1]), row(d), const(d, d), const(1, d), const(1, d),
                  const(d, LANES), const(1, LANES)],
        out_specs=[row(d), pl.BlockSpec((tm * (d // LANES), LANES), lambda i: (i, 0)), row(LANES),
                   const(1, LANES)],
        out_shape=[jax.ShapeDtypeStruct((t, d), F32),
                   jax.ShapeDtypeStruct((t * (d // LANES), LANES), F32),
                   jax.ShapeDtypeStruct((t, LANES), F32),
                   jax.ShapeDtypeStruct((1, LANES), F32)],
        compiler_params=_cparams(("arbitrary",)),
        name="outproj_ln_router",
    )(sb, ret, xt, w_out_bf, ln_g.reshape(1, d).astype(F32), ln_b.reshape(1, d).astype(F32), wr, br)


DMA_UNROLL = 8


def _wait_slabs(hbm, buf, sem, slot, count, chunks):
    def body(n, _):
        pltpu.make_async_copy(hbm.at[pl.ds(0, chunks), :], buf.at[slot, pl.ds(0, chunks), :],
                              sem.at[slot]).wait()
        return 0
    lax.fori_loop(0, count, body, 0, unroll=DMA_UNROLL)


def _gather_kernel(tok_ref, x_hbm, o_ref, buf, sem, *, rows, chunks):
    i = pl.program_id(0)

    def issue_tile(step, slot):
        def body(m, _):
            for u in range(DMA_UNROLL):
                r = m * DMA_UNROLL + u
                src = pl.ds(pl.multiple_of(tok_ref[step * rows + r] * chunks, chunks), chunks)
                dst = pl.ds(pl.multiple_of(r * chunks, chunks), chunks)
                pltpu.make_async_copy(x_hbm.at[src, :], buf.at[slot, dst, :], sem.at[slot]).start(
                    priority=u % 2)
            return 0
        lax.fori_loop(0, rows // DMA_UNROLL, body, 0)

    @pl.when(i == 0)
    def _():
        issue_tile(0, 0)

    @pl.when(i + 1 < pl.num_programs(0))
    def _():
        issue_tile(i + 1, (i + 1) % 2)

    slot = i % 2
    _wait_slabs(x_hbm, buf, sem, slot, rows, chunks)
    o_ref[...] = _from_slabs(buf.at[slot], 0, rows, chunks).astype(o_ref.dtype)


def _gather_rows(row_tok, x1_slabs, chunks, rows=256):
    n_rows = row_tok.shape[0]
    return pl.pallas_call(
        functools.partial(_gather_kernel, rows=rows, chunks=chunks),
        grid_spec=pltpu.PrefetchScalarGridSpec(
            num_scalar_prefetch=1,
            grid=(n_rows // rows,),
            in_specs=[pl.BlockSpec(memory_space=pl.ANY)],
            out_specs=pl.BlockSpec((rows, chunks * LANES), lambda i, tok: (i, 0)),
            scratch_shapes=[pltpu.VMEM((2, rows * chunks, LANES), F32), pltpu.SemaphoreType.DMA((2,))],
        ),
        out_shape=jax.ShapeDtypeStruct((n_rows, chunks * LANES), BF16),
        compiler_params=_cparams(("arbitrary",)),
        name="gather_rows",
    )(row_tok, x1_slabs)


def _moe_kernel(ie_ref, ij_ref, ist_ref, ins_ref, tail_ref, live_ref, x_hbm, wgu_hbm, wd_hbm, bg_ref, bu_ref,
                bd_ref, y_hbm, x_vmem, acc, wg_f, wu_f, wd_f, wg_bf, wu_bf, wd_bf, stage, sem_in, sem_out,
                sem_w, *, n_f):
    del ij_ref
    i = pl.program_id(0)
    j = pl.program_id(1)
    nsub = ins_ref[i]
    start = ist_ref[i]
    d = acc.shape[1]
    tf = wg_bf.shape[1]
    d_ff = n_f * tf
    chunks = d // LANES
    block_rows = ROW_BLOCK * chunks

    def weight_copies(e, jj, slot):
        col_g = pl.ds(pl.multiple_of(jj * tf, tf), tf)
        col_u = pl.ds(pl.multiple_of(d_ff + jj * tf, tf), tf)
        copies = []
        for half in range(2):
            rows_k = pl.ds(half * (d // 2), d // 2)
            rows_f = pl.ds(pl.multiple_of(jj * tf + half * (tf // 2), tf // 2), tf // 2)
            dst_f = pl.ds(half * (tf // 2), tf // 2)
            copies += [
                pltpu.make_async_copy(wgu_hbm.at[e, rows_k, col_g], wg_f.at[slot, rows_k, :], sem_w.at[slot]),
                pltpu.make_async_copy(wgu_hbm.at[e, rows_k, col_u], wu_f.at[slot, rows_k, :], sem_w.at[slot]),
                pltpu.make_async_copy(wd_hbm.at[e, rows_f, :], wd_f.at[slot, dst_f, :], sem_w.at[slot]),
            ]
        return copies

    def start_weights(e, jj, slot):
        for n, c in enumerate(weight_copies(e, jj, slot)):
            c.start(priority=n % 2)

    step = i * n_f + j
    w_slot = step % 2

    @pl.when(step == 0)
    def _():
        start_weights(ie_ref[0], 0, 0)

    @pl.when(step + 1 < live_ref[0])
    def _():
        last = j == n_f - 1
        start_weights(ie_ref[jnp.where(last, i + 1, i)], jnp.where(last, 0, j + 1), 1 - w_slot)

    def rows_of(r):
        return pl.ds(pl.multiple_of(r * ROW_BLOCK, ROW_BLOCK), ROW_BLOCK)

    def slabs_of(block):
        return pl.ds(pl.multiple_of(block * block_rows, block_rows), block_rows)

    def for_each(lo, hi, fn):
        def body(r, _):
            fn(r)
            return 0
        lax.fori_loop(lo, hi, body, 0)

    @pl.when((i == 0) & (j == 0))
    def _():
        stage[0] = jnp.zeros(stage.shape[1:], F32)

        def zero_copy(bk):
            return pltpu.make_async_copy(stage.at[0], y_hbm.at[slabs_of(bk), :], sem_out.at[0])

        n_blocks = y_hbm.shape[0] // block_rows
        for_each(tail_ref[0], n_blocks, lambda bk: zero_copy(bk).start())
        for_each(tail_ref[0], n_blocks, lambda bk: zero_copy(bk).wait())

    first_block = start // ROW_BLOCK

    def in_copy(r):
        src = pl.ds(pl.multiple_of(start + r * ROW_BLOCK, ROW_BLOCK), ROW_BLOCK)
        return pltpu.make_async_copy(x_hbm.at[src, :], x_vmem.at[rows_of(r), :], sem_in)

    def out_copy(r, slot):
        return pltpu.make_async_copy(stage.at[slot], y_hbm.at[slabs_of(first_block + r), :],
                                     sem_out.at[slot])

    def span(r, n_blocks):
        return pl.ds(pl.multiple_of(r * ROW_BLOCK, ROW_BLOCK), n_blocks * ROW_BLOCK)

    def hidden(rows):
        xb = x_vmem[rows, :]
        gate = jnp.dot(xb, wg_bf[...], preferred_element_type=F32) + bg_ref[...]
        up = jnp.dot(xb, wu_bf[...], preferred_element_type=F32) + bu_ref[...]
        gate = jnp.minimum(gate, SWIGLU_LIMIT)
        up = jnp.clip(up, -SWIGLU_LIMIT, SWIGLU_LIMIT)
        act = (up + 1.0) * (gate * jax.nn.sigmoid(SWIGLU_ALPHA * gate))
        return jnp.dot(act.astype(BF16), wd_bf[...], preferred_element_type=F32)

    @pl.when(nsub > 0)
    def _():
        @pl.when(j == 0)
        def _():
            for_each(0, nsub, lambda r: in_copy(r).start())
            bias_rows = jnp.broadcast_to(bd_ref[...], (ROW_BLOCK, d))

            def init(r):
                acc[rows_of(r), :] = bias_rows
            for_each(0, nsub, init)
            for_each(0, nsub, lambda r: in_copy(r).wait())

        for c in weight_copies(ie_ref[i], j, w_slot):
            c.wait()
        wg_bf[...] = wg_f[w_slot].astype(BF16)
        wu_bf[...] = wu_f[w_slot].astype(BF16)
        wd_bf[...] = wd_f[w_slot].astype(BF16)

        n_pairs = nsub // 2
        odd = nsub % 2 == 1

        @pl.when(j < n_f - 1)
        def _():
            def pair(p):
                acc[span(2 * p, 2), :] += hidden(span(2 * p, 2))
            for_each(0, n_pairs, pair)

            @pl.when(odd)
            def _():
                acc[span(nsub - 1, 1), :] += hidden(span(nsub - 1, 1))

        @pl.when(j == n_f - 1)
        def _():
            def write_out(r, slot, final, reuse):
                @pl.when(reuse)
                def _():
                    out_copy(r, slot).wait()
                _to_slabs(stage.at[slot], 0, ROW_BLOCK, final)
                out_copy(r, slot).start()

            def pair(p):
                final = acc[span(2 * p, 2), :] + hidden(span(2 * p, 2))
                write_out(2 * p, 0, final[:ROW_BLOCK], p > 0)
                write_out(2 * p + 1, 1, final[ROW_BLOCK:], p > 0)
            for_each(0, n_pairs, pair)

            @pl.when(odd)
            def _():
                final = acc[span(nsub - 1, 1), :] + hidden(span(nsub - 1, 1))
                write_out(nsub - 1, 0, final, n_pairs > 0)

            out_copy(0, 0).wait()

            @pl.when(n_pairs > 0)
            def _():
                out_copy(0, 1).wait()


def _moe_ffn(item_e, item_j, item_start, item_nsub, tail_block, live_steps, x_rows, w_gate_up, b_gate_up,
             w_down, b_down, r_max, tf=512):
    n_e, d, two_f = w_gate_up.shape
    chunks = d // LANES
    d_ff = two_f // 2
    n_f = d_ff // tf
    n_items = item_e.shape[0]

    def jf(i, j, ij):
        return jnp.where(ij[i] < 0, j, ij[i])

    hbm = pl.BlockSpec(memory_space=pl.ANY)
    in_specs = [
        hbm, hbm, hbm,
        pl.BlockSpec((None, 1, tf), lambda i, j, ie, ij, ist, ins, tl, lv: (ie[i], 0, jf(i, j, ij))),
        pl.BlockSpec((None, 1, tf), lambda i, j, ie, ij, ist, ins, tl, lv: (ie[i], 0, n_f + jf(i, j, ij))),
        pl.BlockSpec((None, 1, d), lambda i, j, ie, ij, ist, ins, tl, lv: (ie[i], 0, 0)),
    ]
    return pl.pallas_call(
        functools.partial(_moe_kernel, n_f=n_f),
        grid_spec=pltpu.PrefetchScalarGridSpec(
            num_scalar_prefetch=6,
            grid=(n_items, n_f),
            in_specs=in_specs,
            out_specs=pl.BlockSpec(memory_space=pl.ANY),
            scratch_shapes=[pltpu.VMEM((r_max, d), BF16), pltpu.VMEM((r_max, d), F32),
                            pltpu.VMEM((2, d, tf), F32), pltpu.VMEM((2, d, tf), F32),
                            pltpu.VMEM((2, tf, d), F32),
                            pltpu.VMEM((d, tf), BF16), pltpu.VMEM((d, tf), BF16),
                            pltpu.VMEM((tf, d), BF16),
                            pltpu.VMEM((2, ROW_BLOCK * chunks, LANES), F32),
                            pltpu.SemaphoreType.DMA(()), pltpu.SemaphoreType.DMA((2,)),
                            pltpu.SemaphoreType.DMA((2,))],
        ),
        out_shape=jax.ShapeDtypeStruct((x_rows.shape[0] * chunks, LANES), F32),
        compiler_params=_cparams(("arbitrary", "arbitrary")),
        name="moe_ffn",
    )(item_e, item_j, item_start, item_nsub, tail_block, live_steps, x_rows, w_gate_up, w_down,
      b_gate_up.reshape(n_e, 1, two_f), b_gate_up.reshape(n_e, 1, two_f), b_down.reshape(n_e, 1, d))


def _combine_kernel(dest_ref, y_hbm, x1_ref, meta_ref, g_ref, b_ref, o_ref, buf, sem, *, alpha, tm, chunks):
    i = pl.program_id(0)
    per_step = tm * TOP_K

    def issue_tile(step, slot):
        def body(tt, _):
            for kk in range(TOP_K):
                src = pl.ds(pl.multiple_of(dest_ref[step * per_step + tt * TOP_K + kk] * chunks, chunks),
                            chunks)
                dst = pl.ds(pl.multiple_of((kk * tm + tt) * chunks, chunks), chunks)
                pltpu.make_async_copy(y_hbm.at[src, :], buf.at[slot, dst, :], sem.at[slot]).start(
                    priority=kk % 2)
            return 0
        lax.fori_loop(0, tm, body, 0, unroll=DMA_UNROLL // TOP_K)

    @pl.when(i == 0)
    def _():
        issue_tile(0, 0)

    @pl.when(i + 1 < pl.num_programs(0))
    def _():
        issue_tile(i + 1, (i + 1) % 2)

    slot = i % 2
    _wait_slabs(y_hbm, buf, sem, slot, per_step, chunks)
    meta = meta_ref[...]
    y = jnp.zeros(x1_ref.shape, F32)
    for kk in range(TOP_K):
        rows = _from_slabs(buf.at[slot], kk * tm * chunks, tm, chunks)
        y = y + meta[:, TOP_K + kk:TOP_K + kk + 1] * rows
    o_ref[...] = _layer_norm(alpha * x1_ref[...] + y, g_ref[...], b_ref[...])


def _combine(dest_flat, y_rows, x1, meta, ln_g, ln_b, alpha, tm=128):
    t, d = x1.shape
    tm = min(tm, t)
    chunks = d // LANES
    row = lambda w: pl.BlockSpec((tm, w), lambda i, dest: (i, 0))
    const = pl.BlockSpec((1, d), lambda i, dest: (0, 0))
    return pl.pallas_call(
        functools.partial(_combine_kernel, alpha=alpha, tm=tm, chunks=chunks),
        grid_spec=pltpu.PrefetchScalarGridSpec(
            num_scalar_prefetch=1,
            grid=(t // tm,),
            in_specs=[pl.BlockSpec(memory_space=pl.ANY), row(d), row(LANES), const, const],
            out_specs=row(d),
            scratch_shapes=[pltpu.VMEM((2, TOP_K * tm * chunks, LANES), F32),
                            pltpu.SemaphoreType.DMA((2,))],
        ),
        out_shape=jax.ShapeDtypeStruct((t, d), F32),
        compiler_params=_cparams(("arbitrary",)),
        name="combine_ln",
    )(dest_flat, y_rows, x1, meta, ln_g.reshape(1, d).astype(F32), ln_b.reshape(1, d).astype(F32))


def _routing_plan(meta, counts_f, t, r_max):
    idx = meta[:, 0:TOP_K].astype(jnp.int32)
    rank = meta[:, 2 * TOP_K:3 * TOP_K].astype(jnp.int32)
    counts = counts_f[0, :N_EXPERTS].astype(jnp.int32)
    n128 = (counts + ROW_BLOCK - 1) // ROW_BLOCK
    padded = n128 * ROW_BLOCK
    pad_start = jnp.cumsum(padded) - padded
    dest = pad_start[idx] + rank
    tk = t * TOP_K
    n_rows = (tk + ROW_BLOCK - 1) // ROW_BLOCK * ROW_BLOCK + N_EXPERTS * ROW_BLOCK
    flat_tok = jnp.arange(tk, dtype=jnp.int32) // TOP_K
    row_tok = jnp.zeros((n_rows,), jnp.int32).at[dest.reshape(tk)].set(flat_tok)

    subs = r_max // ROW_BLOCK
    n_items = N_EXPERTS + (n_rows // ROW_BLOCK - N_EXPERTS) // subs
    items_e = (n128 + subs - 1) // subs
    items_end = jnp.cumsum(items_e)
    total = items_end[-1]
    slot = jnp.arange(n_items, dtype=jnp.int32)
    live = slot < total
    s_eff = jnp.minimum(slot, total - 1)
    e = jnp.minimum(jnp.searchsorted(items_end, s_eff, side='right'), N_EXPERTS - 1).astype(jnp.int32)
    local = s_eff - (items_end[e] - items_e[e])
    item_start = (pad_start[e] + local * r_max).astype(jnp.int32)
    item_nsub = jnp.where(live, jnp.clip(n128[e] - local * subs, 0, subs), 0).astype(jnp.int32)
    tail_block = jnp.sum(n128).reshape(1).astype(jnp.int32)
    return dest, row_tok, e, live, item_start, item_nsub, tail_block


def kernel(x, w_in, ret_gn_gain, w_out, ln1_gain, ln1_bias, w_router, b_router, w_gate_up, b_gate_up,
           w_down, b_down, ln2_gain, ln2_bias):
    b, s, d = x.shape
    t = b * s
    depth = w_in.shape[0]
    alpha = (2 * depth) ** 0.25
    r_max = 1280
    tf = 512
    n_f = (w_gate_up.shape[-1] // 2) // tf
    xt = x.reshape(t, d)
    for layer in range(depth):
        proj = _in_proj(xt.astype(BF16), w_in[layer].astype(BF16))
        sb = _stickbreak(proj, b, s)
        ret = _retention(proj, ret_gn_gain[layer], b, s)
        x1, x1_slabs, meta, counts = _outproj(sb, ret, xt, w_out[layer].astype(BF16), ln1_gain[layer],
                                              ln1_bias[layer], w_router[layer], b_router[layer], alpha)
        dest, row_tok, item_e, live, item_start, item_nsub, tail_block = _routing_plan(
            meta, counts, t, r_max)
        dest_flat = dest.reshape(t * TOP_K)
        item_j = jnp.where(live, -1, n_f - 1).astype(jnp.int32)
        live_steps = (jnp.sum(live.astype(jnp.int32)) * n_f).reshape(1)
        x_rows = _gather_rows(row_tok, x1_slabs, d // LANES)
        y_rows = _moe_ffn(item_e, item_j, item_start, item_nsub, tail_block, live_steps, x_rows,
                          w_gate_up[layer], b_gate_up[layer], w_down[layer], b_down[layer], r_max, tf)
        xt = _combine(dest_flat, y_rows, x1, meta, ln2_gain[layer], ln2_bias[layer], alpha)
    return xt.reshape(b, s, d)
```

```python
import functools
import math

import jax
import jax.numpy as jnp
from jax import lax
from jax.experimental import pallas as pl
from jax.experimental.pallas import tpu as pltpu

F32 = jnp.float32
BF16 = jnp.bfloat16

HEAD_DIM = 128
SB_HEADS = 8
RET_HEADS = 8
CHUNK = 64
ROPE_BASE = 10000.0
N_EXPERTS = 32
TOP_K = 4
SWIGLU_LIMIT = 7.0
SWIGLU_ALPHA = 1.702
LN_EPS = 1e-5
GN_EPS = 1e-5

V7X_VMEM_LIMIT_BYTES = 56 * 1024 * 1024
LANES = 128
ROW_BLOCK = 128
NEG_BIG = -1e30
EXP_UNDERFLOW = -105.0


def _cparams(sem, vmem=V7X_VMEM_LIMIT_BYTES):
    return pltpu.CompilerParams(dimension_semantics=sem, vmem_limit_bytes=vmem)


def _matmul_kernel(x_ref, w_ref, o_ref):
    o_ref[...] = jnp.dot(x_ref[...], w_ref[...], preferred_element_type=F32).astype(o_ref.dtype)


def _in_proj(x_bf, w_bf, tm=1024, tn=1024):
    t, d = x_bf.shape
    n = w_bf.shape[1]
    tm = min(tm, t)
    return pl.pallas_call(
        _matmul_kernel,
        grid=(n // tn, t // tm),
        in_specs=[pl.BlockSpec((tm, d), lambda j, i: (i, 0)),
                  pl.BlockSpec((d, tn), lambda j, i: (0, j))],
        out_specs=pl.BlockSpec((tm, tn), lambda j, i: (i, j)),
        out_shape=jax.ShapeDtypeStruct((t, n), BF16),
        compiler_params=_cparams(("arbitrary", "arbitrary")),
        name="in_proj",
    )(x_bf, w_bf)


def _sb_kernel(q_ref, k_ref, v_ref, o_ref, *, seq, scale, group, sub):
    blk = 128
    nq = seq // blk
    row = lax.broadcasted_iota(jnp.int32, (blk, blk), 0)
    col = lax.broadcasted_iota(jnp.int32, (blk, blk), 1)
    causal = col < row
    r2 = lax.broadcasted_iota(jnp.int32, (2 * blk, 2 * blk), 0)
    c2 = lax.broadcasted_iota(jnp.int32, (2 * blk, 2 * blk), 1)
    cum_rhs = jnp.where((c2 >= blk) | ((r2 & (blk - 1)) >= c2), 1.0, 0.0).astype(BF16)

    n_sub = group // sub
    causal_sub = jnp.concatenate([causal] * sub, axis=0)
    heads_of = lambda s: range(s * sub, (s + 1) * sub)
    cols = lambda g: slice(g * blk, (g + 1) * blk)

    def block(qbs, kb, carries, accs, masked):
        ks = pl.ds(pl.multiple_of(kb * blk, blk), blk)
        zs = [jnp.concatenate(
            [lax.dot_general(qbs[g], k_ref[ks, cols(g)], (((1,), (1,)), ((), ())),
                             preferred_element_type=F32) for g in heads_of(s)], axis=0) * scale
              for s in range(n_sub)]
        sums = []
        for z in zs:
            neg_z = -z
            lnb = jnp.minimum(neg_z, 0.0) - jnp.log(1.0 + jnp.exp(jnp.minimum(z, neg_z)))
            if masked:
                lnb = jnp.where(causal_sub, lnb, 0.0)
            hi = lnb.astype(BF16)
            lo = (lnb - hi.astype(F32)).astype(BF16)
            sums.append(jnp.dot(jnp.concatenate([hi, lo], axis=1), cum_rhs, preferred_element_type=F32))
        new_carries, new_accs = [], []
        for s in range(n_sub):
            incl = sums[s][:, :blk]
            total = sums[s][:, blk:]
            w = jnp.exp(zs[s] + incl + carries[s])
            if masked:
                w = jnp.where(causal_sub, w, 0.0)
            w = w.astype(BF16)
            for n, g in enumerate(heads_of(s)):
                new_accs.append(accs[g] + jnp.dot(w[n * blk:(n + 1) * blk], v_ref[ks, cols(g)],
                                                  preferred_element_type=F32))
            new_carries.append(carries[s] + total)
        return tuple(new_carries), tuple(new_accs)

    def q_body(qi, _):
        qs = pl.ds(pl.multiple_of(qi * blk, blk), blk)
        qbs = [q_ref[qs, cols(g)] for g in range(group)]
        state = block(qbs, qi, (jnp.zeros((sub * blk, blk), F32),) * n_sub,
                      (jnp.zeros((blk, blk), F32),) * group, True)

        def some_weight_left(carries):
            top = functools.reduce(jnp.maximum, carries)
            return (jnp.max(top) > EXP_UNDERFLOW).astype(jnp.int32)

        def live(st):
            t, more, _, _ = st
            return jnp.logical_and(t < qi, more > 0)

        def kb_body(st):
            t, _, carries, accs = st
            carries, accs = block(qbs, qi - 1 - t, carries, accs, False)
            return t + 1, some_weight_left(carries), carries, accs

        _, _, _, accs = lax.while_loop(live, kb_body, (0, 1, state[0], state[1]))
        for g in range(group):
            o_ref[qs, cols(g)] = accs[g].astype(o_ref.dtype)
        return 0

    lax.fori_loop(0, nq, q_body, 0)


def _stickbreak(proj, batch, seq, group=8, sub=4):
    t = batch * seq
    h = SB_HEADS
    n_groups = h // group
    width = group * HEAD_DIM
    kern = functools.partial(_sb_kernel, seq=seq, scale=1.0 / math.sqrt(HEAD_DIM), group=group,
                             sub=sub)
    spec = lambda off: pl.BlockSpec((seq, width), lambda b, hg: (b, off * n_groups + hg))
    return pl.pallas_call(
        kern,
        grid=(batch, n_groups),
        in_specs=[spec(0), spec(1), spec(2)],
        out_specs=pl.BlockSpec((seq, width), lambda b, hg: (b, hg)),
        out_shape=jax.ShapeDtypeStruct((t, h * HEAD_DIM), BF16),
        compiler_params=_cparams(("arbitrary", "arbitrary")),
        name="stickbreak",
    )(proj, proj, proj)


def _ret_kernel(q_ref, k_ref, v_ref, g_ref, cos_ref, sin_ref, intra_ref, qdec_ref, kdec_ref,
                cdec_ref, gain_ref, o_ref, *, seq, heads):
    n_chunks = seq // CHUNK
    half = HEAD_DIM // 2
    k_scale = HEAD_DIM ** -0.5
    cols = lambda h: slice(h * HEAD_DIM, (h + 1) * HEAD_DIM)
    contract_last = (((1,), (1,)), ((), ()))
    contract_rows = (((0,), (0,)), ((), ()))

    def chunk(n, states):
        rs = pl.ds(pl.multiple_of(n * CHUNK, CHUNK), CHUNK)
        cos = cos_ref[rs, :]
        sin = sin_ref[rs, :]
        qrs, krs, crosses, kvs = [], [], [], []
        for h in range(heads):
            q = q_ref[rs, cols(h)].astype(F32)
            k = k_ref[rs, cols(h)].astype(F32)
            qr = q * cos + pltpu.roll(q, half, 1) * sin
            kr = (k * cos + pltpu.roll(k, half, 1) * sin) * k_scale
            qrs.append(qr.astype(BF16))
            krs.append(kr.astype(BF16))
            crosses.append(jnp.dot((qr * qdec_ref[h]).astype(BF16), states[h].astype(BF16),
                                   preferred_element_type=F32))
            kvs.append(lax.dot_general((kr * kdec_ref[h]).astype(BF16), v_ref[rs, cols(h)], contract_rows,
                                       preferred_element_type=F32))
        scores = [lax.dot_general(qrs[h], krs[h], contract_last, preferred_element_type=F32) * intra_ref[h]
                  for h in range(heads)]
        outs = [crosses[h] + jnp.dot(scores[h].astype(BF16), v_ref[rs, cols(h)], preferred_element_type=F32)
                for h in range(heads)]
        new_states = []
        for h in range(heads):
            o = outs[h]
            mu = jnp.mean(o, axis=-1, keepdims=True)
            var = jnp.mean(jnp.square(o - mu), axis=-1, keepdims=True)
            on = (o - mu) * lax.rsqrt(var + GN_EPS)
            g = g_ref[rs, cols(h)].astype(F32)
            out = on * gain_ref[:, cols(h)] * (g * jax.nn.sigmoid(g))
            o_ref[rs, cols(h)] = out.astype(o_ref.dtype)
            new_states.append(states[h] * cdec_ref[h] + kvs[h])
        return tuple(new_states)

    lax.fori_loop(0, n_chunks, chunk, (jnp.zeros((HEAD_DIM, HEAD_DIM), F32),) * heads)


def _retention_tables(seq):
    d = HEAD_DIM
    inv_freq = ROPE_BASE ** (-jnp.arange(0, d, 2, dtype=F32) / d)
    ang = jnp.arange(seq, dtype=F32)[:, None] * inv_freq[None, :]
    cos, sin = jnp.cos(ang), jnp.sin(ang)
    cos_full = jnp.concatenate([cos, cos], axis=-1)
    sin_signed = jnp.concatenate([-sin, sin], axis=-1)
    log_gamma = jnp.log1p(-jnp.exp2(-5.0 - jnp.arange(RET_HEADS, dtype=F32)))
    i = jnp.arange(CHUNK, dtype=F32)
    intra = jnp.exp(log_gamma[:, None, None] * jnp.abs(i[:, None] - i[None, :]))
    k_decay = jnp.exp(log_gamma[:, None] * (CHUNK - 1 - i))
    q_decay = jnp.exp(log_gamma[:, None] * (i + 1.0))
    c_decay = jnp.exp(log_gamma * CHUNK)
    bc = lambda a: jnp.broadcast_to(a[..., None], a.shape + (d,))
    return cos_full, sin_signed, intra, bc(q_decay), bc(k_decay), bc(c_decay[:, None])


def _retention(proj, gn_gain, batch, seq):
    t = batch * seq
    h = RET_HEADS
    base = 3 * SB_HEADS
    cos_full, sin_signed, intra, qdec, kdec, cdec = _retention_tables(seq)
    width = h * HEAD_DIM
    group0 = base // h
    spec = lambda off: pl.BlockSpec((seq, width), lambda b: (b, group0 + off))
    full = pl.BlockSpec((seq, HEAD_DIM), lambda b: (0, 0))
    table = lambda r, c: pl.BlockSpec((h, r, c), lambda b: (0, 0, 0))
    return pl.pallas_call(
        functools.partial(_ret_kernel, seq=seq, heads=h),
        grid=(batch,),
        in_specs=[spec(0), spec(1), spec(2), spec(3), full, full,
                  table(CHUNK, CHUNK), table(CHUNK, HEAD_DIM), table(CHUNK, HEAD_DIM),
                  table(1, HEAD_DIM),
                  pl.BlockSpec((1, width), lambda b: (0, 0))],
        out_specs=pl.BlockSpec((seq, width), lambda b: (b, 0)),
        out_shape=jax.ShapeDtypeStruct((t, width), BF16),
        compiler_params=_cparams(("arbitrary",)),
        name="retention",
    )(proj, proj, proj, proj, cos_full, sin_signed, intra, qdec, kdec, cdec,
      gn_gain.reshape(1, h * HEAD_DIM).astype(F32))


def _layer_norm(hid, gain, bias):
    mu = jnp.mean(hid, axis=-1, keepdims=True)
    cen = hid - mu
    var = jnp.mean(jnp.square(cen), axis=-1, keepdims=True)
    return cen * lax.rsqrt(var + LN_EPS) * gain + bias


def _to_slabs(slab_ref, base, rows, value):
    chunks = value.shape[1] // LANES
    for c in range(chunks):
        slab_ref[pl.ds(base + c, rows, stride=chunks), :] = value[:, c * LANES:(c + 1) * LANES]


def _from_slabs(slab_ref, base, rows, chunks):
    return jnp.concatenate(
        [slab_ref[pl.ds(base + c, rows, stride=chunks), :] for c in range(chunks)], axis=1)


def _bf16_bits(x):
    u = pltpu.bitcast(x, jnp.uint32)
    return lax.shift_right_logical(u + jnp.uint32(0x7FFF) + (lax.shift_right_logical(u, jnp.uint32(16))
                                                             & jnp.uint32(1)), jnp.uint32(16))


def _to_packed_slabs(slab_ref, rows, value):
    words = value.shape[1] // (2 * LANES)
    for c in range(words):
        lo = _bf16_bits(value[:, (2 * c) * LANES:(2 * c + 1) * LANES])
        hi = _bf16_bits(value[:, (2 * c + 1) * LANES:(2 * c + 2) * LANES])
        slab_ref[pl.ds(c, rows, stride=words), :] = lo | lax.shift_left(hi, jnp.uint32(16))


def _from_packed_slabs(slab_ref, rows, words):
    out = []
    for c in range(words):
        w = slab_ref[pl.ds(c, rows, stride=words), :]
        out.append(pltpu.bitcast(lax.shift_left(w, jnp.uint32(16)), F32).astype(BF16))
        out.append(pltpu.bitcast(w & jnp.uint32(0xFFFF0000), F32).astype(BF16))
    return jnp.concatenate(out, axis=1)


def _outproj_kernel(sb_ref, ret_ref, x_ref, w_ref, g_ref, b_ref, wr_ref, br_ref,
                    x1_ref, x1s_ref, meta_ref, cnt_ref, *, alpha, sb_width):
    tm = x_ref.shape[0]
    mix = jnp.dot(sb_ref[...], w_ref[:sb_width, :], preferred_element_type=F32)
    mix = mix + jnp.dot(ret_ref[...], w_ref[sb_width:, :], preferred_element_type=F32)
    x1 = _layer_norm(alpha * x_ref[...] + mix, g_ref[...], b_ref[...])
    x1_ref[...] = x1
    _to_packed_slabs(x1s_ref, tm, x1)

    logits = jnp.dot(x1.astype(BF16), wr_ref[...], preferred_element_type=F32) + br_ref[...]
    lane = lax.broadcasted_iota(jnp.int32, (tm, LANES), 1).astype(F32)
    vals = logits
    tops, idxs, hots = [], [], []
    for _ in range(TOP_K):
        m = jnp.max(vals, axis=-1, keepdims=True)
        idx = jnp.min(jnp.where(vals == m, lane, float(LANES)), axis=-1, keepdims=True)
        hot = lane == idx
        vals = jnp.where(hot, NEG_BIG * 2.0, vals)
        tops.append(m)
        idxs.append(idx)
        hots.append(hot)
    exps = [jnp.exp(m - tops[0]) for m in tops]
    denom = exps[0] + exps[1] + exps[2] + exps[3]
    gates = [e / denom for e in exps]

    @pl.when(pl.program_id(0) == 0)
    def _():
        cnt_ref[...] = jnp.zeros_like(cnt_ref)

    multi = jnp.zeros((tm, LANES), F32)
    for hot in hots:
        multi = multi + jnp.where(hot, 1.0, 0.0)
    r = lax.broadcasted_iota(jnp.int32, (tm, tm), 0)
    c = lax.broadcasted_iota(jnp.int32, (tm, tm), 1)
    strict_lower = jnp.where(c < r, 1.0, 0.0).astype(BF16)
    before = jnp.dot(strict_lower, multi.astype(BF16), preferred_element_type=F32) + cnt_ref[...]
    cnt_ref[...] = cnt_ref[...] + jnp.sum(multi, axis=0, keepdims=True)

    meta = jnp.zeros((tm, LANES), F32)
    for kk in range(TOP_K):
        rank = jnp.sum(jnp.where(hots[kk], before, 0.0), axis=-1, keepdims=True)
        meta = jnp.where(lane == kk, idxs[kk], meta)
        meta = jnp.where(lane == TOP_K + kk, gates[kk], meta)
        meta = jnp.where(lane == 2 * TOP_K + kk, rank, meta)
    meta_ref[...] = meta


def _outproj(sb, ret, xt, w_out_bf, ln_g, ln_b, w_router, b_router, alpha, tm=256):
    t, d = xt.shape
    tm = min(tm, t)
    sbw = sb.shape[1]
    wr = jnp.zeros((d, LANES), BF16).at[:, :N_EXPERTS].set(w_router.astype(BF16))
    br = jnp.full((1, LANES), NEG_BIG, F32).at[0, :N_EXPERTS].set(b_router.astype(F32))
    row = lambda w: pl.BlockSpec((tm, w), lambda i: (i, 0))
    const = lambda r, c: pl.BlockSpec((r, c), lambda i: (0, 0))
    return pl.pallas_call(
        functools.partial(_outproj_kernel, alpha=alpha, sb_width=sbw),
        grid=(t // tm,),
        in_specs=[row(sbw), row(ret.shape[1]), row(d), const(d, d), const(1, d), const(1, d),
                  const(d, LANES), const(1, LANES)],
        out_specs=[row(d), pl.BlockSpec((tm * (d // (2 * LANES)), LANES), lambda i: (i, 0)), row(LANES),
                   const(1, LANES)],
        out_shape=[jax.ShapeDtypeStruct((t, d), F32),
                   jax.ShapeDtypeStruct((t * (d // (2 * LANES)), LANES), jnp.uint32),
                   jax.ShapeDtypeStruct((t, LANES), F32),
                   jax.ShapeDtypeStruct((1, LANES), F32)],
        compiler_params=_cparams(("arbitrary",)),
        name="outproj_ln_router",
    )(sb, ret, xt, w_out_bf, ln_g.reshape(1, d).astype(F32), ln_b.reshape(1, d).astype(F32), wr, br)


DMA_UNROLL = 8


def _wait_slabs(hbm, buf, sem, slot, count, chunks):
    def body(n, _):
        pltpu.make_async_copy(hbm.at[pl.ds(0, chunks), :], buf.at[slot, pl.ds(0, chunks), :],
                              sem.at[slot]).wait()
        return 0
    lax.fori_loop(0, count, body, 0, unroll=DMA_UNROLL)


def _gather_kernel(tok_ref, x_hbm, o_ref, buf, sem, *, rows, chunks):
    i = pl.program_id(0)

    def issue_tile(step, slot):
        def body(m, _):
            for u in range(DMA_UNROLL):
                r = m * DMA_UNROLL + u
                src = pl.ds(pl.multiple_of(tok_ref[step * rows + r] * chunks, chunks), chunks)
                dst = pl.ds(pl.multiple_of(r * chunks, chunks), chunks)
                pltpu.make_async_copy(x_hbm.at[src, :], buf.at[slot, dst, :], sem.at[slot]).start(
                    priority=u % 2)
            return 0
        lax.fori_loop(0, rows // DMA_UNROLL, body, 0)

    @pl.when(i == 0)
    def _():
        issue_tile(0, 0)

    @pl.when(i + 1 < pl.num_programs(0))
    def _():
        issue_tile(i + 1, (i + 1) % 2)

    slot = i % 2
    _wait_slabs(x_hbm, buf, sem, slot, rows, chunks)
    o_ref[...] = _from_packed_slabs(buf.at[slot], rows, chunks)


def _gather_rows(row_tok, x1_slabs, chunks, rows=256):
    n_rows = row_tok.shape[0]
    return pl.pallas_call(
        functools.partial(_gather_kernel, rows=rows, chunks=chunks),
        grid_spec=pltpu.PrefetchScalarGridSpec(
            num_scalar_prefetch=1,
            grid=(n_rows // rows,),
            in_specs=[pl.BlockSpec(memory_space=pl.ANY)],
            out_specs=pl.BlockSpec((rows, 2 * chunks * LANES), lambda i, tok: (i, 0)),
            scratch_shapes=[pltpu.VMEM((2, rows * chunks, LANES), jnp.uint32),
                            pltpu.SemaphoreType.DMA((2,))],
        ),
        out_shape=jax.ShapeDtypeStruct((n_rows, 2 * chunks * LANES), BF16),
        compiler_params=_cparams(("arbitrary",)),
        name="gather_rows",
    )(row_tok, x1_slabs)


def _moe_kernel(ie_ref, ij_ref, ist_ref, ins_ref, tail_ref, live_ref, x_hbm, wgu_hbm, wd_hbm, bg_ref, bu_ref,
                bd_ref, y_hbm, x_vmem, acc, wg_f, wu_f, wd_f, wg_bf, wu_bf, wd_bf, stage, sem_in, sem_out,
                sem_w, *, n_f):
    del ij_ref
    i = pl.program_id(0)
    j = pl.program_id(1)
    nsub = ins_ref[i]
    start = ist_ref[i]
    d = acc.shape[1]
    tf = wg_bf.shape[1]
    d_ff = n_f * tf
    chunks = d // LANES
    block_rows = ROW_BLOCK * chunks

    def weight_copies(e, jj, slot):
        col_g = pl.ds(pl.multiple_of(jj * tf, tf), tf)
        col_u = pl.ds(pl.multiple_of(d_ff + jj * tf, tf), tf)
        copies = []
        for half in range(2):
            rows_k = pl.ds(half * (d // 2), d // 2)
            rows_f = pl.ds(pl.multiple_of(jj * tf + half * (tf // 2), tf // 2), tf // 2)
            dst_f = pl.ds(half * (tf // 2), tf // 2)
            copies += [
                pltpu.make_async_copy(wgu_hbm.at[e, rows_k, col_g], wg_f.at[slot, rows_k, :], sem_w.at[slot]),
                pltpu.make_async_copy(wgu_hbm.at[e, rows_k, col_u], wu_f.at[slot, rows_k, :], sem_w.at[slot]),
                pltpu.make_async_copy(wd_hbm.at[e, rows_f, :], wd_f.at[slot, dst_f, :], sem_w.at[slot]),
            ]
        return copies

    def start_weights(e, jj, slot):
        for n, c in enumerate(weight_copies(e, jj, slot)):
            c.start(priority=n % 2)

    step = i * n_f + j
    w_slot = step % 2

    @pl.when(step == 0)
    def _():
        start_weights(ie_ref[0], 0, 0)

    @pl.when(step + 1 < live_ref[0])
    def _():
        last = j == n_f - 1
        start_weights(ie_ref[jnp.where(last, i + 1, i)], jnp.where(last, 0, j + 1), 1 - w_slot)

    def rows_of(r):
        return pl.ds(pl.multiple_of(r * ROW_BLOCK, ROW_BLOCK), ROW_BLOCK)

    def slabs_of(block):
        return pl.ds(pl.multiple_of(block * block_rows, block_rows), block_rows)

    def for_each(lo, hi, fn):
        def body(r, _):
            fn(r)
            return 0
        lax.fori_loop(lo, hi, body, 0)

    @pl.when((i == 0) & (j == 0))
    def _():
        stage[0] = jnp.zeros(stage.shape[1:], F32)

        def zero_copy(bk):
            return pltpu.make_async_copy(stage.at[0], y_hbm.at[slabs_of(bk), :], sem_out.at[0])

        n_blocks = y_hbm.shape[0] // block_rows
        for_each(tail_ref[0], n_blocks, lambda bk: zero_copy(bk).start())
        for_each(tail_ref[0], n_blocks, lambda bk: zero_copy(bk).wait())

    first_block = start // ROW_BLOCK

    def in_copy(r):
        src = pl.ds(pl.multiple_of(start + r * ROW_BLOCK, ROW_BLOCK), ROW_BLOCK)
        return pltpu.make_async_copy(x_hbm.at[src, :], x_vmem.at[rows_of(r), :], sem_in)

    def out_copy(r, slot):
        return pltpu.make_async_copy(stage.at[slot], y_hbm.at[slabs_of(first_block + r), :],
                                     sem_out.at[slot])

    def span(r, n_blocks):
        return pl.ds(pl.multiple_of(r * ROW_BLOCK, ROW_BLOCK), n_blocks * ROW_BLOCK)

    def hidden(rows):
        xb = x_vmem[rows, :]
        gate = jnp.dot(xb, wg_bf[...], preferred_element_type=F32) + bg_ref[...]
        up = jnp.dot(xb, wu_bf[...], preferred_element_type=F32) + bu_ref[...]
        gate = jnp.minimum(gate, SWIGLU_LIMIT)
        up = jnp.clip(up, -SWIGLU_LIMIT, SWIGLU_LIMIT)
        act = (up + 1.0) * (gate * jax.nn.sigmoid(SWIGLU_ALPHA * gate))
        return jnp.dot(act.astype(BF16), wd_bf[...], preferred_element_type=F32)

    @pl.when(nsub > 0)
    def _():
        @pl.when(j == 0)
        def _():
            for_each(0, nsub, lambda r: in_copy(r).start())
            bias_rows = jnp.broadcast_to(bd_ref[...], (ROW_BLOCK, d))

            def init(r):
                acc[rows_of(r), :] = bias_rows
            for_each(0, nsub, init)
            for_each(0, nsub, lambda r: in_copy(r).wait())

        for c in weight_copies(ie_ref[i], j, w_slot):
            c.wait()
        wg_bf[...] = wg_f[w_slot].astype(BF16)
        wu_bf[...] = wu_f[w_slot].astype(BF16)
        wd_bf[...] = wd_f[w_slot].astype(BF16)

        n_pairs = nsub // 2
        odd = nsub % 2 == 1

        @pl.when(j < n_f - 1)
        def _():
            def pair(p):
                acc[span(2 * p, 2), :] += hidden(span(2 * p, 2))
            for_each(0, n_pairs, pair)

            @pl.when(odd)
            def _():
                acc[span(nsub - 1, 1), :] += hidden(span(nsub - 1, 1))

        @pl.when(j == n_f - 1)
        def _():
            def write_out(r, slot, final, reuse):
                @pl.when(reuse)
                def _():
                    out_copy(r, slot).wait()
                _to_slabs(stage.at[slot], 0, ROW_BLOCK, final)
                out_copy(r, slot).start()

            def pair(p):
                final = acc[span(2 * p, 2), :] + hidden(span(2 * p, 2))
                write_out(2 * p, 0, final[:ROW_BLOCK], p > 0)
                write_out(2 * p + 1, 1, final[ROW_BLOCK:], p > 0)
            for_each(0, n_pairs, pair)

            @pl.when(odd)
            def _():
                final = acc[span(nsub - 1, 1), :] + hidden(span(nsub - 1, 1))
                write_out(nsub - 1, 0, final, n_pairs > 0)

            out_copy(0, 0).wait()

            @pl.when(n_pairs > 0)
            def _():
                out_copy(0, 1).wait()


def _moe_ffn(item_e, item_j, item_start, item_nsub, tail_block, live_steps, x_rows, w_gate_up, b_gate_up,
             w_down, b_down, r_max, tf=512):
    n_e, d, two_f = w_gate_up.shape
    chunks = d // LANES
    d_ff = two_f // 2
    n_f = d_ff // tf
    n_items = item_e.shape[0]

    def jf(i, j, ij):
        return jnp.where(ij[i] < 0, j, ij[i])

    hbm = pl.BlockSpec(memory_space=pl.ANY)
    in_specs = [
        hbm, hbm, hbm,
        pl.BlockSpec((None, 1, tf), lambda i, j, ie, ij, ist, ins, tl, lv: (ie[i], 0, jf(i, j, ij))),
        pl.BlockSpec((None, 1, tf), lambda i, j, ie, ij, ist, ins, tl, lv: (ie[i], 0, n_f + jf(i, j, ij))),
        pl.BlockSpec((None, 1, d), lambda i, j, ie, ij, ist, ins, tl, lv: (ie[i], 0, 0)),
    ]
    return pl.pallas_call(
        functools.partial(_moe_kernel, n_f=n_f),
        grid_spec=pltpu.PrefetchScalarGridSpec(
            num_scalar_prefetch=6,
            grid=(n_items, n_f),
            in_specs=in_specs,
            out_specs=pl.BlockSpec(memory_space=pl.ANY),
            scratch_shapes=[pltpu.VMEM((r_max, d), BF16), pltpu.VMEM((r_max, d), F32),
                            pltpu.VMEM((2, d, tf), F32), pltpu.VMEM((2, d, tf), F32),
                            pltpu.VMEM((2, tf, d), F32),
                            pltpu.VMEM((d, tf), BF16), pltpu.VMEM((d, tf), BF16),
                            pltpu.VMEM((tf, d), BF16),
                            pltpu.VMEM((2, ROW_BLOCK * chunks, LANES), F32),
                            pltpu.SemaphoreType.DMA(()), pltpu.SemaphoreType.DMA((2,)),
                            pltpu.SemaphoreType.DMA((2,))],
        ),
        out_shape=jax.ShapeDtypeStruct((x_rows.shape[0] * chunks, LANES), F32),
        compiler_params=_cparams(("arbitrary", "arbitrary")),
        name="moe_ffn",
    )(item_e, item_j, item_start, item_nsub, tail_block, live_steps, x_rows, w_gate_up, w_down,
      b_gate_up.reshape(n_e, 1, two_f), b_gate_up.reshape(n_e, 1, two_f), b_down.reshape(n_e, 1, d))


def _combine_kernel(dest_ref, y_hbm, x1_ref, meta_ref, g_ref, b_ref, o_ref, buf, sem, *, alpha, tm, chunks):
    i = pl.program_id(0)
    per_step = tm * TOP_K

    def issue_tile(step, slot):
        def body(tt, _):
            for kk in range(TOP_K):
                src = pl.ds(pl.multiple_of(dest_ref[step * per_step + tt * TOP_K + kk] * chunks, chunks),
                            chunks)
                dst = pl.ds(pl.multiple_of((kk * tm + tt) * chunks, chunks), chunks)
                pltpu.make_async_copy(y_hbm.at[src, :], buf.at[slot, dst, :], sem.at[slot]).start(
                    priority=kk % 2)
            return 0
        lax.fori_loop(0, tm, body, 0, unroll=DMA_UNROLL // TOP_K)

    @pl.when(i == 0)
    def _():
        issue_tile(0, 0)

    @pl.when(i + 1 < pl.num_programs(0))
    def _():
        issue_tile(i + 1, (i + 1) % 2)

    slot = i % 2
    _wait_slabs(y_hbm, buf, sem, slot, per_step, chunks)
    meta = meta_ref[...]
    y = jnp.zeros(x1_ref.shape, F32)
    for kk in range(TOP_K):
        rows = _from_slabs(buf.at[slot], kk * tm * chunks, tm, chunks)
        y = y + meta[:, TOP_K + kk:TOP_K + kk + 1] * rows
    o_ref[...] = _layer_norm(alpha * x1_ref[...] + y, g_ref[...], b_ref[...])


def _combine(dest_flat, y_rows, x1, meta, ln_g, ln_b, alpha, tm=128):
    t, d = x1.shape
    tm = min(tm, t)
    chunks = d // LANES
    row = lambda w: pl.BlockSpec((tm, w), lambda i, dest: (i, 0))
    const = pl.BlockSpec((1, d), lambda i, dest: (0, 0))
    return pl.pallas_call(
        functools.partial(_combine_kernel, alpha=alpha, tm=tm, chunks=chunks),
        grid_spec=pltpu.PrefetchScalarGridSpec(
            num_scalar_prefetch=1,
            grid=(t // tm,),
            in_specs=[pl.BlockSpec(memory_space=pl.ANY), row(d), row(LANES), const, const],
            out_specs=row(d),
            scratch_shapes=[pltpu.VMEM((2, TOP_K * tm * chunks, LANES), F32),
                            pltpu.SemaphoreType.DMA((2,))],
        ),
        out_shape=jax.ShapeDtypeStruct((t, d), F32),
        compiler_params=_cparams(("arbitrary",)),
        name="combine_ln",
    )(dest_flat, y_rows, x1, meta, ln_g.reshape(1, d).astype(F32), ln_b.reshape(1, d).astype(F32))


def _routing_plan(meta, counts_f, t, r_max):
    idx = meta[:, 0:TOP_K].astype(jnp.int32)
    rank = meta[:, 2 * TOP_K:3 * TOP_K].astype(jnp.int32)
    counts = counts_f[0, :N_EXPERTS].astype(jnp.int32)
    n128 = (counts + ROW_BLOCK - 1) // ROW_BLOCK
    padded = n128 * ROW_BLOCK
    pad_start = jnp.cumsum(padded) - padded
    dest = pad_start[idx] + rank
    tk = t * TOP_K
    n_rows = (tk + ROW_BLOCK - 1) // ROW_BLOCK * ROW_BLOCK + N_EXPERTS * ROW_BLOCK
    flat_tok = jnp.arange(tk, dtype=jnp.int32) // TOP_K
    row_tok = jnp.zeros((n_rows,), jnp.int32).at[dest.reshape(tk)].set(flat_tok)

    subs = r_max // ROW_BLOCK
    n_items = N_EXPERTS + (n_rows // ROW_BLOCK - N_EXPERTS) // subs
    items_e = (n128 + subs - 1) // subs
    items_end = jnp.cumsum(items_e)
    total = items_end[-1]
    slot = jnp.arange(n_items, dtype=jnp.int32)
    live = slot < total
    s_eff = jnp.minimum(slot, total - 1)
    e = jnp.minimum(jnp.searchsorted(items_end, s_eff, side='right'), N_EXPERTS - 1).astype(jnp.int32)
    local = s_eff - (items_end[e] - items_e[e])
    item_start = (pad_start[e] + local * r_max).astype(jnp.int32)
    item_nsub = jnp.where(live, jnp.clip(n128[e] - local * subs, 0, subs), 0).astype(jnp.int32)
    tail_block = jnp.sum(n128).reshape(1).astype(jnp.int32)
    return dest, row_tok, e, live, item_start, item_nsub, tail_block


def kernel(x, w_in, ret_gn_gain, w_out, ln1_gain, ln1_bias, w_router, b_router, w_gate_up, b_gate_up,
           w_down, b_down, ln2_gain, ln2_bias):
    b, s, d = x.shape
    t = b * s
    depth = w_in.shape[0]
    alpha = (2 * depth) ** 0.25
    r_max = 1280
    tf = 512
    n_f = (w_gate_up.shape[-1] // 2) // tf
    xt = x.reshape(t, d)
    for layer in range(depth):
        proj = _in_proj(xt.astype(BF16), w_in[layer].astype(BF16))
        sb = _stickbreak(proj, b, s)
        ret = _retention(proj, ret_gn_gain[layer], b, s)
        x1, x1_slabs, meta, counts = _outproj(sb, ret, xt, w_out[layer].astype(BF16), ln1_gain[layer],
                                              ln1_bias[layer], w_router[layer], b_router[layer], alpha)
        dest, row_tok, item_e, live, item_start, item_nsub, tail_block = _routing_plan(
            meta, counts, t, r_max)
        dest_flat = dest.reshape(t * TOP_K)
        item_j = jnp.where(live, -1, n_f - 1).astype(jnp.int32)
        live_steps = (jnp.sum(live.astype(jnp.int32)) * n_f).reshape(1)
        x_rows = _gather_rows(row_tok, x1_slabs, d // (2 * LANES))
        y_rows = _moe_ffn(item_e, item_j, item_start, item_nsub, tail_block, live_steps, x_rows,
                          w_gate_up[layer], b_gate_up[layer], w_down[layer], b_down[layer], r_max, tf)
        xt = _combine(dest_flat, y_rows, x1, meta, ln2_gain[layer], ln2_bias[layer], alpha)
    return xt.reshape(b, s, d)
```

```python
import functools
import math

import jax
import jax.numpy as jnp
from jax import lax
from jax.experimental import pallas as pl
from jax.experimental.pallas import tpu as pltpu

F32 = jnp.float32
BF16 = jnp.bfloat16

HEAD_DIM = 128
SB_HEADS = 8
RET_HEADS = 8
CHUNK = 64
ROPE_BASE = 10000.0
N_EXPERTS = 32
TOP_K = 4
SWIGLU_LIMIT = 7.0
SWIGLU_ALPHA = 1.702
LN_EPS = 1e-5
GN_EPS = 1e-5

V7X_VMEM_LIMIT_BYTES = 56 * 1024 * 1024
LANES = 128
ROW_BLOCK = 128
NEG_BIG = -1e30
EXP_UNDERFLOW = -105.0


def _cparams(sem, vmem=V7X_VMEM_LIMIT_BYTES):
    return pltpu.CompilerParams(dimension_semantics=sem, vmem_limit_bytes=vmem)


def _matmul_kernel(x_ref, w_ref, o_ref):
    o_ref[...] = jnp.dot(x_ref[...], w_ref[...], preferred_element_type=F32).astype(o_ref.dtype)


def _in_proj(x_bf, w_bf, tm=1024, tn=1024):
    t, d = x_bf.shape
    n = w_bf.shape[1]
    tm = min(tm, t)
    return pl.pallas_call(
        _matmul_kernel,
        grid=(n // tn, t // tm),
        in_specs=[pl.BlockSpec((tm, d), lambda j, i: (i, 0)),
                  pl.BlockSpec((d, tn), lambda j, i: (0, j))],
        out_specs=pl.BlockSpec((tm, tn), lambda j, i: (i, j)),
        out_shape=jax.ShapeDtypeStruct((t, n), BF16),
        compiler_params=_cparams(("arbitrary", "arbitrary")),
        name="in_proj",
    )(x_bf, w_bf)


def _sb_kernel(q_ref, k_ref, v_ref, o_ref, *, seq, scale, group, sub):
    blk = 128
    nq = seq // blk
    row = lax.broadcasted_iota(jnp.int32, (blk, blk), 0)
    col = lax.broadcasted_iota(jnp.int32, (blk, blk), 1)
    causal = col < row
    r2 = lax.broadcasted_iota(jnp.int32, (2 * blk, 2 * blk), 0)
    c2 = lax.broadcasted_iota(jnp.int32, (2 * blk, 2 * blk), 1)
    cum_rhs = jnp.where((c2 >= blk) | ((r2 & (blk - 1)) >= c2), 1.0, 0.0).astype(BF16)

    n_sub = group // sub
    causal_sub = jnp.concatenate([causal] * sub, axis=0)
    heads_of = lambda s: range(s * sub, (s + 1) * sub)
    cols = lambda g: slice(g * blk, (g + 1) * blk)

    def block(qbs, kb, carries, accs, masked):
        ks = pl.ds(pl.multiple_of(kb * blk, blk), blk)
        zs = [jnp.concatenate(
            [lax.dot_general(qbs[g], k_ref[ks, cols(g)], (((1,), (1,)), ((), ())),
                             preferred_element_type=F32) for g in heads_of(s)], axis=0) * scale
              for s in range(n_sub)]
        sums = []
        for z in zs:
            neg_z = -z
            lnb = jnp.minimum(neg_z, 0.0) - jnp.log(1.0 + jnp.exp(jnp.minimum(z, neg_z)))
            if masked:
                lnb = jnp.where(causal_sub, lnb, 0.0)
            hi = lnb.astype(BF16)
            lo = (lnb - hi.astype(F32)).astype(BF16)
            sums.append(jnp.dot(jnp.concatenate([hi, lo], axis=1), cum_rhs, preferred_element_type=F32))
        new_carries, new_accs = [], []
        for s in range(n_sub):
            incl = sums[s][:, :blk]
            total = sums[s][:, blk:]
            w = jnp.exp(zs[s] + incl + carries[s])
            if masked:
                w = jnp.where(causal_sub, w, 0.0)
            w = w.astype(BF16)
            for n, g in enumerate(heads_of(s)):
                new_accs.append(accs[g] + jnp.dot(w[n * blk:(n + 1) * blk], v_ref[ks, cols(g)],
                                                  preferred_element_type=F32))
            new_carries.append(carries[s] + total)
        return tuple(new_carries), tuple(new_accs)

    def q_body(qi, _):
        qs = pl.ds(pl.multiple_of(qi * blk, blk), blk)
        qbs = [q_ref[qs, cols(g)] for g in range(group)]
        state = block(qbs, qi, (jnp.zeros((sub * blk, blk), F32),) * n_sub,
                      (jnp.zeros((blk, blk), F32),) * group, True)

        def some_weight_left(carries):
            top = functools.reduce(jnp.maximum, carries)
            return (jnp.max(top) > EXP_UNDERFLOW).astype(jnp.int32)

        def live(st):
            t, more, _, _ = st
            return jnp.logical_and(t < qi, more > 0)

        def kb_body(st):
            t, _, carries, accs = st
            carries, accs = block(qbs, qi - 1 - t, carries, accs, False)
            return t + 1, some_weight_left(carries), carries, accs

        _, _, _, accs = lax.while_loop(live, kb_body, (0, 1, state[0], state[1]))
        for g in range(group):
            o_ref[qs, cols(g)] = accs[g].astype(o_ref.dtype)
        return 0

    lax.fori_loop(0, nq, q_body, 0)


def _stickbreak(proj, batch, seq, group=8, sub=4):
    t = batch * seq
    h = SB_HEADS
    n_groups = h // group
    width = group * HEAD_DIM
    kern = functools.partial(_sb_kernel, seq=seq, scale=1.0 / math.sqrt(HEAD_DIM), group=group,
                             sub=sub)
    spec = lambda off: pl.BlockSpec((seq, width), lambda b, hg: (b, off * n_groups + hg))
    return pl.pallas_call(
        kern,
        grid=(batch, n_groups),
        in_specs=[spec(0), spec(1), spec(2)],
        out_specs=pl.BlockSpec((seq, width), lambda b, hg: (b, hg)),
        out_shape=jax.ShapeDtypeStruct((t, h * HEAD_DIM), BF16),
        compiler_params=_cparams(("arbitrary", "arbitrary")),
        name="stickbreak",
    )(proj, proj, proj)


def _ret_kernel(q_ref, k_ref, v_ref, g_ref, cos_ref, sin_ref, intra_ref, qdec_ref, kdec_ref,
                cdec_ref, gain_ref, o_ref, *, seq, heads):
    n_chunks = seq // CHUNK
    half = HEAD_DIM // 2
    k_scale = HEAD_DIM ** -0.5
    cols = lambda h: slice(h * HEAD_DIM, (h + 1) * HEAD_DIM)
    contract_last = (((1,), (1,)), ((), ()))
    contract_rows = (((0,), (0,)), ((), ()))

    def chunk(n, states):
        rs = pl.ds(pl.multiple_of(n * CHUNK, CHUNK), CHUNK)
        cos = cos_ref[rs, :]
        sin = sin_ref[rs, :]
        qrs, krs, crosses, kvs = [], [], [], []
        for h in range(heads):
            q = q_ref[rs, cols(h)].astype(F32)
            k = k_ref[rs, cols(h)].astype(F32)
            qr = q * cos + pltpu.roll(q, half, 1) * sin
            kr = (k * cos + pltpu.roll(k, half, 1) * sin) * k_scale
            qrs.append(qr.astype(BF16))
            krs.append(kr.astype(BF16))
            crosses.append(jnp.dot((qr * qdec_ref[h]).astype(BF16), states[h].astype(BF16),
                                   preferred_element_type=F32))
            kvs.append(lax.dot_general((kr * kdec_ref[h]).astype(BF16), v_ref[rs, cols(h)], contract_rows,
                                       preferred_element_type=F32))
        scores = [lax.dot_general(qrs[h], krs[h], contract_last, preferred_element_type=F32) * intra_ref[h]
                  for h in range(heads)]
        outs = [crosses[h] + jnp.dot(scores[h].astype(BF16), v_ref[rs, cols(h)], preferred_element_type=F32)
                for h in range(heads)]
        new_states = []
        for h in range(heads):
            o = outs[h]
            mu = jnp.mean(o, axis=-1, keepdims=True)
            var = jnp.mean(jnp.square(o - mu), axis=-1, keepdims=True)
            on = (o - mu) * lax.rsqrt(var + GN_EPS)
            g = g_ref[rs, cols(h)].astype(F32)
            out = on * gain_ref[:, cols(h)] * (g * jax.nn.sigmoid(g))
            o_ref[rs, cols(h)] = out.astype(o_ref.dtype)
            new_states.append(states[h] * cdec_ref[h] + kvs[h])
        return tuple(new_states)

    lax.fori_loop(0, n_chunks, chunk, (jnp.zeros((HEAD_DIM, HEAD_DIM), F32),) * heads)


def _retention_tables(seq):
    d = HEAD_DIM
    inv_freq = ROPE_BASE ** (-jnp.arange(0, d, 2, dtype=F32) / d)
    ang = jnp.arange(seq, dtype=F32)[:, None] * inv_freq[None, :]
    cos, sin = jnp.cos(ang), jnp.sin(ang)
    cos_full = jnp.concatenate([cos, cos], axis=-1)
    sin_signed = jnp.concatenate([-sin, sin], axis=-1)
    log_gamma = jnp.log1p(-jnp.exp2(-5.0 - jnp.arange(RET_HEADS, dtype=F32)))
    i = jnp.arange(CHUNK, dtype=F32)
    intra = jnp.exp(log_gamma[:, None, None] * jnp.abs(i[:, None] - i[None, :]))
    k_decay = jnp.exp(log_gamma[:, None] * (CHUNK - 1 - i))
    q_decay = jnp.exp(log_gamma[:, None] * (i + 1.0))
    c_decay = jnp.exp(log_gamma * CHUNK)
    bc = lambda a: jnp.broadcast_to(a[..., None], a.shape + (d,))
    return cos_full, sin_signed, intra, bc(q_decay), bc(k_decay), bc(c_decay[:, None])


def _retention(proj, gn_gain, batch, seq):
    t = batch * seq
    h = RET_HEADS
    base = 3 * SB_HEADS
    cos_full, sin_signed, intra, qdec, kdec, cdec = _retention_tables(seq)
    width = h * HEAD_DIM
    group0 = base // h
    spec = lambda off: pl.BlockSpec((seq, width), lambda b: (b, group0 + off))
    full = pl.BlockSpec((seq, HEAD_DIM), lambda b: (0, 0))
    table = lambda r, c: pl.BlockSpec((h, r, c), lambda b: (0, 0, 0))
    return pl.pallas_call(
        functools.partial(_ret_kernel, seq=seq, heads=h),
        grid=(batch,),
        in_specs=[spec(0), spec(1), spec(2), spec(3), full, full,
                  table(CHUNK, CHUNK), table(CHUNK, HEAD_DIM), table(CHUNK, HEAD_DIM),
                  table(1, HEAD_DIM),
                  pl.BlockSpec((1, width), lambda b: (0, 0))],
        out_specs=pl.BlockSpec((seq, width), lambda b: (b, 0)),
        out_shape=jax.ShapeDtypeStruct((t, width), BF16),
        compiler_params=_cparams(("arbitrary",)),
        name="retention",
    )(proj, proj, proj, proj, cos_full, sin_signed, intra, qdec, kdec, cdec,
      gn_gain.reshape(1, h * HEAD_DIM).astype(F32))


def _layer_norm(hid, gain, bias):
    mu = jnp.mean(hid, axis=-1, keepdims=True)
    cen = hid - mu
    var = jnp.mean(jnp.square(cen), axis=-1, keepdims=True)
    return cen * lax.rsqrt(var + LN_EPS) * gain + bias


def _to_slabs(slab_ref, base, rows, value):
    chunks = value.shape[1] // LANES
    for c in range(chunks):
        slab_ref[pl.ds(base + c, rows, stride=chunks), :] = value[:, c * LANES:(c + 1) * LANES]


def _from_slabs(slab_ref, base, rows, chunks):
    return jnp.concatenate(
        [slab_ref[pl.ds(base + c, rows, stride=chunks), :] for c in range(chunks)], axis=1)


def _bf16_bits(x):
    u = pltpu.bitcast(x, jnp.uint32)
    return lax.shift_right_logical(u + jnp.uint32(0x7FFF) + (lax.shift_right_logical(u, jnp.uint32(16))
                                                             & jnp.uint32(1)), jnp.uint32(16))


def _to_packed_slabs(slab_ref, rows, value):
    words = value.shape[1] // (2 * LANES)
    for c in range(words):
        lo = _bf16_bits(value[:, (2 * c) * LANES:(2 * c + 1) * LANES])
        hi = _bf16_bits(value[:, (2 * c + 1) * LANES:(2 * c + 2) * LANES])
        slab_ref[pl.ds(c, rows, stride=words), :] = lo | lax.shift_left(hi, jnp.uint32(16))


def _from_packed_slabs(slab_ref, base, rows, words):
    out = []
    for c in range(words):
        w = slab_ref[pl.ds(base + c, rows, stride=words), :]
        out.append(pltpu.bitcast(lax.shift_left(w, jnp.uint32(16)), F32).astype(BF16))
        out.append(pltpu.bitcast(w & jnp.uint32(0xFFFF0000), F32).astype(BF16))
    return jnp.concatenate(out, axis=1)


def _outproj_kernel(sb_ref, ret_ref, x_ref, w_ref, g_ref, b_ref, wr_ref, br_ref,
                    x1_ref, x1s_ref, meta_ref, cnt_ref, *, alpha, sb_width):
    tm = x_ref.shape[0]
    mix = jnp.dot(sb_ref[...], w_ref[:sb_width, :], preferred_element_type=F32)
    mix = mix + jnp.dot(ret_ref[...], w_ref[sb_width:, :], preferred_element_type=F32)
    x1 = _layer_norm(alpha * x_ref[...] + mix, g_ref[...], b_ref[...])
    x1_ref[...] = x1
    _to_packed_slabs(x1s_ref, tm, x1)

    logits = jnp.dot(x1.astype(BF16), wr_ref[...], preferred_element_type=F32) + br_ref[...]
    lane = lax.broadcasted_iota(jnp.int32, (tm, LANES), 1).astype(F32)
    vals = logits
    tops, idxs, hots = [], [], []
    for _ in range(TOP_K):
        m = jnp.max(vals, axis=-1, keepdims=True)
        idx = jnp.min(jnp.where(vals == m, lane, float(LANES)), axis=-1, keepdims=True)
        hot = lane == idx
        vals = jnp.where(hot, NEG_BIG * 2.0, vals)
        tops.append(m)
        idxs.append(idx)
        hots.append(hot)
    exps = [jnp.exp(m - tops[0]) for m in tops]
    denom = exps[0] + exps[1] + exps[2] + exps[3]
    gates = [e / denom for e in exps]

    @pl.when(pl.program_id(0) == 0)
    def _():
        cnt_ref[...] = jnp.zeros_like(cnt_ref)

    multi = jnp.zeros((tm, LANES), F32)
    for hot in hots:
        multi = multi + jnp.where(hot, 1.0, 0.0)
    r = lax.broadcasted_iota(jnp.int32, (tm, tm), 0)
    c = lax.broadcasted_iota(jnp.int32, (tm, tm), 1)
    strict_lower = jnp.where(c < r, 1.0, 0.0).astype(BF16)
    before = jnp.dot(strict_lower, multi.astype(BF16), preferred_element_type=F32) + cnt_ref[...]
    cnt_ref[...] = cnt_ref[...] + jnp.sum(multi, axis=0, keepdims=True)

    meta = jnp.zeros((tm, LANES), F32)
    for kk in range(TOP_K):
        rank = jnp.sum(jnp.where(hots[kk], before, 0.0), axis=-1, keepdims=True)
        meta = jnp.where(lane == kk, idxs[kk], meta)
        meta = jnp.where(lane == TOP_K + kk, gates[kk], meta)
        meta = jnp.where(lane == 2 * TOP_K + kk, rank, meta)
    meta_ref[...] = meta


def _outproj(sb, ret, xt, w_out_bf, ln_g, ln_b, w_router, b_router, alpha, tm=256):
    t, d = xt.shape
    tm = min(tm, t)
    sbw = sb.shape[1]
    wr = jnp.zeros((d, LANES), BF16).at[:, :N_EXPERTS].set(w_router.astype(BF16))
    br = jnp.full((1, LANES), NEG_BIG, F32).at[0, :N_EXPERTS].set(b_router.astype(F32))
    row = lambda w: pl.BlockSpec((tm, w), lambda i: (i, 0))
    const = lambda r, c: pl.BlockSpec((r, c), lambda i: (0, 0))
    return pl.pallas_call(
        functools.partial(_outproj_kernel, alpha=alpha, sb_width=sbw),
        grid=(t // tm,),
        in_specs=[row(sbw), row(ret.shape[1]), row(d), const(d, d), const(1, d), const(1, d),
                  const(d, LANES), const(1, LANES)],
        out_specs=[row(d), pl.BlockSpec((tm * (d // (2 * LANES)), LANES), lambda i: (i, 0)), row(LANES),
                   const(1, LANES)],
        out_shape=[jax.ShapeDtypeStruct((t, d), F32),
                   jax.ShapeDtypeStruct((t * (d // (2 * LANES)), LANES), jnp.uint32),
                   jax.ShapeDtypeStruct((t, LANES), F32),
                   jax.ShapeDtypeStruct((1, LANES), F32)],
        compiler_params=_cparams(("arbitrary",)),
        name="outproj_ln_router",
    )(sb, ret, xt, w_out_bf, ln_g.reshape(1, d).astype(F32), ln_b.reshape(1, d).astype(F32), wr, br)


DMA_UNROLL = 8


def _wait_slabs(hbm, buf, sem, slot, count, chunks):
    def body(n, _):
        pltpu.make_async_copy(hbm.at[pl.ds(0, chunks), :], buf.at[slot, pl.ds(0, chunks), :],
                              sem.at[slot]).wait()
        return 0
    lax.fori_loop(0, count, body, 0, unroll=DMA_UNROLL)


def _moe_kernel(ie_ref, ij_ref, ist_ref, ins_ref, tail_ref, live_ref, tok_ref, x_hbm, wgu_hbm, wd_hbm, bg_ref,
                bu_ref, bd_ref, y_hbm, x_vmem, acc, wg_f, wu_f, wd_f, wg_bf, wu_bf, wd_bf, stage, gbuf, sem_g,
                sem_out, sem_w, *, n_f):
    del ij_ref
    i = pl.program_id(0)
    j = pl.program_id(1)
    nsub = ins_ref[i]
    start = ist_ref[i]
    d = acc.shape[1]
    tf = wg_bf.shape[1]
    d_ff = n_f * tf
    chunks = d // LANES
    block_rows = ROW_BLOCK * chunks

    def weight_copies(e, jj, slot):
        col_g = pl.ds(pl.multiple_of(jj * tf, tf), tf)
        col_u = pl.ds(pl.multiple_of(d_ff + jj * tf, tf), tf)
        copies = []
        for half in range(2):
            rows_k = pl.ds(half * (d // 2), d // 2)
            rows_f = pl.ds(pl.multiple_of(jj * tf + half * (tf // 2), tf // 2), tf // 2)
            dst_f = pl.ds(half * (tf // 2), tf // 2)
            copies += [
                pltpu.make_async_copy(wgu_hbm.at[e, rows_k, col_g], wg_f.at[slot, rows_k, :], sem_w.at[slot]),
                pltpu.make_async_copy(wgu_hbm.at[e, rows_k, col_u], wu_f.at[slot, rows_k, :], sem_w.at[slot]),
                pltpu.make_async_copy(wd_hbm.at[e, rows_f, :], wd_f.at[slot, dst_f, :], sem_w.at[slot]),
            ]
        return copies

    def start_weights(e, jj, slot):
        for n, c in enumerate(weight_copies(e, jj, slot)):
            c.start(priority=n % 2)

    step = i * n_f + j
    w_slot = step % 2

    @pl.when(step == 0)
    def _():
        start_weights(ie_ref[0], 0, 0)

    @pl.when(step + 1 < live_ref[0])
    def _():
        last = j == n_f - 1
        start_weights(ie_ref[jnp.where(last, i + 1, i)], jnp.where(last, 0, j + 1), 1 - w_slot)

    def rows_of(r):
        return pl.ds(pl.multiple_of(r * ROW_BLOCK, ROW_BLOCK), ROW_BLOCK)

    def slabs_of(block):
        return pl.ds(pl.multiple_of(block * block_rows, block_rows), block_rows)

    def for_each(lo, hi, fn):
        def body(r, _):
            fn(r)
            return 0
        lax.fori_loop(lo, hi, body, 0)

    @pl.when((i == 0) & (j == 0))
    def _():
        stage[0] = jnp.zeros(stage.shape[1:], F32)

        def zero_copy(bk):
            return pltpu.make_async_copy(stage.at[0], y_hbm.at[slabs_of(bk), :], sem_out.at[0])

        n_blocks = y_hbm.shape[0] // block_rows
        for_each(tail_ref[0], n_blocks, lambda bk: zero_copy(bk).start())
        for_each(tail_ref[0], n_blocks, lambda bk: zero_copy(bk).wait())

    first_block = start // ROW_BLOCK
    words = d // (2 * LANES)

    def start_gather(item):
        row0 = ist_ref[item]

        def body(m, _):
            for u in range(DMA_UNROLL):
                r = m * DMA_UNROLL + u
                src = pl.ds(pl.multiple_of(tok_ref[row0 + r] * words, words), words)
                dst = pl.ds(pl.multiple_of(r * words, words), words)
                pltpu.make_async_copy(x_hbm.at[src, :], gbuf.at[dst, :], sem_g).start(priority=u % 2)
            return 0
        lax.fori_loop(0, ins_ref[item] * (ROW_BLOCK // DMA_UNROLL), body, 0)

    def wait_gather(item):
        def body(m, _):
            for _u in range(DMA_UNROLL):
                pltpu.make_async_copy(x_hbm.at[pl.ds(0, words), :], gbuf.at[pl.ds(0, words), :], sem_g).wait()
            return 0
        lax.fori_loop(0, ins_ref[item] * (ROW_BLOCK // DMA_UNROLL), body, 0)

    def out_copy(r, slot):
        return pltpu.make_async_copy(stage.at[slot], y_hbm.at[slabs_of(first_block + r), :],
                                     sem_out.at[slot])

    def span(r, n_blocks):
        return pl.ds(pl.multiple_of(r * ROW_BLOCK, ROW_BLOCK), n_blocks * ROW_BLOCK)

    def hidden(rows):
        xb = x_vmem[rows, :]
        gate = jnp.dot(xb, wg_bf[...], preferred_element_type=F32) + bg_ref[...]
        up = jnp.dot(xb, wu_bf[...], preferred_element_type=F32) + bu_ref[...]
        gate = jnp.minimum(gate, SWIGLU_LIMIT)
        up = jnp.clip(up, -SWIGLU_LIMIT, SWIGLU_LIMIT)
        act = (up + 1.0) * (gate * jax.nn.sigmoid(SWIGLU_ALPHA * gate))
        return jnp.dot(act.astype(BF16), wd_bf[...], preferred_element_type=F32)

    @pl.when(nsub > 0)
    def _():
        @pl.when(j == 0)
        def _():
            @pl.when(i == 0)
            def _():
                start_gather(0)
            wait_gather(i)
            bias_rows = jnp.broadcast_to(bd_ref[...], (ROW_BLOCK, d))

            def unpack(r):
                x_vmem[rows_of(r), :] = _from_packed_slabs(
                    gbuf, pl.multiple_of(r * (ROW_BLOCK * words), ROW_BLOCK * words), ROW_BLOCK, words)
                acc[rows_of(r), :] = bias_rows
            for_each(0, nsub, unpack)

            @pl.when(i + 1 < pl.num_programs(0))
            def _():
                start_gather(i + 1)

        for c in weight_copies(ie_ref[i], j, w_slot):
            c.wait()
        wg_bf[...] = wg_f[w_slot].astype(BF16)
        wu_bf[...] = wu_f[w_slot].astype(BF16)
        wd_bf[...] = wd_f[w_slot].astype(BF16)

        n_pairs = nsub // 2
        odd = nsub % 2 == 1

        @pl.when(j < n_f - 1)
        def _():
            def pair(p):
                acc[span(2 * p, 2), :] += hidden(span(2 * p, 2))
            for_each(0, n_pairs, pair)

            @pl.when(odd)
            def _():
                acc[span(nsub - 1, 1), :] += hidden(span(nsub - 1, 1))

        @pl.when(j == n_f - 1)
        def _():
            def write_out(r, slot, final, reuse):
                @pl.when(reuse)
                def _():
                    out_copy(r, slot).wait()
                _to_slabs(stage.at[slot], 0, ROW_BLOCK, final)
                out_copy(r, slot).start()

            def pair(p):
                final = acc[span(2 * p, 2), :] + hidden(span(2 * p, 2))
                write_out(2 * p, 0, final[:ROW_BLOCK], p > 0)
                write_out(2 * p + 1, 1, final[ROW_BLOCK:], p > 0)
            for_each(0, n_pairs, pair)

            @pl.when(odd)
            def _():
                final = acc[span(nsub - 1, 1), :] + hidden(span(nsub - 1, 1))
                write_out(nsub - 1, 0, final, n_pairs > 0)

            out_copy(0, 0).wait()

            @pl.when(n_pairs > 0)
            def _():
                out_copy(0, 1).wait()


def _moe_ffn(item_e, item_j, item_start, item_nsub, tail_block, live_steps, row_tok, x1_packed, w_gate_up,
             b_gate_up, w_down, b_down, r_max, tf=512):
    n_e, d, two_f = w_gate_up.shape
    chunks = d // LANES
    words = d // (2 * LANES)
    n_rows = row_tok.shape[0]
    d_ff = two_f // 2
    n_f = d_ff // tf
    n_items = item_e.shape[0]

    def jf(i, j, ij):
        return jnp.where(ij[i] < 0, j, ij[i])

    hbm = pl.BlockSpec(memory_space=pl.ANY)
    in_specs = [
        hbm, hbm, hbm,
        pl.BlockSpec((None, 1, tf), lambda i, j, ie, ij, ist, ins, tl, lv, tk: (ie[i], 0, jf(i, j, ij))),
        pl.BlockSpec((None, 1, tf), lambda i, j, ie, ij, ist, ins, tl, lv, tk: (ie[i], 0, n_f + jf(i, j, ij))),
        pl.BlockSpec((None, 1, d), lambda i, j, ie, ij, ist, ins, tl, lv, tk: (ie[i], 0, 0)),
    ]
    return pl.pallas_call(
        functools.partial(_moe_kernel, n_f=n_f),
        grid_spec=pltpu.PrefetchScalarGridSpec(
            num_scalar_prefetch=7,
            grid=(n_items, n_f),
            in_specs=in_specs,
            out_specs=pl.BlockSpec(memory_space=pl.ANY),
            scratch_shapes=[pltpu.VMEM((r_max, d), BF16), pltpu.VMEM((r_max, d), F32),
                            pltpu.VMEM((2, d, tf), F32), pltpu.VMEM((2, d, tf), F32),
                            pltpu.VMEM((2, tf, d), F32),
                            pltpu.VMEM((d, tf), BF16), pltpu.VMEM((d, tf), BF16),
                            pltpu.VMEM((tf, d), BF16),
                            pltpu.VMEM((2, ROW_BLOCK * chunks, LANES), F32),
                            pltpu.VMEM((r_max * words, LANES), jnp.uint32),
                            pltpu.SemaphoreType.DMA(()), pltpu.SemaphoreType.DMA((2,)),
                            pltpu.SemaphoreType.DMA((2,))],
        ),
        out_shape=jax.ShapeDtypeStruct((n_rows * chunks, LANES), F32),
        compiler_params=_cparams(("arbitrary", "arbitrary")),
        name="moe_ffn",
    )(item_e, item_j, item_start, item_nsub, tail_block, live_steps, row_tok, x1_packed, w_gate_up, w_down,
      b_gate_up.reshape(n_e, 1, two_f), b_gate_up.reshape(n_e, 1, two_f), b_down.reshape(n_e, 1, d))


def _combine_kernel(dest_ref, y_hbm, x1_ref, meta_ref, g_ref, b_ref, o_ref, buf, sem, *, alpha, tm, chunks):
    i = pl.program_id(0)
    per_step = tm * TOP_K

    def issue_tile(step, slot):
        def body(tt, _):
            for kk in range(TOP_K):
                src = pl.ds(pl.multiple_of(dest_ref[step * per_step + tt * TOP_K + kk] * chunks, chunks),
                            chunks)
                dst = pl.ds(pl.multiple_of((kk * tm + tt) * chunks, chunks), chunks)
                pltpu.make_async_copy(y_hbm.at[src, :], buf.at[slot, dst, :], sem.at[slot]).start(
                    priority=kk % 2)
            return 0
        lax.fori_loop(0, tm, body, 0, unroll=DMA_UNROLL // TOP_K)

    @pl.when(i == 0)
    def _():
        issue_tile(0, 0)

    @pl.when(i + 1 < pl.num_programs(0))
    def _():
        issue_tile(i + 1, (i + 1) % 2)

    slot = i % 2
    _wait_slabs(y_hbm, buf, sem, slot, per_step, chunks)
    meta = meta_ref[...]
    y = jnp.zeros(x1_ref.shape, F32)
    for kk in range(TOP_K):
        rows = _from_slabs(buf.at[slot], kk * tm * chunks, tm, chunks)
        y = y + meta[:, TOP_K + kk:TOP_K + kk + 1] * rows
    o_ref[...] = _layer_norm(alpha * x1_ref[...] + y, g_ref[...], b_ref[...])


def _combine(dest_flat, y_rows, x1, meta, ln_g, ln_b, alpha, tm=128):
    t, d = x1.shape
    tm = min(tm, t)
    chunks = d // LANES
    row = lambda w: pl.BlockSpec((tm, w), lambda i, dest: (i, 0))
    const = pl.BlockSpec((1, d), lambda i, dest: (0, 0))
    return pl.pallas_call(
        functools.partial(_combine_kernel, alpha=alpha, tm=tm, chunks=chunks),
        grid_spec=pltpu.PrefetchScalarGridSpec(
            num_scalar_prefetch=1,
            grid=(t // tm,),
            in_specs=[pl.BlockSpec(memory_space=pl.ANY), row(d), row(LANES), const, const],
            out_specs=row(d),
            scratch_shapes=[pltpu.VMEM((2, TOP_K * tm * chunks, LANES), F32),
                            pltpu.SemaphoreType.DMA((2,))],
        ),
        out_shape=jax.ShapeDtypeStruct((t, d), F32),
        compiler_params=_cparams(("arbitrary",)),
        name="combine_ln",
    )(dest_flat, y_rows, x1, meta, ln_g.reshape(1, d).astype(F32), ln_b.reshape(1, d).astype(F32))


def _routing_plan(meta, counts_f, t, r_max):
    idx = meta[:, 0:TOP_K].astype(jnp.int32)
    rank = meta[:, 2 * TOP_K:3 * TOP_K].astype(jnp.int32)
    counts = counts_f[0, :N_EXPERTS].astype(jnp.int32)
    n128 = (counts + ROW_BLOCK - 1) // ROW_BLOCK
    padded = n128 * ROW_BLOCK
    pad_start = jnp.cumsum(padded) - padded
    dest = pad_start[idx] + rank
    tk = t * TOP_K
    n_rows = (tk + ROW_BLOCK - 1) // ROW_BLOCK * ROW_BLOCK + N_EXPERTS * ROW_BLOCK
    flat_tok = jnp.arange(tk, dtype=jnp.int32) // TOP_K
    row_tok = jnp.zeros((n_rows,), jnp.int32).at[dest.reshape(tk)].set(flat_tok)

    subs = r_max // ROW_BLOCK
    n_items = N_EXPERTS + (n_rows // ROW_BLOCK - N_EXPERTS) // subs
    items_e = (n128 + subs - 1) // subs
    items_end = jnp.cumsum(items_e)
    total = items_end[-1]
    slot = jnp.arange(n_items, dtype=jnp.int32)
    live = slot < total
    s_eff = jnp.minimum(slot, total - 1)
    e = jnp.minimum(jnp.searchsorted(items_end, s_eff, side='right'), N_EXPERTS - 1).astype(jnp.int32)
    local = s_eff - (items_end[e] - items_e[e])
    item_start = (pad_start[e] + local * r_max).astype(jnp.int32)
    item_nsub = jnp.where(live, jnp.clip(n128[e] - local * subs, 0, subs), 0).astype(jnp.int32)
    tail_block = jnp.sum(n128).reshape(1).astype(jnp.int32)
    return dest, row_tok, e, live, item_start, item_nsub, tail_block


def kernel(x, w_in, ret_gn_gain, w_out, ln1_gain, ln1_bias, w_router, b_router, w_gate_up, b_gate_up,
           w_down, b_down, ln2_gain, ln2_bias):
    b, s, d = x.shape
    t = b * s
    depth = w_in.shape[0]
    alpha = (2 * depth) ** 0.25
    r_max = 1280
    tf = 512
    n_f = (w_gate_up.shape[-1] // 2) // tf
    xt = x.reshape(t, d)
    for layer in range(depth):
        proj = _in_proj(xt.astype(BF16), w_in[layer].astype(BF16))
        sb = _stickbreak(proj, b, s)
        ret = _retention(proj, ret_gn_gain[layer], b, s)
        x1, x1_slabs, meta, counts = _outproj(sb, ret, xt, w_out[layer].astype(BF16), ln1_gain[layer],
                                              ln1_bias[layer], w_router[layer], b_router[layer], alpha)
        dest, row_tok, item_e, live, item_start, item_nsub, tail_block = _routing_plan(
            meta, counts, t, r_max)
        dest_flat = dest.reshape(t * TOP_K)
        item_j = jnp.where(live, -1, n_f - 1).astype(jnp.int32)
        live_steps = (jnp.sum(live.astype(jnp.int32)) * n_f).reshape(1)
        y_rows = _moe_ffn(item_e, item_j, item_start, item_nsub, tail_block, live_steps, row_tok, x1_slabs,
                          w_gate_up[layer], b_gate_up[layer], w_down[layer], b_down[layer], r_max, tf)
        xt = _combine(dest_flat, y_rows, x1, meta, ln2_gain[layer], ln2_bias[layer], alpha)
    return xt.reshape(b, s, d)
```

```python
import functools
import math

import jax
import jax.numpy as jnp
from jax import lax
from jax.experimental import pallas as pl
from jax.experimental.pallas import tpu as pltpu

F32 = jnp.float32
BF16 = jnp.bfloat16

HEAD_DIM = 128
SB_HEADS = 8
RET_HEADS = 8
CHUNK = 64
ROPE_BASE = 10000.0
N_EXPERTS = 32
TOP_K = 4
SWIGLU_LIMIT = 7.0
SWIGLU_ALPHA = 1.702
LN_EPS = 1e-5
GN_EPS = 1e-5

V7X_VMEM_LIMIT_BYTES = 56 * 1024 * 1024
LANES = 128
ROW_BLOCK = 128
NEG_BIG = -1e30
EXP_UNDERFLOW = -105.0


def _cparams(sem, vmem=V7X_VMEM_LIMIT_BYTES):
    return pltpu.CompilerParams(dimension_semantics=sem, vmem_limit_bytes=vmem)


def _matmul_kernel(x_ref, w_ref, o_ref):
    o_ref[...] = jnp.dot(x_ref[...], w_ref[...], preferred_element_type=F32).astype(o_ref.dtype)


def _in_proj(x_bf, w_bf, tm=1024, tn=1024):
    t, d = x_bf.shape
    n = w_bf.shape[1]
    tm = min(tm, t)
    return pl.pallas_call(
        _matmul_kernel,
        grid=(n // tn, t // tm),
        in_specs=[pl.BlockSpec((tm, d), lambda j, i: (i, 0)),
                  pl.BlockSpec((d, tn), lambda j, i: (0, j))],
        out_specs=pl.BlockSpec((tm, tn), lambda j, i: (i, j)),
        out_shape=jax.ShapeDtypeStruct((t, n), BF16),
        compiler_params=_cparams(("arbitrary", "arbitrary")),
        name="in_proj",
    )(x_bf, w_bf)


def _sb_kernel(q_ref, k_ref, v_ref, o_ref, *, seq, scale, group, sub):
    blk = 128
    nq = seq // blk
    row = lax.broadcasted_iota(jnp.int32, (blk, blk), 0)
    col = lax.broadcasted_iota(jnp.int32, (blk, blk), 1)
    causal = col < row
    r2 = lax.broadcasted_iota(jnp.int32, (2 * blk, 2 * blk), 0)
    c2 = lax.broadcasted_iota(jnp.int32, (2 * blk, 2 * blk), 1)
    cum_rhs = jnp.where((c2 >= blk) | ((r2 & (blk - 1)) >= c2), 1.0, 0.0).astype(BF16)

    n_sub = group // sub
    causal_sub = jnp.concatenate([causal] * sub, axis=0)
    heads_of = lambda s: range(s * sub, (s + 1) * sub)
    cols = lambda g: slice(g * blk, (g + 1) * blk)

    def block(qbs, kb, carries, accs, masked):
        ks = pl.ds(pl.multiple_of(kb * blk, blk), blk)
        zs = [jnp.concatenate(
            [lax.dot_general(qbs[g], k_ref[ks, cols(g)], (((1,), (1,)), ((), ())),
                             preferred_element_type=F32) for g in heads_of(s)], axis=0) * scale
              for s in range(n_sub)]
        sums = []
        for z in zs:
            neg_z = -z
            lnb = jnp.minimum(neg_z, 0.0) - jnp.log(1.0 + jnp.exp(jnp.minimum(z, neg_z)))
            if masked:
                lnb = jnp.where(causal_sub, lnb, 0.0)
            hi = lnb.astype(BF16)
            lo = (lnb - hi.astype(F32)).astype(BF16)
            sums.append(jnp.dot(jnp.concatenate([hi, lo], axis=1), cum_rhs, preferred_element_type=F32))
        new_carries, new_accs = [], []
        for s in range(n_sub):
            incl = sums[s][:, :blk]
            total = sums[s][:, blk:]
            w = jnp.exp(zs[s] + incl + carries[s])
            if masked:
                w = jnp.where(causal_sub, w, 0.0)
            w = w.astype(BF16)
            for n, g in enumerate(heads_of(s)):
                new_accs.append(accs[g] + jnp.dot(w[n * blk:(n + 1) * blk], v_ref[ks, cols(g)],
                                                  preferred_element_type=F32))
            new_carries.append(carries[s] + total)
        return tuple(new_carries), tuple(new_accs)

    def q_body(qi, _):
        qs = pl.ds(pl.multiple_of(qi * blk, blk), blk)
        qbs = [q_ref[qs, cols(g)] for g in range(group)]
        state = block(qbs, qi, (jnp.zeros((sub * blk, blk), F32),) * n_sub,
                      (jnp.zeros((blk, blk), F32),) * group, True)

        def some_weight_left(carries):
            top = functools.reduce(jnp.maximum, carries)
            return (jnp.max(top) > EXP_UNDERFLOW).astype(jnp.int32)

        def live(st):
            t, more, _, _ = st
            return jnp.logical_and(t < qi, more > 0)

        def kb_body(st):
            t, _, carries, accs = st
            carries, accs = block(qbs, qi - 1 - t, carries, accs, False)
            return t + 1, some_weight_left(carries), carries, accs

        _, _, _, accs = lax.while_loop(live, kb_body, (0, 1, state[0], state[1]))
        for g in range(group):
            o_ref[qs, cols(g)] = accs[g].astype(o_ref.dtype)
        return 0

    lax.fori_loop(0, nq, q_body, 0)


def _stickbreak(proj, batch, seq, group=8, sub=4):
    t = batch * seq
    h = SB_HEADS
    n_groups = h // group
    width = group * HEAD_DIM
    kern = functools.partial(_sb_kernel, seq=seq, scale=1.0 / math.sqrt(HEAD_DIM), group=group,
                             sub=sub)
    spec = lambda off: pl.BlockSpec((seq, width), lambda b, hg: (b, off * n_groups + hg))
    return pl.pallas_call(
        kern,
        grid=(batch, n_groups),
        in_specs=[spec(0), spec(1), spec(2)],
        out_specs=pl.BlockSpec((seq, width), lambda b, hg: (b, hg)),
        out_shape=jax.ShapeDtypeStruct((t, h * HEAD_DIM), BF16),
        compiler_params=_cparams(("arbitrary", "arbitrary")),
        name="stickbreak",
    )(proj, proj, proj)


def _ret_kernel(q_ref, k_ref, v_ref, g_ref, cos_ref, sin_ref, intra_ref, qdec_ref, kdec_ref,
                cdec_ref, gain_ref, o_ref, *, seq, heads):
    n_chunks = seq // CHUNK
    half = HEAD_DIM // 2
    k_scale = HEAD_DIM ** -0.5
    cols = lambda h: slice(h * HEAD_DIM, (h + 1) * HEAD_DIM)
    contract_last = (((1,), (1,)), ((), ()))
    contract_rows = (((0,), (0,)), ((), ()))

    def chunk(n, states):
        rs = pl.ds(pl.multiple_of(n * CHUNK, CHUNK), CHUNK)
        cos = cos_ref[rs, :]
        sin = sin_ref[rs, :]
        qrs, krs, crosses, kvs = [], [], [], []
        for h in range(heads):
            q = q_ref[rs, cols(h)].astype(F32)
            k = k_ref[rs, cols(h)].astype(F32)
            qr = q * cos + pltpu.roll(q, half, 1) * sin
            kr = (k * cos + pltpu.roll(k, half, 1) * sin) * k_scale
            qrs.append(qr.astype(BF16))
            krs.append(kr.astype(BF16))
            crosses.append(jnp.dot((qr * qdec_ref[h]).astype(BF16), states[h].astype(BF16),
                                   preferred_element_type=F32))
            kvs.append(lax.dot_general((kr * kdec_ref[h]).astype(BF16), v_ref[rs, cols(h)], contract_rows,
                                       preferred_element_type=F32))
        scores = [lax.dot_general(qrs[h], krs[h], contract_last, preferred_element_type=F32) * intra_ref[h]
                  for h in range(heads)]
        outs = [crosses[h] + jnp.dot(scores[h].astype(BF16), v_ref[rs, cols(h)], preferred_element_type=F32)
                for h in range(heads)]
        new_states = []
        for h in range(heads):
            o = outs[h]
            mu = jnp.mean(o, axis=-1, keepdims=True)
            var = jnp.mean(jnp.square(o - mu), axis=-1, keepdims=True)
            on = (o - mu) * lax.rsqrt(var + GN_EPS)
            g = g_ref[rs, cols(h)].astype(F32)
            out = on * gain_ref[:, cols(h)] * (g * jax.nn.sigmoid(g))
            o_ref[rs, cols(h)] = out.astype(o_ref.dtype)
            new_states.append(states[h] * cdec_ref[h] + kvs[h])
        return tuple(new_states)

    lax.fori_loop(0, n_chunks, chunk, (jnp.zeros((HEAD_DIM, HEAD_DIM), F32),) * heads)


def _retention_tables(seq):
    d = HEAD_DIM
    inv_freq = ROPE_BASE ** (-jnp.arange(0, d, 2, dtype=F32) / d)
    ang = jnp.arange(seq, dtype=F32)[:, None] * inv_freq[None, :]
    cos, sin = jnp.cos(ang), jnp.sin(ang)
    cos_full = jnp.concatenate([cos, cos], axis=-1)
    sin_signed = jnp.concatenate([-sin, sin], axis=-1)
    log_gamma = jnp.log1p(-jnp.exp2(-5.0 - jnp.arange(RET_HEADS, dtype=F32)))
    i = jnp.arange(CHUNK, dtype=F32)
    intra = jnp.exp(log_gamma[:, None, None] * jnp.abs(i[:, None] - i[None, :]))
    k_decay = jnp.exp(log_gamma[:, None] * (CHUNK - 1 - i))
    q_decay = jnp.exp(log_gamma[:, None] * (i + 1.0))
    c_decay = jnp.exp(log_gamma * CHUNK)
    bc = lambda a: jnp.broadcast_to(a[..., None], a.shape + (d,))
    return cos_full, sin_signed, intra, bc(q_decay), bc(k_decay), bc(c_decay[:, None])


def _retention(proj, gn_gain, batch, seq):
    t = batch * seq
    h = RET_HEADS
    base = 3 * SB_HEADS
    cos_full, sin_signed, intra, qdec, kdec, cdec = _retention_tables(seq)
    width = h * HEAD_DIM
    group0 = base // h
    spec = lambda off: pl.BlockSpec((seq, width), lambda b: (b, group0 + off))
    full = pl.BlockSpec((seq, HEAD_DIM), lambda b: (0, 0))
    table = lambda r, c: pl.BlockSpec((h, r, c), lambda b: (0, 0, 0))
    return pl.pallas_call(
        functools.partial(_ret_kernel, seq=seq, heads=h),
        grid=(batch,),
        in_specs=[spec(0), spec(1), spec(2), spec(3), full, full,
                  table(CHUNK, CHUNK), table(CHUNK, HEAD_DIM), table(CHUNK, HEAD_DIM),
                  table(1, HEAD_DIM),
                  pl.BlockSpec((1, width), lambda b: (0, 0))],
        out_specs=pl.BlockSpec((seq, width), lambda b: (b, 0)),
        out_shape=jax.ShapeDtypeStruct((t, width), BF16),
        compiler_params=_cparams(("arbitrary",)),
        name="retention",
    )(proj, proj, proj, proj, cos_full, sin_signed, intra, qdec, kdec, cdec,
      gn_gain.reshape(1, h * HEAD_DIM).astype(F32))


def _layer_norm(hid, gain, bias):
    mu = jnp.mean(hid, axis=-1, keepdims=True)
    cen = hid - mu
    var = jnp.mean(jnp.square(cen), axis=-1, keepdims=True)
    return cen * lax.rsqrt(var + LN_EPS) * gain + bias


def _to_slabs(slab_ref, base, rows, value):
    chunks = value.shape[1] // LANES
    for c in range(chunks):
        slab_ref[pl.ds(base + c, rows, stride=chunks), :] = value[:, c * LANES:(c + 1) * LANES]


def _bf16_bits(x):
    u = pltpu.bitcast(x, jnp.uint32)
    return lax.shift_right_logical(u + jnp.uint32(0x7FFF) + (lax.shift_right_logical(u, jnp.uint32(16))
                                                             & jnp.uint32(1)), jnp.uint32(16))


def _to_packed_slabs(slab_ref, rows, value):
    words = value.shape[1] // (2 * LANES)
    for c in range(words):
        lo = _bf16_bits(value[:, (2 * c) * LANES:(2 * c + 1) * LANES])
        hi = _bf16_bits(value[:, (2 * c + 1) * LANES:(2 * c + 2) * LANES])
        slab_ref[pl.ds(c, rows, stride=words), :] = lo | lax.shift_left(hi, jnp.uint32(16))


def _from_packed_slabs(slab_ref, base, rows, words, pitch):
    out = []
    for c in range(words):
        w = slab_ref[pl.ds(base + c, rows, stride=pitch), :]
        out.append(pltpu.bitcast(lax.shift_left(w, jnp.uint32(16)), F32).astype(BF16))
        out.append(pltpu.bitcast(w & jnp.uint32(0xFFFF0000), F32).astype(BF16))
    return jnp.concatenate(out, axis=1)


def _outproj_kernel(sb_ref, ret_ref, x_ref, w_ref, g_ref, b_ref, wr_ref, br_ref,
                    x1_ref, x1s_ref, meta_ref, cnt_ref, *, alpha, sb_width):
    tm = x_ref.shape[0]
    mix = jnp.dot(sb_ref[...], w_ref[:sb_width, :], preferred_element_type=F32)
    mix = mix + jnp.dot(ret_ref[...], w_ref[sb_width:, :], preferred_element_type=F32)
    x1 = _layer_norm(alpha * x_ref[...] + mix, g_ref[...], b_ref[...])
    x1_ref[...] = x1
    _to_packed_slabs(x1s_ref, tm, x1)

    logits = jnp.dot(x1.astype(BF16), wr_ref[...], preferred_element_type=F32) + br_ref[...]
    lane = lax.broadcasted_iota(jnp.int32, (tm, LANES), 1).astype(F32)
    vals = logits
    tops, idxs, hots = [], [], []
    for _ in range(TOP_K):
        m = jnp.max(vals, axis=-1, keepdims=True)
        idx = jnp.min(jnp.where(vals == m, lane, float(LANES)), axis=-1, keepdims=True)
        hot = lane == idx
        vals = jnp.where(hot, NEG_BIG * 2.0, vals)
        tops.append(m)
        idxs.append(idx)
        hots.append(hot)
    exps = [jnp.exp(m - tops[0]) for m in tops]
    denom = exps[0] + exps[1] + exps[2] + exps[3]
    gates = [e / denom for e in exps]

    @pl.when(pl.program_id(0) == 0)
    def _():
        cnt_ref[...] = jnp.zeros_like(cnt_ref)

    multi = jnp.zeros((tm, LANES), F32)
    for hot in hots:
        multi = multi + jnp.where(hot, 1.0, 0.0)
    r = lax.broadcasted_iota(jnp.int32, (tm, tm), 0)
    c = lax.broadcasted_iota(jnp.int32, (tm, tm), 1)
    strict_lower = jnp.where(c < r, 1.0, 0.0).astype(BF16)
    before = jnp.dot(strict_lower, multi.astype(BF16), preferred_element_type=F32) + cnt_ref[...]
    cnt_ref[...] = cnt_ref[...] + jnp.sum(multi, axis=0, keepdims=True)

    meta = jnp.zeros((tm, LANES), F32)
    for kk in range(TOP_K):
        rank = jnp.sum(jnp.where(hots[kk], before, 0.0), axis=-1, keepdims=True)
        meta = jnp.where(lane == kk, idxs[kk], meta)
        meta = jnp.where(lane == TOP_K + kk, gates[kk], meta)
        meta = jnp.where(lane == 2 * TOP_K + kk, rank, meta)
    meta_ref[...] = meta


def _outproj(sb, ret, xt, w_out_bf, ln_g, ln_b, w_router, b_router, alpha, tm=256):
    t, d = xt.shape
    tm = min(tm, t)
    sbw = sb.shape[1]
    wr = jnp.zeros((d, LANES), BF16).at[:, :N_EXPERTS].set(w_router.astype(BF16))
    br = jnp.full((1, LANES), NEG_BIG, F32).at[0, :N_EXPERTS].set(b_router.astype(F32))
    row = lambda w: pl.BlockSpec((tm, w), lambda i: (i, 0))
    const = lambda r, c: pl.BlockSpec((r, c), lambda i: (0, 0))
    return pl.pallas_call(
        functools.partial(_outproj_kernel, alpha=alpha, sb_width=sbw),
        grid=(t // tm,),
        in_specs=[row(sbw), row(ret.shape[1]), row(d), const(d, d), const(1, d), const(1, d),
                  const(d, LANES), const(1, LANES)],
        out_specs=[row(d), pl.BlockSpec((tm * (d // (2 * LANES)), LANES), lambda i: (i, 0)), row(LANES),
                   const(1, LANES)],
        out_shape=[jax.ShapeDtypeStruct((t, d), F32),
                   jax.ShapeDtypeStruct((t * (d // (2 * LANES)), LANES), jnp.uint32),
                   jax.ShapeDtypeStruct((t, LANES), F32),
                   jax.ShapeDtypeStruct((1, LANES), F32)],
        compiler_params=_cparams(("arbitrary",)),
        name="outproj_ln_router",
    )(sb, ret, xt, w_out_bf, ln_g.reshape(1, d).astype(F32), ln_b.reshape(1, d).astype(F32), wr, br)


DMA_UNROLL = 8


def _wait_slabs(hbm, buf, sem, slot, count, chunks):
    def body(n, _):
        pltpu.make_async_copy(hbm.at[pl.ds(0, chunks), :], buf.at[slot, pl.ds(0, chunks), :],
                              sem.at[slot]).wait()
        return 0
    lax.fori_loop(0, count, body, 0, unroll=DMA_UNROLL)


def _moe_kernel(ie_ref, ij_ref, ist_ref, ins_ref, tail_ref, live_ref, dest_ref, x_hbm, wgu_hbm, wd_hbm, bg_ref,
                bu_ref, bd_ref, y_hbm, x_vmem, acc, wg_f, wu_f, wd_f, wg_bf, wu_bf, wd_bf, stage, gbuf, tok_ref,
                sem_g, sem_out, sem_w, *, n_f):
    del ij_ref
    i = pl.program_id(0)
    j = pl.program_id(1)
    nsub = ins_ref[i]
    start = ist_ref[i]
    d = acc.shape[1]
    tf = wg_bf.shape[1]
    d_ff = n_f * tf
    chunks = d // LANES
    block_rows = ROW_BLOCK * chunks

    def weight_copies(e, jj, slot):
        col_g = pl.ds(pl.multiple_of(jj * tf, tf), tf)
        col_u = pl.ds(pl.multiple_of(d_ff + jj * tf, tf), tf)
        copies = []
        for half in range(2):
            rows_k = pl.ds(half * (d // 2), d // 2)
            rows_f = pl.ds(pl.multiple_of(jj * tf + half * (tf // 2), tf // 2), tf // 2)
            dst_f = pl.ds(half * (tf // 2), tf // 2)
            copies += [
                pltpu.make_async_copy(wgu_hbm.at[e, rows_k, col_g], wg_f.at[slot, rows_k, :], sem_w.at[slot]),
                pltpu.make_async_copy(wgu_hbm.at[e, rows_k, col_u], wu_f.at[slot, rows_k, :], sem_w.at[slot]),
                pltpu.make_async_copy(wd_hbm.at[e, rows_f, :], wd_f.at[slot, dst_f, :], sem_w.at[slot]),
            ]
        return copies

    def start_weights(e, jj, slot):
        for n, c in enumerate(weight_copies(e, jj, slot)):
            c.start(priority=n % 2)

    step = i * n_f + j
    w_slot = step % 2

    @pl.when(step == 0)
    def _():
        start_weights(ie_ref[0], 0, 0)

    @pl.when(step + 1 < live_ref[0])
    def _():
        last = j == n_f - 1
        start_weights(ie_ref[jnp.where(last, i + 1, i)], jnp.where(last, 0, j + 1), 1 - w_slot)

    def rows_of(r):
        return pl.ds(pl.multiple_of(r * ROW_BLOCK, ROW_BLOCK), ROW_BLOCK)

    def slabs_of(block):
        return pl.ds(pl.multiple_of(block * block_rows, block_rows), block_rows)

    def for_each(lo, hi, fn):
        def body(r, _):
            fn(r)
            return 0
        lax.fori_loop(lo, hi, body, 0)

    @pl.when((i == 0) & (j == 0))
    def _():
        stage[0] = jnp.zeros(stage.shape[1:], F32)

        def zero_copy(bk):
            return pltpu.make_async_copy(stage.at[0], y_hbm.at[slabs_of(bk), :], sem_out.at[0])

        n_blocks = y_hbm.shape[0] // block_rows
        for_each(tail_ref[0], n_blocks, lambda bk: zero_copy(bk).start())
        for_each(tail_ref[0], n_blocks, lambda bk: zero_copy(bk).wait())

    first_block = start // ROW_BLOCK
    words = d // (2 * LANES)

    pitch = words + 1

    def build_row_tokens():
        n_tok = dest_ref.shape[0] // TOP_K

        def clear(m, _):
            for u in range(DMA_UNROLL):
                tok_ref[m * DMA_UNROLL + u] = 0
            return 0
        lax.fori_loop(0, tok_ref.shape[0] // DMA_UNROLL, clear, 0)

        def fill(m, _):
            for u in range(DMA_UNROLL):
                t = m * DMA_UNROLL + u
                for kk in range(TOP_K):
                    tok_ref[dest_ref[kk * n_tok + t]] = t
            return 0
        lax.fori_loop(0, n_tok // DMA_UNROLL, fill, 0)

    def start_gather(item):
        row0 = ist_ref[item]

        def body(m, _):
            for u in range(DMA_UNROLL):
                r = m * DMA_UNROLL + u
                src = pl.ds(pl.multiple_of(tok_ref[row0 + r] * words, words), words)
                pltpu.make_async_copy(x_hbm.at[src, :], gbuf.at[pl.ds(r * pitch, words), :], sem_g).start(
                    priority=u % 2)
            return 0
        lax.fori_loop(0, ins_ref[item] * (ROW_BLOCK // DMA_UNROLL), body, 0)

    def wait_gather(item):
        def body(m, _):
            for _u in range(DMA_UNROLL):
                pltpu.make_async_copy(x_hbm.at[pl.ds(0, words), :], gbuf.at[pl.ds(0, words), :], sem_g).wait()
            return 0
        lax.fori_loop(0, ins_ref[item] * (ROW_BLOCK // DMA_UNROLL), body, 0)

    def out_copy(r, slot):
        return pltpu.make_async_copy(stage.at[slot], y_hbm.at[slabs_of(first_block + r), :],
                                     sem_out.at[slot])

    def span(r, n_blocks):
        return pl.ds(pl.multiple_of(r * ROW_BLOCK, ROW_BLOCK), n_blocks * ROW_BLOCK)

    def hidden(rows):
        xb = x_vmem[rows, :]
        gate = jnp.dot(xb, wg_bf[...], preferred_element_type=F32) + bg_ref[...]
        up = jnp.dot(xb, wu_bf[...], preferred_element_type=F32) + bu_ref[...]
        gate = jnp.minimum(gate, SWIGLU_LIMIT)
        up = jnp.clip(up, -SWIGLU_LIMIT, SWIGLU_LIMIT)
        act = (up + 1.0) * (gate * jax.nn.sigmoid(SWIGLU_ALPHA * gate))
        return jnp.dot(act.astype(BF16), wd_bf[...], preferred_element_type=F32)

    @pl.when(nsub > 0)
    def _():
        @pl.when(j == 0)
        def _():
            @pl.when(i == 0)
            def _():
                build_row_tokens()
                start_gather(0)
            wait_gather(i)
            bias_rows = jnp.broadcast_to(bd_ref[...], (ROW_BLOCK, d))

            def unpack(r):
                x_vmem[rows_of(r), :] = _from_packed_slabs(gbuf, r * (ROW_BLOCK * pitch), ROW_BLOCK, words, pitch)
                acc[rows_of(r), :] = bias_rows
            for_each(0, nsub, unpack)

            @pl.when(i + 1 < pl.num_programs(0))
            def _():
                start_gather(i + 1)

        for c in weight_copies(ie_ref[i], j, w_slot):
            c.wait()
        wg_bf[...] = wg_f[w_slot].astype(BF16)
        wu_bf[...] = wu_f[w_slot].astype(BF16)
        wd_bf[...] = wd_f[w_slot].astype(BF16)

        n_pairs = nsub // 2
        odd = nsub % 2 == 1

        @pl.when(j < n_f - 1)
        def _():
            def pair(p):
                acc[span(2 * p, 2), :] += hidden(span(2 * p, 2))
            for_each(0, n_pairs, pair)

            @pl.when(odd)
            def _():
                acc[span(nsub - 1, 1), :] += hidden(span(nsub - 1, 1))

        @pl.when(j == n_f - 1)
        def _():
            def write_out(r, slot, final, reuse):
                @pl.when(reuse)
                def _():
                    out_copy(r, slot).wait()
                _to_slabs(stage.at[slot], 0, ROW_BLOCK, final)
                out_copy(r, slot).start()

            def pair(p):
                final = acc[span(2 * p, 2), :] + hidden(span(2 * p, 2))
                write_out(2 * p, 0, final[:ROW_BLOCK], p > 0)
                write_out(2 * p + 1, 1, final[ROW_BLOCK:], p > 0)
            for_each(0, n_pairs, pair)

            @pl.when(odd)
            def _():
                final = acc[span(nsub - 1, 1), :] + hidden(span(nsub - 1, 1))
                write_out(nsub - 1, 0, final, n_pairs > 0)

            out_copy(0, 0).wait()

            @pl.when(n_pairs > 0)
            def _():
                out_copy(0, 1).wait()


def _moe_ffn(item_e, item_j, item_start, item_nsub, tail_block, live_steps, dest_flat, x1_packed, n_rows,
             w_gate_up, b_gate_up, w_down, b_down, r_max, tf=512):
    n_e, d, two_f = w_gate_up.shape
    chunks = d // LANES
    words = d // (2 * LANES)
    d_ff = two_f // 2
    n_f = d_ff // tf
    n_items = item_e.shape[0]

    def jf(i, j, ij):
        return jnp.where(ij[i] < 0, j, ij[i])

    hbm = pl.BlockSpec(memory_space=pl.ANY)
    in_specs = [
        hbm, hbm, hbm,
        pl.BlockSpec((None, 1, tf), lambda i, j, ie, ij, ist, ins, tl, lv, tk: (ie[i], 0, jf(i, j, ij))),
        pl.BlockSpec((None, 1, tf), lambda i, j, ie, ij, ist, ins, tl, lv, tk: (ie[i], 0, n_f + jf(i, j, ij))),
        pl.BlockSpec((None, 1, d), lambda i, j, ie, ij, ist, ins, tl, lv, tk: (ie[i], 0, 0)),
    ]
    return pl.pallas_call(
        functools.partial(_moe_kernel, n_f=n_f),
        grid_spec=pltpu.PrefetchScalarGridSpec(
            num_scalar_prefetch=7,
            grid=(n_items, n_f),
            in_specs=in_specs,
            out_specs=pl.BlockSpec(memory_space=pl.ANY),
            scratch_shapes=[pltpu.VMEM((r_max, d), BF16), pltpu.VMEM((r_max, d), F32),
                            pltpu.VMEM((2, d, tf), F32), pltpu.VMEM((2, d, tf), F32),
                            pltpu.VMEM((2, tf, d), F32),
                            pltpu.VMEM((d, tf), BF16), pltpu.VMEM((d, tf), BF16),
                            pltpu.VMEM((tf, d), BF16),
                            pltpu.VMEM((2, ROW_BLOCK * chunks, LANES), F32),
                            pltpu.VMEM((r_max * (words + 1), LANES), jnp.uint32),
                            pltpu.SMEM((n_rows,), jnp.int32),
                            pltpu.SemaphoreType.DMA(()), pltpu.SemaphoreType.DMA((2,)),
                            pltpu.SemaphoreType.DMA((2,))],
        ),
        out_shape=jax.ShapeDtypeStruct((n_rows * chunks, LANES), F32),
        compiler_params=_cparams(("arbitrary", "arbitrary")),
        name="moe_ffn",
    )(item_e, item_j, item_start, item_nsub, tail_block, live_steps, dest_flat, x1_packed, w_gate_up, w_down,
      b_gate_up.reshape(n_e, 1, two_f), b_gate_up.reshape(n_e, 1, two_f), b_down.reshape(n_e, 1, d))


def _combine_kernel(dest_ref, y_hbm, x1_ref, meta_ref, g_ref, b_ref, o_ref, buf, sem, *, alpha, tm, chunks):
    i = pl.program_id(0)
    per_step = tm * TOP_K
    n_tok = dest_ref.shape[0] // TOP_K
    pitch = chunks + 1

    def issue_tile(step, slot):
        def body(tt, _):
            for kk in range(TOP_K):
                src = pl.ds(pl.multiple_of(dest_ref[kk * n_tok + step * tm + tt] * chunks, chunks), chunks)
                dst = pl.ds((kk * tm + tt) * pitch, chunks)
                pltpu.make_async_copy(y_hbm.at[src, :], buf.at[slot, dst, :], sem.at[slot]).start(
                    priority=kk % 2)
            return 0
        lax.fori_loop(0, tm, body, 0, unroll=DMA_UNROLL // TOP_K)

    @pl.when(i == 0)
    def _():
        issue_tile(0, 0)

    @pl.when(i + 1 < pl.num_programs(0))
    def _():
        issue_tile(i + 1, (i + 1) % 2)

    slot = i % 2
    _wait_slabs(y_hbm, buf, sem, slot, per_step, chunks)
    meta = meta_ref[...]
    y = jnp.zeros(x1_ref.shape, F32)
    for kk in range(TOP_K):
        rows = jnp.concatenate(
            [buf[slot, pl.ds(kk * tm * pitch + c, tm, stride=pitch), :] for c in range(chunks)], axis=1)
        y = y + meta[:, TOP_K + kk:TOP_K + kk + 1] * rows
    o_ref[...] = _layer_norm(alpha * x1_ref[...] + y, g_ref[...], b_ref[...])


def _combine(dest_flat, y_rows, x1, meta, ln_g, ln_b, alpha, tm=128):
    t, d = x1.shape
    tm = min(tm, t)
    chunks = d // LANES
    row = lambda w: pl.BlockSpec((tm, w), lambda i, dest: (i, 0))
    const = pl.BlockSpec((1, d), lambda i, dest: (0, 0))
    return pl.pallas_call(
        functools.partial(_combine_kernel, alpha=alpha, tm=tm, chunks=chunks),
        grid_spec=pltpu.PrefetchScalarGridSpec(
            num_scalar_prefetch=1,
            grid=(t // tm,),
            in_specs=[pl.BlockSpec(memory_space=pl.ANY), row(d), row(LANES), const, const],
            out_specs=row(d),
            scratch_shapes=[pltpu.VMEM((2, TOP_K * tm * (chunks + 1), LANES), F32),
                            pltpu.SemaphoreType.DMA((2,))],
        ),
        out_shape=jax.ShapeDtypeStruct((t, d), F32),
        compiler_params=_cparams(("arbitrary",)),
        name="combine_ln",
    )(dest_flat, y_rows, x1, meta, ln_g.reshape(1, d).astype(F32), ln_b.reshape(1, d).astype(F32))


def _routing_plan(meta, counts_f, t, r_max):
    meta_t = meta.T
    idx = meta_t[0:TOP_K].astype(jnp.int32)
    rank = meta_t[2 * TOP_K:3 * TOP_K].astype(jnp.int32)
    counts = counts_f[0, :N_EXPERTS].astype(jnp.int32)
    n128 = (counts + ROW_BLOCK - 1) // ROW_BLOCK
    padded = n128 * ROW_BLOCK
    pad_start = jnp.cumsum(padded) - padded
    dest = pad_start[idx] + rank
    tk = t * TOP_K
    n_rows = (tk + ROW_BLOCK - 1) // ROW_BLOCK * ROW_BLOCK + N_EXPERTS * ROW_BLOCK

    subs = r_max // ROW_BLOCK
    n_items = N_EXPERTS + (n_rows // ROW_BLOCK - N_EXPERTS) // subs
    items_e = (n128 + subs - 1) // subs
    items_end = jnp.cumsum(items_e)
    total = items_end[-1]
    slot = jnp.arange(n_items, dtype=jnp.int32)
    live = slot < total
    s_eff = jnp.minimum(slot, total - 1)
    e = jnp.minimum(jnp.searchsorted(items_end, s_eff, side='right'), N_EXPERTS - 1).astype(jnp.int32)
    local = s_eff - (items_end[e] - items_e[e])
    item_start = (pad_start[e] + local * r_max).astype(jnp.int32)
    item_nsub = jnp.where(live, jnp.clip(n128[e] - local * subs, 0, subs), 0).astype(jnp.int32)
    tail_block = jnp.sum(n128).reshape(1).astype(jnp.int32)
    return dest.reshape(tk), n_rows, e, live, item_start, item_nsub, tail_block


def kernel(x, w_in, ret_gn_gain, w_out, ln1_gain, ln1_bias, w_router, b_router, w_gate_up, b_gate_up,
           w_down, b_down, ln2_gain, ln2_bias):
    b, s, d = x.shape
    t = b * s
    depth = w_in.shape[0]
    alpha = (2 * depth) ** 0.25
    r_max = 1280
    tf = 512
    n_f = (w_gate_up.shape[-1] // 2) // tf
    xt = x.reshape(t, d)
    for layer in range(depth):
        proj = _in_proj(xt.astype(BF16), w_in[layer].astype(BF16))
        sb = _stickbreak(proj, b, s)
        ret = _retention(proj, ret_gn_gain[layer], b, s)
        x1, x1_slabs, meta, counts = _outproj(sb, ret, xt, w_out[layer].astype(BF16), ln1_gain[layer],
                                              ln1_bias[layer], w_router[layer], b_router[layer], alpha)
        dest_flat, n_rows, item_e, live, item_start, item_nsub, tail_block = _routing_plan(
            meta, counts, t, r_max)
        item_j = jnp.where(live, -1, n_f - 1).astype(jnp.int32)
        live_steps = (jnp.sum(live.astype(jnp.int32)) * n_f).reshape(1)
        y_rows = _moe_ffn(item_e, item_j, item_start, item_nsub, tail_block, live_steps, dest_flat, x1_slabs,
                          n_rows, w_gate_up[layer], b_gate_up[layer], w_down[layer], b_down[layer], r_max, tf)
        xt = _combine(dest_flat, y_rows, x1, meta, ln2_gain[layer], ln2_bias[layer], alpha)
    return xt.reshape(b, s, d)
```

```python
import functools
import math

import jax
import jax.numpy as jnp
from jax import lax
from jax.experimental import pallas as pl
from jax.experimental.pallas import tpu as pltpu

F32 = jnp.float32
BF16 = jnp.bfloat16

HEAD_DIM = 128
SB_HEADS = 8
RET_HEADS = 8
CHUNK = 64
ROPE_BASE = 10000.0
N_EXPERTS = 32
TOP_K = 4
SWIGLU_LIMIT = 7.0
SWIGLU_ALPHA = 1.702
LN_EPS = 1e-5
GN_EPS = 1e-5

V7X_VMEM_LIMIT_BYTES = 56 * 1024 * 1024
LANES = 128
ROW_BLOCK = 128
NEG_BIG = -1e30
EXP_UNDERFLOW = -105.0


def _cparams(sem, vmem=V7X_VMEM_LIMIT_BYTES):
    return pltpu.CompilerParams(dimension_semantics=sem, vmem_limit_bytes=vmem)


def _matmul_kernel(x_ref, w_ref, o_ref, x_bf):
    @pl.when(pl.program_id(1) == 0)
    def _():
        x_bf[...] = x_ref[...].astype(BF16)
    o_ref[...] = jnp.dot(x_bf[...], w_ref[...], preferred_element_type=F32).astype(o_ref.dtype)


def _in_proj(x, w_bf, tm=1024, tn=1024):
    t, d = x.shape
    n = w_bf.shape[1]
    tm = min(tm, t)
    return pl.pallas_call(
        _matmul_kernel,
        grid=(t // tm, n // tn),
        in_specs=[pl.BlockSpec((tm, d), lambda i, j: (i, 0)),
                  pl.BlockSpec((d, tn), lambda i, j: (0, j))],
        out_specs=pl.BlockSpec((tm, tn), lambda i, j: (i, j)),
        out_shape=jax.ShapeDtypeStruct((t, n), BF16),
        scratch_shapes=[pltpu.VMEM((tm, d), BF16)],
        compiler_params=_cparams(("arbitrary", "arbitrary")),
        name="in_proj",
    )(x, w_bf)


def _sb_kernel(q_ref, k_ref, v_ref, o_ref, *, seq, scale, group, sub):
    blk = 128
    nq = seq // blk
    row = lax.broadcasted_iota(jnp.int32, (blk, blk), 0)
    col = lax.broadcasted_iota(jnp.int32, (blk, blk), 1)
    causal = col < row
    r2 = lax.broadcasted_iota(jnp.int32, (2 * blk, 2 * blk), 0)
    c2 = lax.broadcasted_iota(jnp.int32, (2 * blk, 2 * blk), 1)
    cum_rhs = jnp.where((c2 >= blk) | ((r2 & (blk - 1)) >= c2), 1.0, 0.0).astype(BF16)

    n_sub = group // sub
    causal_sub = jnp.concatenate([causal] * sub, axis=0)
    heads_of = lambda s: range(s * sub, (s + 1) * sub)
    cols = lambda g: slice(g * blk, (g + 1) * blk)

    def block(qbs, kb, carries, accs, masked):
        ks = pl.ds(pl.multiple_of(kb * blk, blk), blk)
        zs = [jnp.concatenate(
            [lax.dot_general(qbs[g], k_ref[ks, cols(g)], (((1,), (1,)), ((), ())),
                             preferred_element_type=F32) for g in heads_of(s)], axis=0) * scale
              for s in range(n_sub)]
        sums = []
        for z in zs:
            neg_z = -z
            lnb = jnp.minimum(neg_z, 0.0) - jnp.log(1.0 + jnp.exp(jnp.minimum(z, neg_z)))
            if masked:
                lnb = jnp.where(causal_sub, lnb, 0.0)
            hi = lnb.astype(BF16)
            lo = (lnb - hi.astype(F32)).astype(BF16)
            sums.append(jnp.dot(jnp.concatenate([hi, lo], axis=1), cum_rhs, preferred_element_type=F32))
        new_carries, new_accs = [], []
        for s in range(n_sub):
            incl = sums[s][:, :blk]
            total = sums[s][:, blk:]
            w = jnp.exp(zs[s] + incl + carries[s])
            if masked:
                w = jnp.where(causal_sub, w, 0.0)
            w = w.astype(BF16)
            for n, g in enumerate(heads_of(s)):
                new_accs.append(accs[g] + jnp.dot(w[n * blk:(n + 1) * blk], v_ref[ks, cols(g)],
                                                  preferred_element_type=F32))
            new_carries.append(carries[s] + total)
        return tuple(new_carries), tuple(new_accs)

    def q_body(qi, _):
        qs = pl.ds(pl.multiple_of(qi * blk, blk), blk)
        qbs = [q_ref[qs, cols(g)] for g in range(group)]
        state = block(qbs, qi, (jnp.zeros((sub * blk, blk), F32),) * n_sub,
                      (jnp.zeros((blk, blk), F32),) * group, True)

        def some_weight_left(carries):
            top = functools.reduce(jnp.maximum, carries)
            return (jnp.max(top) > EXP_UNDERFLOW).astype(jnp.int32)

        def live(st):
            t, more, _, _ = st
            return jnp.logical_and(t < qi, more > 0)

        def kb_body(st):
            t, _, carries, accs = st
            carries, accs = block(qbs, qi - 1 - t, carries, accs, False)
            return t + 1, some_weight_left(carries), carries, accs

        _, _, _, accs = lax.while_loop(live, kb_body, (0, 1, state[0], state[1]))
        for g in range(group):
            o_ref[qs, cols(g)] = accs[g].astype(o_ref.dtype)
        return 0

    lax.fori_loop(0, nq, q_body, 0)


def _stickbreak(proj, batch, seq, group=8, sub=4):
    t = batch * seq
    h = SB_HEADS
    n_groups = h // group
    width = group * HEAD_DIM
    kern = functools.partial(_sb_kernel, seq=seq, scale=1.0 / math.sqrt(HEAD_DIM), group=group,
                             sub=sub)
    spec = lambda off: pl.BlockSpec((seq, width), lambda b, hg: (b, off * n_groups + hg))
    return pl.pallas_call(
        kern,
        grid=(batch, n_groups),
        in_specs=[spec(0), spec(1), spec(2)],
        out_specs=pl.BlockSpec((seq, width), lambda b, hg: (b, hg)),
        out_shape=jax.ShapeDtypeStruct((t, h * HEAD_DIM), BF16),
        compiler_params=_cparams(("arbitrary", "arbitrary")),
        name="stickbreak",
    )(proj, proj, proj)


def _ret_kernel(q_ref, k_ref, v_ref, g_ref, cos_ref, sin_ref, intra_ref, qdec_ref, kdec_ref,
                cdec_ref, gain_ref, o_ref, *, seq, heads):
    n_chunks = seq // CHUNK
    half = HEAD_DIM // 2
    k_scale = HEAD_DIM ** -0.5
    cols = lambda h: slice(h * HEAD_DIM, (h + 1) * HEAD_DIM)
    contract_last = (((1,), (1,)), ((), ()))
    contract_rows = (((0,), (0,)), ((), ()))

    def chunk(n, states):
        rs = pl.ds(pl.multiple_of(n * CHUNK, CHUNK), CHUNK)
        cos = cos_ref[rs, :]
        sin = sin_ref[rs, :]
        qrs, krs, crosses, kvs = [], [], [], []
        for h in range(heads):
            q = q_ref[rs, cols(h)].astype(F32)
            k = k_ref[rs, cols(h)].astype(F32)
            qr = q * cos + pltpu.roll(q, half, 1) * sin
            kr = (k * cos + pltpu.roll(k, half, 1) * sin) * k_scale
            qrs.append(qr.astype(BF16))
            krs.append(kr.astype(BF16))
            crosses.append(jnp.dot((qr * qdec_ref[h]).astype(BF16), states[h].astype(BF16),
                                   preferred_element_type=F32))
            kvs.append(lax.dot_general((kr * kdec_ref[h]).astype(BF16), v_ref[rs, cols(h)], contract_rows,
                                       preferred_element_type=F32))
        scores = [lax.dot_general(qrs[h], krs[h], contract_last, preferred_element_type=F32) * intra_ref[h]
                  for h in range(heads)]
        outs = [crosses[h] + jnp.dot(scores[h].astype(BF16), v_ref[rs, cols(h)], preferred_element_type=F32)
                for h in range(heads)]
        new_states = []
        for h in range(heads):
            o = outs[h]
            mu = jnp.mean(o, axis=-1, keepdims=True)
            var = jnp.mean(jnp.square(o - mu), axis=-1, keepdims=True)
            on = (o - mu) * lax.rsqrt(var + GN_EPS)
            g = g_ref[rs, cols(h)].astype(F32)
            out = on * gain_ref[:, cols(h)] * (g * jax.nn.sigmoid(g))
            o_ref[rs, cols(h)] = out.astype(o_ref.dtype)
            new_states.append(states[h] * cdec_ref[h] + kvs[h])
        return tuple(new_states)

    lax.fori_loop(0, n_chunks, chunk, (jnp.zeros((HEAD_DIM, HEAD_DIM), F32),) * heads)


def _retention_tables(seq):
    d = HEAD_DIM
    inv_freq = ROPE_BASE ** (-jnp.arange(0, d, 2, dtype=F32) / d)
    ang = jnp.arange(seq, dtype=F32)[:, None] * inv_freq[None, :]
    cos, sin = jnp.cos(ang), jnp.sin(ang)
    cos_full = jnp.concatenate([cos, cos], axis=-1)
    sin_signed = jnp.concatenate([-sin, sin], axis=-1)
    log_gamma = jnp.log1p(-jnp.exp2(-5.0 - jnp.arange(RET_HEADS, dtype=F32)))
    i = jnp.arange(CHUNK, dtype=F32)
    intra = jnp.exp(log_gamma[:, None, None] * jnp.abs(i[:, None] - i[None, :]))
    k_decay = jnp.exp(log_gamma[:, None] * (CHUNK - 1 - i))
    q_decay = jnp.exp(log_gamma[:, None] * (i + 1.0))
    c_decay = jnp.exp(log_gamma * CHUNK)
    bc = lambda a: jnp.broadcast_to(a[..., None], a.shape + (d,))
    return cos_full, sin_signed, intra, bc(q_decay), bc(k_decay), bc(c_decay[:, None])


def _retention(proj, gn_gain, batch, seq):
    t = batch * seq
    h = RET_HEADS
    base = 3 * SB_HEADS
    cos_full, sin_signed, intra, qdec, kdec, cdec = _retention_tables(seq)
    width = h * HEAD_DIM
    group0 = base // h
    spec = lambda off: pl.BlockSpec((seq, width), lambda b: (b, group0 + off))
    full = pl.BlockSpec((seq, HEAD_DIM), lambda b: (0, 0))
    table = lambda r, c: pl.BlockSpec((h, r, c), lambda b: (0, 0, 0))
    return pl.pallas_call(
        functools.partial(_ret_kernel, seq=seq, heads=h),
        grid=(batch,),
        in_specs=[spec(0), spec(1), spec(2), spec(3), full, full,
                  table(CHUNK, CHUNK), table(CHUNK, HEAD_DIM), table(CHUNK, HEAD_DIM),
                  table(1, HEAD_DIM),
                  pl.BlockSpec((1, width), lambda b: (0, 0))],
        out_specs=pl.BlockSpec((seq, width), lambda b: (b, 0)),
        out_shape=jax.ShapeDtypeStruct((t, width), BF16),
        compiler_params=_cparams(("arbitrary",)),
        name="retention",
    )(proj, proj, proj, proj, cos_full, sin_signed, intra, qdec, kdec, cdec,
      gn_gain.reshape(1, h * HEAD_DIM).astype(F32))


def _layer_norm(hid, gain, bias):
    mu = jnp.mean(hid, axis=-1, keepdims=True)
    cen = hid - mu
    var = jnp.mean(jnp.square(cen), axis=-1, keepdims=True)
    return cen * lax.rsqrt(var + LN_EPS) * gain + bias


def _to_slabs(slab_ref, base, rows, value):
    chunks = value.shape[1] // LANES
    for c in range(chunks):
        slab_ref[pl.ds(base + c, rows, stride=chunks), :] = value[:, c * LANES:(c + 1) * LANES]


def _bf16_bits(x):
    u = pltpu.bitcast(x, jnp.uint32)
    return lax.shift_right_logical(u + jnp.uint32(0x7FFF) + (lax.shift_right_logical(u, jnp.uint32(16))
                                                             & jnp.uint32(1)), jnp.uint32(16))


def _to_packed_slabs(slab_ref, rows, value):
    words = value.shape[1] // (2 * LANES)
    for c in range(words):
        lo = _bf16_bits(value[:, (2 * c) * LANES:(2 * c + 1) * LANES])
        hi = _bf16_bits(value[:, (2 * c + 1) * LANES:(2 * c + 2) * LANES])
        slab_ref[pl.ds(c, rows, stride=words), :] = lo | lax.shift_left(hi, jnp.uint32(16))


def _from_packed_slabs(slab_ref, base, rows, words, pitch):
    out = []
    for c in range(words):
        w = slab_ref[pl.ds(base + c, rows, stride=pitch), :]
        out.append(pltpu.bitcast(lax.shift_left(w, jnp.uint32(16)), F32).astype(BF16))
        out.append(pltpu.bitcast(w & jnp.uint32(0xFFFF0000), F32).astype(BF16))
    return jnp.concatenate(out, axis=1)


def _outproj_kernel(sb_ref, ret_ref, x_ref, w_ref, g_ref, b_ref, wr_ref, br_ref,
                    x1_ref, x1s_ref, meta_ref, cnt_ref, *, alpha, sb_width):
    tm = x_ref.shape[0]
    mix = jnp.dot(sb_ref[...], w_ref[:sb_width, :], preferred_element_type=F32)
    mix = mix + jnp.dot(ret_ref[...], w_ref[sb_width:, :], preferred_element_type=F32)
    x1 = _layer_norm(alpha * x_ref[...] + mix, g_ref[...], b_ref[...])
    x1_ref[...] = x1
    _to_packed_slabs(x1s_ref, tm, x1)

    logits = jnp.dot(x1.astype(BF16), wr_ref[...], preferred_element_type=F32) + br_ref[...]
    lane = lax.broadcasted_iota(jnp.int32, (tm, LANES), 1).astype(F32)
    vals = logits
    tops, idxs, hots = [], [], []
    for _ in range(TOP_K):
        m = jnp.max(vals, axis=-1, keepdims=True)
        idx = jnp.min(jnp.where(vals == m, lane, float(LANES)), axis=-1, keepdims=True)
        hot = lane == idx
        vals = jnp.where(hot, NEG_BIG * 2.0, vals)
        tops.append(m)
        idxs.append(idx)
        hots.append(hot)
    exps = [jnp.exp(m - tops[0]) for m in tops]
    denom = exps[0] + exps[1] + exps[2] + exps[3]
    gates = [e / denom for e in exps]

    @pl.when(pl.program_id(0) == 0)
    def _():
        cnt_ref[...] = jnp.zeros_like(cnt_ref)

    multi = jnp.zeros((tm, LANES), F32)
    for hot in hots:
        multi = multi + jnp.where(hot, 1.0, 0.0)
    r = lax.broadcasted_iota(jnp.int32, (tm, tm), 0)
    c = lax.broadcasted_iota(jnp.int32, (tm, tm), 1)
    strict_lower = jnp.where(c < r, 1.0, 0.0).astype(BF16)
    before = jnp.dot(strict_lower, multi.astype(BF16), preferred_element_type=F32) + cnt_ref[...]
    cnt_ref[...] = cnt_ref[...] + jnp.sum(multi, axis=0, keepdims=True)

    meta = jnp.zeros((tm, LANES), F32)
    for kk in range(TOP_K):
        rank = jnp.sum(jnp.where(hots[kk], before, 0.0), axis=-1, keepdims=True)
        meta = jnp.where(lane == kk, idxs[kk], meta)
        meta = jnp.where(lane == TOP_K + kk, gates[kk], meta)
        meta = jnp.where(lane == 2 * TOP_K + kk, rank, meta)
    meta_ref[...] = meta


def _outproj(sb, ret, xt, w_out_bf, ln_g, ln_b, w_router, b_router, alpha, tm=256):
    t, d = xt.shape
    tm = min(tm, t)
    sbw = sb.shape[1]
    wr = jnp.zeros((d, LANES), BF16).at[:, :N_EXPERTS].set(w_router.astype(BF16))
    br = jnp.full((1, LANES), NEG_BIG, F32).at[0, :N_EXPERTS].set(b_router.astype(F32))
    row = lambda w: pl.BlockSpec((tm, w), lambda i: (i, 0))
    const = lambda r, c: pl.BlockSpec((r, c), lambda i: (0, 0))
    return pl.pallas_call(
        functools.partial(_outproj_kernel, alpha=alpha, sb_width=sbw),
        grid=(t // tm,),
        in_specs=[row(sbw), row(ret.shape[1]), row(d), const(d, d), const(1, d), const(1, d),
                  const(d, LANES), const(1, LANES)],
        out_specs=[row(d), pl.BlockSpec((tm * (d // (2 * LANES)), LANES), lambda i: (i, 0)), row(LANES),
                   const(1, LANES)],
        out_shape=[jax.ShapeDtypeStruct((t, d), F32),
                   jax.ShapeDtypeStruct((t * (d // (2 * LANES)), LANES), jnp.uint32),
                   jax.ShapeDtypeStruct((t, LANES), F32),
                   jax.ShapeDtypeStruct((1, LANES), F32)],
        compiler_params=_cparams(("arbitrary",)),
        name="outproj_ln_router",
    )(sb, ret, xt, w_out_bf, ln_g.reshape(1, d).astype(F32), ln_b.reshape(1, d).astype(F32), wr, br)


DMA_UNROLL = 8


def _wait_slabs(hbm, buf, sem, slot, count, chunks):
    def body(n, _):
        pltpu.make_async_copy(hbm.at[pl.ds(0, chunks), :], buf.at[slot, pl.ds(0, chunks), :],
                              sem.at[slot]).wait()
        return 0
    lax.fori_loop(0, count, body, 0, unroll=DMA_UNROLL)


def _moe_kernel(ie_ref, ij_ref, ist_ref, ins_ref, tail_ref, live_ref, dest_ref, x_hbm, wgu_hbm, wd_hbm, bg_ref,
                bu_ref, bd_ref, y_hbm, x_vmem, acc, wg_f, wu_f, wd_f, wg_bf, wu_bf, wd_bf, stage, gbuf, tok_ref,
                sem_g, sem_out, sem_w, *, n_f):
    del ij_ref
    i = pl.program_id(0)
    j = pl.program_id(1)
    nsub = ins_ref[i]
    start = ist_ref[i]
    d = acc.shape[1]
    tf = wg_bf.shape[1]
    d_ff = n_f * tf
    chunks = d // LANES
    block_rows = ROW_BLOCK * chunks

    def weight_copies(e, jj, slot):
        col_g = pl.ds(pl.multiple_of(jj * tf, tf), tf)
        col_u = pl.ds(pl.multiple_of(d_ff + jj * tf, tf), tf)
        copies = []
        for half in range(2):
            rows_k = pl.ds(half * (d // 2), d // 2)
            rows_f = pl.ds(pl.multiple_of(jj * tf + half * (tf // 2), tf // 2), tf // 2)
            dst_f = pl.ds(half * (tf // 2), tf // 2)
            copies += [
                pltpu.make_async_copy(wgu_hbm.at[e, rows_k, col_g], wg_f.at[slot, rows_k, :], sem_w.at[slot]),
                pltpu.make_async_copy(wgu_hbm.at[e, rows_k, col_u], wu_f.at[slot, rows_k, :], sem_w.at[slot]),
                pltpu.make_async_copy(wd_hbm.at[e, rows_f, :], wd_f.at[slot, dst_f, :], sem_w.at[slot]),
            ]
        return copies

    def start_weights(e, jj, slot):
        for n, c in enumerate(weight_copies(e, jj, slot)):
            c.start(priority=n % 2)

    step = i * n_f + j
    w_slot = step % 2

    @pl.when(step == 0)
    def _():
        start_weights(ie_ref[0], 0, 0)

    @pl.when(step + 1 < live_ref[0])
    def _():
        last = j == n_f - 1
        start_weights(ie_ref[jnp.where(last, i + 1, i)], jnp.where(last, 0, j + 1), 1 - w_slot)

    def rows_of(r):
        return pl.ds(pl.multiple_of(r * ROW_BLOCK, ROW_BLOCK), ROW_BLOCK)

    def slabs_of(block):
        return pl.ds(pl.multiple_of(block * block_rows, block_rows), block_rows)

    def for_each(lo, hi, fn):
        def body(r, _):
            fn(r)
            return 0
        lax.fori_loop(lo, hi, body, 0)

    @pl.when((i == 0) & (j == 0))
    def _():
        stage[0] = jnp.zeros(stage.shape[1:], F32)

        def zero_copy(bk):
            return pltpu.make_async_copy(stage.at[0], y_hbm.at[slabs_of(bk), :], sem_out.at[0])

        n_blocks = y_hbm.shape[0] // block_rows
        for_each(tail_ref[0], n_blocks, lambda bk: zero_copy(bk).start())
        for_each(tail_ref[0], n_blocks, lambda bk: zero_copy(bk).wait())

    first_block = start // ROW_BLOCK
    words = d // (2 * LANES)

    pitch = words + 1

    def build_row_tokens():
        n_tok = dest_ref.shape[0] // TOP_K

        def clear(m, _):
            for u in range(DMA_UNROLL):
                tok_ref[m * DMA_UNROLL + u] = 0
            return 0
        lax.fori_loop(0, tok_ref.shape[0] // DMA_UNROLL, clear, 0)

        def fill(m, _):
            for u in range(DMA_UNROLL):
                t = m * DMA_UNROLL + u
                for kk in range(TOP_K):
                    tok_ref[dest_ref[kk * n_tok + t]] = t
            return 0
        lax.fori_loop(0, n_tok // DMA_UNROLL, fill, 0)

    def start_gather(item):
        row0 = ist_ref[item]

        def body(m, _):
            for u in range(DMA_UNROLL):
                r = m * DMA_UNROLL + u
                src = pl.ds(pl.multiple_of(tok_ref[row0 + r] * words, words), words)
                pltpu.make_async_copy(x_hbm.at[src, :], gbuf.at[pl.ds(r * pitch, words), :], sem_g).start(
                    priority=u % 2)
            return 0
        lax.fori_loop(0, ins_ref[item] * (ROW_BLOCK // DMA_UNROLL), body, 0)

    def wait_gather(item):
        def body(m, _):
            for _u in range(DMA_UNROLL):
                pltpu.make_async_copy(x_hbm.at[pl.ds(0, words), :], gbuf.at[pl.ds(0, words), :], sem_g).wait()
            return 0
        lax.fori_loop(0, ins_ref[item] * (ROW_BLOCK // DMA_UNROLL), body, 0)

    def out_copy(r, slot):
        return pltpu.make_async_copy(stage.at[slot], y_hbm.at[slabs_of(first_block + r), :],
                                     sem_out.at[slot])

    def span(r, n_blocks):
        return pl.ds(pl.multiple_of(r * ROW_BLOCK, ROW_BLOCK), n_blocks * ROW_BLOCK)

    def hidden(rows):
        xb = x_vmem[rows, :]
        gate = jnp.dot(xb, wg_bf[...], preferred_element_type=F32) + bg_ref[...]
        up = jnp.dot(xb, wu_bf[...], preferred_element_type=F32) + bu_ref[...]
        gate = jnp.minimum(gate, SWIGLU_LIMIT)
        up = jnp.clip(up, -SWIGLU_LIMIT, SWIGLU_LIMIT)
        act = (up + 1.0) * (gate * jax.nn.sigmoid(SWIGLU_ALPHA * gate))
        return jnp.dot(act.astype(BF16), wd_bf[...], preferred_element_type=F32)

    @pl.when(nsub > 0)
    def _():
        @pl.when(j == 0)
        def _():
            @pl.when(i == 0)
            def _():
                build_row_tokens()
                start_gather(0)
            wait_gather(i)
            bias_rows = jnp.broadcast_to(bd_ref[...], (ROW_BLOCK, d))

            def unpack(r):
                x_vmem[rows_of(r), :] = _from_packed_slabs(gbuf, r * (ROW_BLOCK * pitch), ROW_BLOCK, words, pitch)
                acc[rows_of(r), :] = bias_rows
            for_each(0, nsub, unpack)

            @pl.when(i + 1 < pl.num_programs(0))
            def _():
                start_gather(i + 1)

        for c in weight_copies(ie_ref[i], j, w_slot):
            c.wait()
        wg_bf[...] = wg_f[w_slot].astype(BF16)
        wu_bf[...] = wu_f[w_slot].astype(BF16)
        wd_bf[...] = wd_f[w_slot].astype(BF16)

        n_pairs = nsub // 2
        odd = nsub % 2 == 1

        @pl.when(j < n_f - 1)
        def _():
            def pair(p):
                acc[span(2 * p, 2), :] += hidden(span(2 * p, 2))
            for_each(0, n_pairs, pair)

            @pl.when(odd)
            def _():
                acc[span(nsub - 1, 1), :] += hidden(span(nsub - 1, 1))

        @pl.when(j == n_f - 1)
        def _():
            def write_out(r, slot, final, reuse):
                @pl.when(reuse)
                def _():
                    out_copy(r, slot).wait()
                _to_slabs(stage.at[slot], 0, ROW_BLOCK, final)
                out_copy(r, slot).start()

            def pair(p):
                final = acc[span(2 * p, 2), :] + hidden(span(2 * p, 2))
                write_out(2 * p, 0, final[:ROW_BLOCK], p > 0)
                write_out(2 * p + 1, 1, final[ROW_BLOCK:], p > 0)
            for_each(0, n_pairs, pair)

            @pl.when(odd)
            def _():
                final = acc[span(nsub - 1, 1), :] + hidden(span(nsub - 1, 1))
                write_out(nsub - 1, 0, final, n_pairs > 0)

            out_copy(0, 0).wait()

            @pl.when(n_pairs > 0)
            def _():
                out_copy(0, 1).wait()


def _moe_ffn(item_e, item_j, item_start, item_nsub, tail_block, live_steps, dest_flat, x1_packed, n_rows,
             w_gate_up, b_gate_up, w_down, b_down, r_max, tf=512):
    n_e, d, two_f = w_gate_up.shape
    chunks = d // LANES
    words = d // (2 * LANES)
    d_ff = two_f // 2
    n_f = d_ff // tf
    n_items = item_e.shape[0]

    def jf(i, j, ij):
        return jnp.where(ij[i] < 0, j, ij[i])

    hbm = pl.BlockSpec(memory_space=pl.ANY)
    in_specs = [
        hbm, hbm, hbm,
        pl.BlockSpec((None, 1, tf), lambda i, j, ie, ij, ist, ins, tl, lv, tk: (ie[i], 0, jf(i, j, ij))),
        pl.BlockSpec((None, 1, tf), lambda i, j, ie, ij, ist, ins, tl, lv, tk: (ie[i], 0, n_f + jf(i, j, ij))),
        pl.BlockSpec((None, 1, d), lambda i, j, ie, ij, ist, ins, tl, lv, tk: (ie[i], 0, 0)),
    ]
    return pl.pallas_call(
        functools.partial(_moe_kernel, n_f=n_f),
        grid_spec=pltpu.PrefetchScalarGridSpec(
            num_scalar_prefetch=7,
            grid=(n_items, n_f),
            in_specs=in_specs,
            out_specs=pl.BlockSpec(memory_space=pl.ANY),
            scratch_shapes=[pltpu.VMEM((r_max, d), BF16), pltpu.VMEM((r_max, d), F32),
                            pltpu.VMEM((2, d, tf), F32), pltpu.VMEM((2, d, tf), F32),
                            pltpu.VMEM((2, tf, d), F32),
                            pltpu.VMEM((d, tf), BF16), pltpu.VMEM((d, tf), BF16),
                            pltpu.VMEM((tf, d), BF16),
                            pltpu.VMEM((2, ROW_BLOCK * chunks, LANES), F32),
                            pltpu.VMEM((r_max * (words + 1), LANES), jnp.uint32),
                            pltpu.SMEM((n_rows,), jnp.int32),
                            pltpu.SemaphoreType.DMA(()), pltpu.SemaphoreType.DMA((2,)),
                            pltpu.SemaphoreType.DMA((2,))],
        ),
        out_shape=jax.ShapeDtypeStruct((n_rows * chunks, LANES), F32),
        compiler_params=_cparams(("arbitrary", "arbitrary")),
        name="moe_ffn",
    )(item_e, item_j, item_start, item_nsub, tail_block, live_steps, dest_flat, x1_packed, w_gate_up, w_down,
      b_gate_up.reshape(n_e, 1, two_f), b_gate_up.reshape(n_e, 1, two_f), b_down.reshape(n_e, 1, d))


def _combine_kernel(dest_ref, y_hbm, x1_ref, meta_ref, g_ref, b_ref, o_ref, buf, sem, *, alpha, tm, chunks):
    i = pl.program_id(0)
    per_step = tm * TOP_K
    n_tok = dest_ref.shape[0] // TOP_K
    pitch = chunks + 1

    def issue_tile(step, slot):
        def body(tt, _):
            for kk in range(TOP_K):
                src = pl.ds(pl.multiple_of(dest_ref[kk * n_tok + step * tm + tt] * chunks, chunks), chunks)
                dst = pl.ds((kk * tm + tt) * pitch, chunks)
                pltpu.make_async_copy(y_hbm.at[src, :], buf.at[slot, dst, :], sem.at[slot]).start(
                    priority=kk % 2)
            return 0
        lax.fori_loop(0, tm, body, 0, unroll=DMA_UNROLL // TOP_K)

    @pl.when(i == 0)
    def _():
        issue_tile(0, 0)

    @pl.when(i + 1 < pl.num_programs(0))
    def _():
        issue_tile(i + 1, (i + 1) % 2)

    slot = i % 2
    _wait_slabs(y_hbm, buf, sem, slot, per_step, chunks)
    meta = meta_ref[...]
    y = jnp.zeros(x1_ref.shape, F32)
    for kk in range(TOP_K):
        rows = jnp.concatenate(
            [buf[slot, pl.ds(kk * tm * pitch + c, tm, stride=pitch), :] for c in range(chunks)], axis=1)
        y = y + meta[:, TOP_K + kk:TOP_K + kk + 1] * rows
    o_ref[...] = _layer_norm(alpha * x1_ref[...] + y, g_ref[...], b_ref[...])


def _combine(dest_flat, y_rows, x1, meta, ln_g, ln_b, alpha, tm=128):
    t, d = x1.shape
    tm = min(tm, t)
    chunks = d // LANES
    row = lambda w: pl.BlockSpec((tm, w), lambda i, dest: (i, 0))
    const = pl.BlockSpec((1, d), lambda i, dest: (0, 0))
    return pl.pallas_call(
        functools.partial(_combine_kernel, alpha=alpha, tm=tm, chunks=chunks),
        grid_spec=pltpu.PrefetchScalarGridSpec(
            num_scalar_prefetch=1,
            grid=(t // tm,),
            in_specs=[pl.BlockSpec(memory_space=pl.ANY), row(d), row(LANES), const, const],
            out_specs=row(d),
            scratch_shapes=[pltpu.VMEM((2, TOP_K * tm * (chunks + 1), LANES), F32),
                            pltpu.SemaphoreType.DMA((2,))],
        ),
        out_shape=jax.ShapeDtypeStruct((t, d), F32),
        compiler_params=_cparams(("arbitrary",)),
        name="combine_ln",
    )(dest_flat, y_rows, x1, meta, ln_g.reshape(1, d).astype(F32), ln_b.reshape(1, d).astype(F32))


def _routing_plan(meta, counts_f, t, r_max):
    meta_t = meta.T
    idx = meta_t[0:TOP_K].astype(jnp.int32)
    rank = meta_t[2 * TOP_K:3 * TOP_K].astype(jnp.int32)
    counts = counts_f[0, :N_EXPERTS].astype(jnp.int32)
    n128 = (counts + ROW_BLOCK - 1) // ROW_BLOCK
    padded = n128 * ROW_BLOCK
    pad_start = jnp.cumsum(padded) - padded
    experts = jnp.arange(N_EXPERTS, dtype=jnp.int32)[:, None, None]
    region = jnp.sum(jnp.where(idx[None] == experts, pad_start[:, None, None], 0), axis=0)
    dest = region + rank
    tk = t * TOP_K
    n_rows = (tk + ROW_BLOCK - 1) // ROW_BLOCK * ROW_BLOCK + N_EXPERTS * ROW_BLOCK

    subs = r_max // ROW_BLOCK
    n_items = N_EXPERTS + (n_rows // ROW_BLOCK - N_EXPERTS) // subs
    items_e = (n128 + subs - 1) // subs
    items_end = jnp.cumsum(items_e)
    total = items_end[-1]
    slot = jnp.arange(n_items, dtype=jnp.int32)
    live = slot < total
    s_eff = jnp.minimum(slot, total - 1)
    e = jnp.minimum(jnp.searchsorted(items_end, s_eff, side='right'), N_EXPERTS - 1).astype(jnp.int32)
    local = s_eff - (items_end[e] - items_e[e])
    item_start = (pad_start[e] + local * r_max).astype(jnp.int32)
    item_nsub = jnp.where(live, jnp.clip(n128[e] - local * subs, 0, subs), 0).astype(jnp.int32)
    tail_block = jnp.sum(n128).reshape(1).astype(jnp.int32)
    return dest.reshape(tk), n_rows, e, live, item_start, item_nsub, tail_block


def kernel(x, w_in, ret_gn_gain, w_out, ln1_gain, ln1_bias, w_router, b_router, w_gate_up, b_gate_up,
           w_down, b_down, ln2_gain, ln2_bias):
    b, s, d = x.shape
    t = b * s
    depth = w_in.shape[0]
    alpha = (2 * depth) ** 0.25
    r_max = 1280
    tf = 512
    n_f = (w_gate_up.shape[-1] // 2) // tf
    xt = x.reshape(t, d)
    for layer in range(depth):
        proj = _in_proj(xt, w_in[layer].astype(BF16))
        sb = _stickbreak(proj, b, s)
        ret = _retention(proj, ret_gn_gain[layer], b, s)
        x1, x1_slabs, meta, counts = _outproj(sb, ret, xt, w_out[layer].astype(BF16), ln1_gain[layer],
                                              ln1_bias[layer], w_router[layer], b_router[layer], alpha)
        dest_flat, n_rows, item_e, live, item_start, item_nsub, tail_block = _routing_plan(
            meta, counts, t, r_max)
        item_j = jnp.where(live, -1, n_f - 1).astype(jnp.int32)
        live_steps = (jnp.sum(live.astype(jnp.int32)) * n_f).reshape(1)
        y_rows = _moe_ffn(item_e, item_j, item_start, item_nsub, tail_block, live_steps, dest_flat, x1_slabs,
                          n_rows, w_gate_up[layer], b_gate_up[layer], w_down[layer], b_down[layer], r_max, tf)
        xt = _combine(dest_flat, y_rows, x1, meta, ln2_gain[layer], ln2_bias[layer], alpha)
    return xt.reshape(b, s, d)
```

```python
import functools
import math

import jax
import jax.numpy as jnp
from jax import lax
from jax.experimental import pallas as pl
from jax.experimental.pallas import tpu as pltpu

F32 = jnp.float32
BF16 = jnp.bfloat16

HEAD_DIM = 128
SB_HEADS = 8
RET_HEADS = 8
CHUNK = 64
ROPE_BASE = 10000.0
N_EXPERTS = 32
TOP_K = 4
SWIGLU_LIMIT = 7.0
SWIGLU_ALPHA = 1.702
LN_EPS = 1e-5
GN_EPS = 1e-5

V7X_VMEM_LIMIT_BYTES = 56 * 1024 * 1024
LANES = 128
ROW_BLOCK = 128
NEG_BIG = -1e30
EXP_UNDERFLOW = -105.0


def _cparams(sem, vmem=V7X_VMEM_LIMIT_BYTES):
    return pltpu.CompilerParams(dimension_semantics=sem, vmem_limit_bytes=vmem)


def _matmul_kernel(x_ref, w_ref, o_ref, x_bf):
    @pl.when(pl.program_id(1) == 0)
    def _():
        x_bf[...] = x_ref[...].astype(BF16)
    o_ref[...] = jnp.dot(x_bf[...], w_ref[...], preferred_element_type=F32).astype(o_ref.dtype)


def _in_proj(x, w_bf, tm=1024, tn=1024):
    t, d = x.shape
    n = w_bf.shape[1]
    tm = min(tm, t)
    return pl.pallas_call(
        _matmul_kernel,
        grid=(t // tm, n // tn),
        in_specs=[pl.BlockSpec((tm, d), lambda i, j: (i, 0)),
                  pl.BlockSpec((d, tn), lambda i, j: (0, j))],
        out_specs=pl.BlockSpec((tm, tn), lambda i, j: (i, j)),
        out_shape=jax.ShapeDtypeStruct((t, n), BF16),
        scratch_shapes=[pltpu.VMEM((tm, d), BF16)],
        compiler_params=_cparams(("arbitrary", "arbitrary")),
        name="in_proj",
    )(x, w_bf)


def _sb_kernel(q_ref, k_ref, v_ref, o_ref, *, seq, scale, group, sub):
    blk = 128
    nq = seq // blk
    row = lax.broadcasted_iota(jnp.int32, (blk, blk), 0)
    col = lax.broadcasted_iota(jnp.int32, (blk, blk), 1)
    causal = col < row
    r2 = lax.broadcasted_iota(jnp.int32, (2 * blk, 2 * blk), 0)
    c2 = lax.broadcasted_iota(jnp.int32, (2 * blk, 2 * blk), 1)
    cum_rhs = jnp.where((c2 >= blk) | ((r2 & (blk - 1)) >= c2), 1.0, 0.0).astype(BF16)

    n_sub = group // sub
    causal_sub = jnp.concatenate([causal] * sub, axis=0)
    heads_of = lambda s: range(s * sub, (s + 1) * sub)
    cols = lambda g: slice(g * blk, (g + 1) * blk)

    def block(qbs, kb, carries, accs, masked):
        ks = pl.ds(pl.multiple_of(kb * blk, blk), blk)
        zs = [jnp.concatenate(
            [lax.dot_general(qbs[g], k_ref[ks, cols(g)], (((1,), (1,)), ((), ())),
                             preferred_element_type=F32) for g in heads_of(s)], axis=0) * scale
              for s in range(n_sub)]
        sums = []
        for z in zs:
            neg_z = -z
            lnb = jnp.minimum(neg_z, 0.0) - jnp.log(1.0 + jnp.exp(jnp.minimum(z, neg_z)))
            if masked:
                lnb = jnp.where(causal_sub, lnb, 0.0)
            hi = lnb.astype(BF16)
            lo = (lnb - hi.astype(F32)).astype(BF16)
            sums.append(jnp.dot(jnp.concatenate([hi, lo], axis=1), cum_rhs, preferred_element_type=F32))
        new_carries, new_accs = [], []
        for s in range(n_sub):
            incl = sums[s][:, :blk]
            total = sums[s][:, blk:]
            w = jnp.exp(zs[s] + incl + carries[s])
            if masked:
                w = jnp.where(causal_sub, w, 0.0)
            w = w.astype(BF16)
            for n, g in enumerate(heads_of(s)):
                new_accs.append(accs[g] + jnp.dot(w[n * blk:(n + 1) * blk], v_ref[ks, cols(g)],
                                                  preferred_element_type=F32))
            new_carries.append(carries[s] + total)
        return tuple(new_carries), tuple(new_accs)

    def q_body(qi, _):
        qs = pl.ds(pl.multiple_of(qi * blk, blk), blk)
        qbs = [q_ref[qs, cols(g)] for g in range(group)]
        state = block(qbs, qi, (jnp.zeros((sub * blk, blk), F32),) * n_sub,
                      (jnp.zeros((blk, blk), F32),) * group, True)

        def some_weight_left(carries):
            top = functools.reduce(jnp.maximum, carries)
            return (jnp.max(top) > EXP_UNDERFLOW).astype(jnp.int32)

        def live(st):
            t, more, _, _ = st
            return jnp.logical_and(t < qi, more > 0)

        def kb_body(st):
            t, _, carries, accs = st
            carries, accs = block(qbs, qi - 1 - t, carries, accs, False)
            return t + 1, some_weight_left(carries), carries, accs

        _, _, _, accs = lax.while_loop(live, kb_body, (0, 1, state[0], state[1]))
        for g in range(group):
            o_ref[qs, cols(g)] = accs[g].astype(o_ref.dtype)
        return 0

    lax.fori_loop(0, nq, q_body, 0)


def _stickbreak(proj, batch, seq, group=8, sub=4):
    t = batch * seq
    h = SB_HEADS
    n_groups = h // group
    width = group * HEAD_DIM
    kern = functools.partial(_sb_kernel, seq=seq, scale=1.0 / math.sqrt(HEAD_DIM), group=group,
                             sub=sub)
    spec = lambda off: pl.BlockSpec((seq, width), lambda b, hg: (b, off * n_groups + hg))
    return pl.pallas_call(
        kern,
        grid=(batch, n_groups),
        in_specs=[spec(0), spec(1), spec(2)],
        out_specs=pl.BlockSpec((seq, width), lambda b, hg: (b, hg)),
        out_shape=jax.ShapeDtypeStruct((t, h * HEAD_DIM), BF16),
        compiler_params=_cparams(("arbitrary", "arbitrary")),
        name="stickbreak",
    )(proj, proj, proj)


def _ret_kernel(q_ref, k_ref, v_ref, g_ref, cos_ref, sin_ref, intra_ref, qdec_ref, kdec_ref,
                cdec_ref, gain_ref, o_ref, *, seq, heads):
    n_chunks = seq // CHUNK
    half = HEAD_DIM // 2
    k_scale = HEAD_DIM ** -0.5
    cols = lambda h: slice(h * HEAD_DIM, (h + 1) * HEAD_DIM)
    contract_last = (((1,), (1,)), ((), ()))
    contract_rows = (((0,), (0,)), ((), ()))

    def chunk(n, states):
        rs = pl.ds(pl.multiple_of(n * CHUNK, CHUNK), CHUNK)
        cos = cos_ref[rs, :]
        sin = sin_ref[rs, :]
        qrs, krs, crosses, kvs = [], [], [], []
        for h in range(heads):
            q = q_ref[rs, cols(h)].astype(F32)
            k = k_ref[rs, cols(h)].astype(F32)
            qr = q * cos + pltpu.roll(q, half, 1) * sin
            kr = (k * cos + pltpu.roll(k, half, 1) * sin) * k_scale
            qrs.append(qr.astype(BF16))
            krs.append(kr.astype(BF16))
            crosses.append(jnp.dot((qr * qdec_ref[h]).astype(BF16), states[h].astype(BF16),
                                   preferred_element_type=F32))
            kvs.append(lax.dot_general((kr * kdec_ref[h]).astype(BF16), v_ref[rs, cols(h)], contract_rows,
                                       preferred_element_type=F32))
        scores = [lax.dot_general(qrs[h], krs[h], contract_last, preferred_element_type=F32) * intra_ref[h]
                  for h in range(heads)]
        outs = [crosses[h] + jnp.dot(scores[h].astype(BF16), v_ref[rs, cols(h)], preferred_element_type=F32)
                for h in range(heads)]
        new_states = []
        for h in range(heads):
            o = outs[h]
            mu = jnp.mean(o, axis=-1, keepdims=True)
            var = jnp.mean(jnp.square(o - mu), axis=-1, keepdims=True)
            on = (o - mu) * lax.rsqrt(var + GN_EPS)
            g = g_ref[rs, cols(h)].astype(F32)
            out = on * gain_ref[:, cols(h)] * (g * jax.nn.sigmoid(g))
            o_ref[rs, cols(h)] = out.astype(o_ref.dtype)
            new_states.append(states[h] * cdec_ref[h] + kvs[h])
        return tuple(new_states)

    lax.fori_loop(0, n_chunks, chunk, (jnp.zeros((HEAD_DIM, HEAD_DIM), F32),) * heads)


def _retention_tables(seq):
    d = HEAD_DIM
    inv_freq = ROPE_BASE ** (-jnp.arange(0, d, 2, dtype=F32) / d)
    ang = jnp.arange(seq, dtype=F32)[:, None] * inv_freq[None, :]
    cos, sin = jnp.cos(ang), jnp.sin(ang)
    cos_full = jnp.concatenate([cos, cos], axis=-1)
    sin_signed = jnp.concatenate([-sin, sin], axis=-1)
    log_gamma = jnp.log1p(-jnp.exp2(-5.0 - jnp.arange(RET_HEADS, dtype=F32)))
    i = jnp.arange(CHUNK, dtype=F32)
    intra = jnp.exp(log_gamma[:, None, None] * jnp.abs(i[:, None] - i[None, :]))
    k_decay = jnp.exp(log_gamma[:, None] * (CHUNK - 1 - i))
    q_decay = jnp.exp(log_gamma[:, None] * (i + 1.0))
    c_decay = jnp.exp(log_gamma * CHUNK)
    bc = lambda a: jnp.broadcast_to(a[..., None], a.shape + (d,))
    return cos_full, sin_signed, intra, bc(q_decay), bc(k_decay), bc(c_decay[:, None])


def _retention(proj, gn_gain, batch, seq):
    t = batch * seq
    h = RET_HEADS
    base = 3 * SB_HEADS
    cos_full, sin_signed, intra, qdec, kdec, cdec = _retention_tables(seq)
    width = h * HEAD_DIM
    group0 = base // h
    spec = lambda off: pl.BlockSpec((seq, width), lambda b: (b, group0 + off))
    full = pl.BlockSpec((seq, HEAD_DIM), lambda b: (0, 0))
    table = lambda r, c: pl.BlockSpec((h, r, c), lambda b: (0, 0, 0))
    return pl.pallas_call(
        functools.partial(_ret_kernel, seq=seq, heads=h),
        grid=(batch,),
        in_specs=[spec(0), spec(1), spec(2), spec(3), full, full,
                  table(CHUNK, CHUNK), table(CHUNK, HEAD_DIM), table(CHUNK, HEAD_DIM),
                  table(1, HEAD_DIM),
                  pl.BlockSpec((1, width), lambda b: (0, 0))],
        out_specs=pl.BlockSpec((seq, width), lambda b: (b, 0)),
        out_shape=jax.ShapeDtypeStruct((t, width), BF16),
        compiler_params=_cparams(("arbitrary",)),
        name="retention",
    )(proj, proj, proj, proj, cos_full, sin_signed, intra, qdec, kdec, cdec,
      gn_gain.reshape(1, h * HEAD_DIM).astype(F32))


def _layer_norm(hid, gain, bias):
    mu = jnp.mean(hid, axis=-1, keepdims=True)
    cen = hid - mu
    var = jnp.mean(jnp.square(cen), axis=-1, keepdims=True)
    return cen * lax.rsqrt(var + LN_EPS) * gain + bias


SLAB_PAD = 4


def _to_slabs(slab_ref, base, rows, value, pitch):
    for c in range(value.shape[1] // LANES):
        slab_ref[pl.ds(base + c, rows, stride=pitch), :] = value[:, c * LANES:(c + 1) * LANES]


def _bf16_bits(x):
    u = pltpu.bitcast(x, jnp.uint32)
    return lax.shift_right_logical(u + jnp.uint32(0x7FFF) + (lax.shift_right_logical(u, jnp.uint32(16))
                                                             & jnp.uint32(1)), jnp.uint32(16))


def _to_packed_slabs(slab_ref, rows, value):
    words = value.shape[1] // (2 * LANES)
    for c in range(words):
        lo = _bf16_bits(value[:, (2 * c) * LANES:(2 * c + 1) * LANES])
        hi = _bf16_bits(value[:, (2 * c + 1) * LANES:(2 * c + 2) * LANES])
        slab_ref[pl.ds(c, rows, stride=words), :] = lo | lax.shift_left(hi, jnp.uint32(16))


def _from_packed_slabs(slab_ref, base, rows, words, pitch):
    out = []
    for c in range(words):
        w = slab_ref[pl.ds(base + c, rows, stride=pitch), :]
        out.append(pltpu.bitcast(lax.shift_left(w, jnp.uint32(16)), F32).astype(BF16))
        out.append(pltpu.bitcast(w & jnp.uint32(0xFFFF0000), F32).astype(BF16))
    return jnp.concatenate(out, axis=1)


def _outproj_kernel(sb_ref, ret_ref, x_ref, w_ref, g_ref, b_ref, wr_ref, br_ref,
                    x1_ref, x1s_ref, meta_ref, cnt_ref, *, alpha, sb_width):
    tm = x_ref.shape[0]
    mix = jnp.dot(sb_ref[...], w_ref[:sb_width, :], preferred_element_type=F32)
    mix = mix + jnp.dot(ret_ref[...], w_ref[sb_width:, :], preferred_element_type=F32)
    x1 = _layer_norm(alpha * x_ref[...] + mix, g_ref[...], b_ref[...])
    x1_ref[...] = x1
    _to_packed_slabs(x1s_ref, tm, x1)

    logits = jnp.dot(x1.astype(BF16), wr_ref[...], preferred_element_type=F32) + br_ref[...]
    lane = lax.broadcasted_iota(jnp.int32, (tm, LANES), 1).astype(F32)
    vals = logits
    tops, idxs, hots = [], [], []
    for _ in range(TOP_K):
        m = jnp.max(vals, axis=-1, keepdims=True)
        idx = jnp.min(jnp.where(vals == m, lane, float(LANES)), axis=-1, keepdims=True)
        hot = lane == idx
        vals = jnp.where(hot, NEG_BIG * 2.0, vals)
        tops.append(m)
        idxs.append(idx)
        hots.append(hot)
    exps = [jnp.exp(m - tops[0]) for m in tops]
    denom = exps[0] + exps[1] + exps[2] + exps[3]
    gates = [e / denom for e in exps]

    @pl.when(pl.program_id(0) == 0)
    def _():
        cnt_ref[...] = jnp.zeros_like(cnt_ref)

    multi = jnp.zeros((tm, LANES), F32)
    for hot in hots:
        multi = multi + jnp.where(hot, 1.0, 0.0)
    r = lax.broadcasted_iota(jnp.int32, (tm, tm), 0)
    c = lax.broadcasted_iota(jnp.int32, (tm, tm), 1)
    strict_lower = jnp.where(c < r, 1.0, 0.0).astype(BF16)
    before = jnp.dot(strict_lower, multi.astype(BF16), preferred_element_type=F32) + cnt_ref[...]
    cnt_ref[...] = cnt_ref[...] + jnp.sum(multi, axis=0, keepdims=True)

    meta = jnp.zeros((tm, LANES), F32)
    for kk in range(TOP_K):
        rank = jnp.sum(jnp.where(hots[kk], before, 0.0), axis=-1, keepdims=True)
        meta = jnp.where(lane == kk, idxs[kk], meta)
        meta = jnp.where(lane == TOP_K + kk, gates[kk], meta)
        meta = jnp.where(lane == 2 * TOP_K + kk, rank, meta)
    meta_ref[...] = meta


def _outproj(sb, ret, xt, w_out_bf, ln_g, ln_b, w_router, b_router, alpha, tm=256):
    t, d = xt.shape
    tm = min(tm, t)
    sbw = sb.shape[1]
    wr = jnp.zeros((d, LANES), BF16).at[:, :N_EXPERTS].set(w_router.astype(BF16))
    br = jnp.full((1, LANES), NEG_BIG, F32).at[0, :N_EXPERTS].set(b_router.astype(F32))
    row = lambda w: pl.BlockSpec((tm, w), lambda i: (i, 0))
    const = lambda r, c: pl.BlockSpec((r, c), lambda i: (0, 0))
    return pl.pallas_call(
        functools.partial(_outproj_kernel, alpha=alpha, sb_width=sbw),
        grid=(t // tm,),
        in_specs=[row(sbw), row(ret.shape[1]), row(d), const(d, d), const(1, d), const(1, d),
                  const(d, LANES), const(1, LANES)],
        out_specs=[row(d), pl.BlockSpec((tm * (d // (2 * LANES)), LANES), lambda i: (i, 0)), row(LANES),
                   const(1, LANES)],
        out_shape=[jax.ShapeDtypeStruct((t, d), F32),
                   jax.ShapeDtypeStruct((t * (d // (2 * LANES)), LANES), jnp.uint32),
                   jax.ShapeDtypeStruct((t, LANES), F32),
                   jax.ShapeDtypeStruct((1, LANES), F32)],
        compiler_params=_cparams(("arbitrary",)),
        name="outproj_ln_router",
    )(sb, ret, xt, w_out_bf, ln_g.reshape(1, d).astype(F32), ln_b.reshape(1, d).astype(F32), wr, br)


DMA_UNROLL = 8


def _wait_slabs(hbm, buf, sem, slot, count, chunks):
    def body(n, _):
        pltpu.make_async_copy(hbm.at[pl.ds(0, chunks), :], buf.at[slot, pl.ds(0, chunks), :],
                              sem.at[slot]).wait()
        return 0
    lax.fori_loop(0, count, body, 0, unroll=DMA_UNROLL)


def _moe_kernel(ie_ref, ij_ref, ist_ref, ins_ref, tail_ref, live_ref, dest_ref, x_hbm, wgu_hbm, wd_hbm, bg_ref,
                bu_ref, bd_ref, zeros_hbm, y_hbm, x_vmem, acc, wg_f, wu_f, wd_f, wgu_bf, wd_bf, stage, gbuf, tok_ref,
                sem_g, sem_out, sem_w, *, n_f):
    del ij_ref
    i = pl.program_id(0)
    j = pl.program_id(1)
    nsub = ins_ref[i]
    start = ist_ref[i]
    d = acc.shape[1]
    tf = wd_bf.shape[0]
    d_ff = n_f * tf
    y_pitch = d // LANES + SLAB_PAD
    block_rows = ROW_BLOCK * y_pitch

    def weight_copies(e, jj, slot):
        col_g = pl.ds(pl.multiple_of(jj * tf, tf), tf)
        col_u = pl.ds(pl.multiple_of(d_ff + jj * tf, tf), tf)
        copies = []
        for half in range(2):
            rows_k = pl.ds(half * (d // 2), d // 2)
            rows_f = pl.ds(pl.multiple_of(jj * tf + half * (tf // 2), tf // 2), tf // 2)
            dst_f = pl.ds(half * (tf // 2), tf // 2)
            copies += [
                pltpu.make_async_copy(wgu_hbm.at[e, rows_k, col_g], wg_f.at[slot, rows_k, :], sem_w.at[slot]),
                pltpu.make_async_copy(wgu_hbm.at[e, rows_k, col_u], wu_f.at[slot, rows_k, :], sem_w.at[slot]),
                pltpu.make_async_copy(wd_hbm.at[e, rows_f, :], wd_f.at[slot, dst_f, :], sem_w.at[slot]),
            ]
        return copies

    def start_weights(e, jj, slot):
        for n, c in enumerate(weight_copies(e, jj, slot)):
            c.start(priority=n % 2)

    step = i * n_f + j
    w_slot = step % 2

    @pl.when(step == 0)
    def _():
        start_weights(ie_ref[0], 0, 0)

    @pl.when(step + 1 < live_ref[0])
    def _():
        last = j == n_f - 1
        start_weights(ie_ref[jnp.where(last, i + 1, i)], jnp.where(last, 0, j + 1), 1 - w_slot)

    def rows_of(r):
        return pl.ds(pl.multiple_of(r * ROW_BLOCK, ROW_BLOCK), ROW_BLOCK)

    def slabs_of(block):
        return pl.ds(pl.multiple_of(block * block_rows, block_rows), block_rows)

    def for_each(lo, hi, fn):
        def body(r, _):
            fn(r)
            return 0
        lax.fori_loop(lo, hi, body, 0)

    @pl.when((i == 0) & (j == 0))
    def _():
        stage[...] = jnp.zeros(stage.shape, F32)

        def zero_copy(bk):
            return pltpu.make_async_copy(stage.at[0], y_hbm.at[slabs_of(bk), :], sem_out.at[0])

        n_blocks = y_hbm.shape[0] // block_rows
        for_each(tail_ref[0], n_blocks, lambda bk: zero_copy(bk).start())
        for_each(tail_ref[0], n_blocks, lambda bk: zero_copy(bk).wait())

    first_block = start // ROW_BLOCK
    words = d // (2 * LANES)

    pitch = words + 1

    def build_row_tokens():
        n_tok = dest_ref.shape[0] // TOP_K
        clear = pltpu.make_async_copy(zeros_hbm, tok_ref, sem_g)
        clear.start()
        clear.wait()

        def fill(m, _):
            for u in range(DMA_UNROLL):
                t = m * DMA_UNROLL + u
                for kk in range(TOP_K):
                    tok_ref[dest_ref[kk * n_tok + t]] = t
            return 0
        lax.fori_loop(0, n_tok // DMA_UNROLL, fill, 0)

    def start_gather(item):
        row0 = ist_ref[item]

        def body(m, _):
            for u in range(DMA_UNROLL):
                r = m * DMA_UNROLL + u
                src = pl.ds(pl.multiple_of(tok_ref[row0 + r] * words, words), words)
                pltpu.make_async_copy(x_hbm.at[src, :], gbuf.at[pl.ds(r * pitch, words), :], sem_g).start(
                    priority=u % 2)
            return 0
        lax.fori_loop(0, ins_ref[item] * (ROW_BLOCK // DMA_UNROLL), body, 0)

    def wait_gather(item):
        def body(m, _):
            for _u in range(DMA_UNROLL):
                pltpu.make_async_copy(x_hbm.at[pl.ds(0, words), :], gbuf.at[pl.ds(0, words), :], sem_g).wait()
            return 0
        lax.fori_loop(0, ins_ref[item] * (ROW_BLOCK // DMA_UNROLL), body, 0)

    def out_copy(r, slot):
        return pltpu.make_async_copy(stage.at[slot], y_hbm.at[slabs_of(first_block + r), :],
                                     sem_out.at[slot])

    def span(r, n_blocks):
        return pl.ds(pl.multiple_of(r * ROW_BLOCK, ROW_BLOCK), n_blocks * ROW_BLOCK)

    def hidden(rows):
        xb = x_vmem[rows, :]
        gate_up = jnp.dot(xb, wgu_bf[...], preferred_element_type=F32)
        gate = gate_up[:, :tf] + bg_ref[...]
        up = gate_up[:, tf:] + bu_ref[...]
        gate = jnp.minimum(gate, SWIGLU_LIMIT)
        up = jnp.clip(up, -SWIGLU_LIMIT, SWIGLU_LIMIT)
        act = (up + 1.0) * (gate * jax.nn.sigmoid(SWIGLU_ALPHA * gate))
        return jnp.dot(act.astype(BF16), wd_bf[...], preferred_element_type=F32)

    @pl.when(nsub > 0)
    def _():
        @pl.when(j == 0)
        def _():
            @pl.when(i == 0)
            def _():
                build_row_tokens()
                start_gather(0)
            wait_gather(i)
            bias_rows = jnp.broadcast_to(bd_ref[...], (ROW_BLOCK, d))

            def unpack(r):
                x_vmem[rows_of(r), :] = _from_packed_slabs(gbuf, r * (ROW_BLOCK * pitch), ROW_BLOCK, words, pitch)
                acc[rows_of(r), :] = bias_rows
            for_each(0, nsub, unpack)

            @pl.when(i + 1 < pl.num_programs(0))
            def _():
                start_gather(i + 1)

        for c in weight_copies(ie_ref[i], j, w_slot):
            c.wait()
        wgu_bf[:, :tf] = wg_f[w_slot].astype(BF16)
        wgu_bf[:, tf:] = wu_f[w_slot].astype(BF16)
        wd_bf[...] = wd_f[w_slot].astype(BF16)

        n_pairs = nsub // 2
        odd = nsub % 2 == 1

        @pl.when(j < n_f - 1)
        def _():
            def pair(p):
                acc[span(2 * p, 2), :] += hidden(span(2 * p, 2))
            for_each(0, n_pairs, pair)

            @pl.when(odd)
            def _():
                acc[span(nsub - 1, 1), :] += hidden(span(nsub - 1, 1))

        @pl.when(j == n_f - 1)
        def _():
            def write_out(r, slot, final, reuse):
                @pl.when(reuse)
                def _():
                    out_copy(r, slot).wait()
                _to_slabs(stage.at[slot], 0, ROW_BLOCK, final, y_pitch)
                out_copy(r, slot).start()

            def pair(p):
                final = acc[span(2 * p, 2), :] + hidden(span(2 * p, 2))
                write_out(2 * p, 0, final[:ROW_BLOCK], p > 0)
                write_out(2 * p + 1, 1, final[ROW_BLOCK:], p > 0)
            for_each(0, n_pairs, pair)

            @pl.when(odd)
            def _():
                final = acc[span(nsub - 1, 1), :] + hidden(span(nsub - 1, 1))
                write_out(nsub - 1, 0, final, n_pairs > 0)

            out_copy(0, 0).wait()

            @pl.when(n_pairs > 0)
            def _():
                out_copy(0, 1).wait()


def _moe_ffn(item_e, item_j, item_start, item_nsub, tail_block, live_steps, dest_flat, x1_packed, n_rows,
             w_gate_up, b_gate_up, w_down, b_down, r_max, tf=512):
    n_e, d, two_f = w_gate_up.shape
    chunks = d // LANES
    words = d // (2 * LANES)
    d_ff = two_f // 2
    n_f = d_ff // tf
    n_items = item_e.shape[0]

    def jf(i, j, ij):
        return jnp.where(ij[i] < 0, j, ij[i])

    hbm = pl.BlockSpec(memory_space=pl.ANY)
    in_specs = [
        hbm, hbm, hbm,
        pl.BlockSpec((None, 1, tf), lambda i, j, ie, ij, ist, ins, tl, lv, tk: (ie[i], 0, jf(i, j, ij))),
        pl.BlockSpec((None, 1, tf), lambda i, j, ie, ij, ist, ins, tl, lv, tk: (ie[i], 0, n_f + jf(i, j, ij))),
        pl.BlockSpec((None, 1, d), lambda i, j, ie, ij, ist, ins, tl, lv, tk: (ie[i], 0, 0)),
        hbm,
    ]
    return pl.pallas_call(
        functools.partial(_moe_kernel, n_f=n_f),
        grid_spec=pltpu.PrefetchScalarGridSpec(
            num_scalar_prefetch=7,
            grid=(n_items, n_f),
            in_specs=in_specs,
            out_specs=pl.BlockSpec(memory_space=pl.ANY),
            scratch_shapes=[pltpu.VMEM((r_max, d), BF16), pltpu.VMEM((r_max, d), F32),
                            pltpu.VMEM((2, d, tf), F32), pltpu.VMEM((2, d, tf), F32),
                            pltpu.VMEM((2, tf, d), F32),
                            pltpu.VMEM((d, 2 * tf), BF16), pltpu.VMEM((tf, d), BF16),
                            pltpu.VMEM((2, ROW_BLOCK * (chunks + SLAB_PAD), LANES), F32),
                            pltpu.VMEM((r_max * (words + 1), LANES), jnp.uint32),
                            pltpu.SMEM((n_rows,), jnp.int32),
                            pltpu.SemaphoreType.DMA(()), pltpu.SemaphoreType.DMA((2,)),
                            pltpu.SemaphoreType.DMA((2,))],
        ),
        out_shape=jax.ShapeDtypeStruct((n_rows * (chunks + SLAB_PAD), LANES), F32),
        compiler_params=_cparams(("arbitrary", "arbitrary")),
        name="moe_ffn",
    )(item_e, item_j, item_start, item_nsub, tail_block, live_steps, dest_flat, x1_packed, w_gate_up, w_down,
      b_gate_up.reshape(n_e, 1, two_f), b_gate_up.reshape(n_e, 1, two_f), b_down.reshape(n_e, 1, d),
      jnp.zeros((n_rows,), jnp.int32))


def _combine_kernel(dest_ref, y_hbm, x1_ref, meta_ref, g_ref, b_ref, o_ref, buf, sem, *, alpha, tm, chunks):
    i = pl.program_id(0)
    per_step = tm * TOP_K
    n_tok = dest_ref.shape[0] // TOP_K
    pitch = chunks + 1

    def issue_tile(step, slot):
        def body(tt, _):
            for kk in range(TOP_K):
                src = pl.ds(dest_ref[kk * n_tok + step * tm + tt] * (chunks + SLAB_PAD), chunks)
                dst = pl.ds((kk * tm + tt) * pitch, chunks)
                pltpu.make_async_copy(y_hbm.at[src, :], buf.at[slot, dst, :], sem.at[slot]).start(
                    priority=kk % 2)
            return 0
        lax.fori_loop(0, tm, body, 0, unroll=DMA_UNROLL // TOP_K)

    @pl.when(i == 0)
    def _():
        issue_tile(0, 0)

    @pl.when(i + 1 < pl.num_programs(0))
    def _():
        issue_tile(i + 1, (i + 1) % 2)

    slot = i % 2
    _wait_slabs(y_hbm, buf, sem, slot, per_step, chunks)
    meta = meta_ref[...]
    y = jnp.zeros(x1_ref.shape, F32)
    for kk in range(TOP_K):
        rows = jnp.concatenate(
            [buf[slot, pl.ds(kk * tm * pitch + c, tm, stride=pitch), :] for c in range(chunks)], axis=1)
        y = y + meta[:, TOP_K + kk:TOP_K + kk + 1] * rows
    o_ref[...] = _layer_norm(alpha * x1_ref[...] + y, g_ref[...], b_ref[...])


def _combine(dest_flat, y_rows, x1, meta, ln_g, ln_b, alpha, tm=128):
    t, d = x1.shape
    tm = min(tm, t)
    chunks = d // LANES
    row = lambda w: pl.BlockSpec((tm, w), lambda i, dest: (i, 0))
    const = pl.BlockSpec((1, d), lambda i, dest: (0, 0))
    return pl.pallas_call(
        functools.partial(_combine_kernel, alpha=alpha, tm=tm, chunks=chunks),
        grid_spec=pltpu.PrefetchScalarGridSpec(
            num_scalar_prefetch=1,
            grid=(t // tm,),
            in_specs=[pl.BlockSpec(memory_space=pl.ANY), row(d), row(LANES), const, const],
            out_specs=row(d),
            scratch_shapes=[pltpu.VMEM((2, TOP_K * tm * (chunks + 1), LANES), F32),
                            pltpu.SemaphoreType.DMA((2,))],
        ),
        out_shape=jax.ShapeDtypeStruct((t, d), F32),
        compiler_params=_cparams(("arbitrary",)),
        name="combine_ln",
    )(dest_flat, y_rows, x1, meta, ln_g.reshape(1, d).astype(F32), ln_b.reshape(1, d).astype(F32))


def _routing_plan(meta, counts_f, t, r_max):
    meta_t = meta.T
    idx = meta_t[0:TOP_K].astype(jnp.int32)
    rank = meta_t[2 * TOP_K:3 * TOP_K].astype(jnp.int32)
    counts = counts_f[0, :N_EXPERTS].astype(jnp.int32)
    n128 = (counts + ROW_BLOCK - 1) // ROW_BLOCK
    padded = n128 * ROW_BLOCK
    pad_start = jnp.cumsum(padded) - padded
    experts = jnp.arange(N_EXPERTS, dtype=jnp.int32)[:, None, None]
    region = jnp.sum(jnp.where(idx[None] == experts, pad_start[:, None, None], 0), axis=0)
    dest = region + rank
    tk = t * TOP_K
    n_rows = (tk + ROW_BLOCK - 1) // ROW_BLOCK * ROW_BLOCK + N_EXPERTS * ROW_BLOCK

    subs = r_max // ROW_BLOCK
    n_items = N_EXPERTS + (n_rows // ROW_BLOCK - N_EXPERTS) // subs
    items_e = (n128 + subs - 1) // subs
    items_end = jnp.cumsum(items_e)
    total = items_end[-1]
    slot = jnp.arange(n_items, dtype=jnp.int32)
    live = slot < total
    s_eff = jnp.minimum(slot, total - 1)
    e = jnp.minimum(jnp.searchsorted(items_end, s_eff, side='right'), N_EXPERTS - 1).astype(jnp.int32)
    local = s_eff - (items_end[e] - items_e[e])
    item_start = (pad_start[e] + local * r_max).astype(jnp.int32)
    item_nsub = jnp.where(live, jnp.clip(n128[e] - local * subs, 0, subs), 0).astype(jnp.int32)
    tail_block = jnp.sum(n128).reshape(1).astype(jnp.int32)
    return dest.reshape(tk), n_rows, e, live, item_start, item_nsub, tail_block


def kernel(x, w_in, ret_gn_gain, w_out, ln1_gain, ln1_bias, w_router, b_router, w_gate_up, b_gate_up,
           w_down, b_down, ln2_gain, ln2_bias):
    b, s, d = x.shape
    t = b * s
    depth = w_in.shape[0]
    alpha = (2 * depth) ** 0.25
    r_max = 1280
    tf = 512
    n_f = (w_gate_up.shape[-1] // 2) // tf
    xt = x.reshape(t, d)
    for layer in range(depth):
        proj = _in_proj(xt, w_in[layer].astype(BF16))
        sb = _stickbreak(proj, b, s)
        ret = _retention(proj, ret_gn_gain[layer], b, s)
        x1, x1_slabs, meta, counts = _outproj(sb, ret, xt, w_out[layer].astype(BF16), ln1_gain[layer],
                                              ln1_bias[layer], w_router[layer], b_router[layer], alpha)
        dest_flat, n_rows, item_e, live, item_start, item_nsub, tail_block = _routing_plan(
            meta, counts, t, r_max)
        item_j = jnp.where(live, -1, n_f - 1).astype(jnp.int32)
        live_steps = (jnp.sum(live.astype(jnp.int32)) * n_f).reshape(1)
        y_rows = _moe_ffn(item_e, item_j, item_start, item_nsub, tail_block, live_steps, dest_flat, x1_slabs,
                          n_rows, w_gate_up[layer], b_gate_up[layer], w_down[layer], b_down[layer], r_max, tf)
        xt = _combine(dest_flat, y_rows, x1, meta, ln2_gain[layer], ln2_bias[layer], alpha)
    return xt.reshape(b, s, d)
```

```python
import functools
import math

import jax
import jax.numpy as jnp
from jax import lax
from jax.experimental import pallas as pl
from jax.experimental.pallas import tpu as pltpu

F32 = jnp.float32
BF16 = jnp.bfloat16

HEAD_DIM = 128
SB_HEADS = 8
RET_HEADS = 8
CHUNK = 64
ROPE_BASE = 10000.0
N_EXPERTS = 32
TOP_K = 4
SWIGLU_LIMIT = 7.0
SWIGLU_ALPHA = 1.702
LN_EPS = 1e-5
GN_EPS = 1e-5

V7X_VMEM_LIMIT_BYTES = 58 * 1024 * 1024
LANES = 128
ROW_BLOCK = 128
NEG_BIG = -1e30
EXP_UNDERFLOW = -105.0


def _cparams(sem, vmem=V7X_VMEM_LIMIT_BYTES):
    return pltpu.CompilerParams(dimension_semantics=sem, vmem_limit_bytes=vmem)


def _matmul_kernel(x_ref, w_ref, o_ref, x_bf):
    @pl.when(pl.program_id(1) == 0)
    def _():
        x_bf[...] = x_ref[...].astype(BF16)
    o_ref[...] = jnp.dot(x_bf[...], w_ref[...], preferred_element_type=F32).astype(o_ref.dtype)


def _in_proj(x, w_bf, tm=1024, tn=1024):
    t, d = x.shape
    n = w_bf.shape[1]
    tm = min(tm, t)
    return pl.pallas_call(
        _matmul_kernel,
        grid=(t // tm, n // tn),
        in_specs=[pl.BlockSpec((tm, d), lambda i, j: (i, 0)),
                  pl.BlockSpec((d, tn), lambda i, j: (0, j))],
        out_specs=pl.BlockSpec((tm, tn), lambda i, j: (i, j)),
        out_shape=jax.ShapeDtypeStruct((t, n), BF16),
        scratch_shapes=[pltpu.VMEM((tm, d), BF16)],
        compiler_params=_cparams(("arbitrary", "arbitrary")),
        name="in_proj",
    )(x, w_bf)


def _sb_kernel(q_ref, k_ref, v_ref, o_ref, *, seq, scale, group, sub):
    blk = 128
    nq = seq // blk
    row = lax.broadcasted_iota(jnp.int32, (blk, blk), 0)
    col = lax.broadcasted_iota(jnp.int32, (blk, blk), 1)
    causal = col < row
    r2 = lax.broadcasted_iota(jnp.int32, (2 * blk, 2 * blk), 0)
    c2 = lax.broadcasted_iota(jnp.int32, (2 * blk, 2 * blk), 1)
    cum_rhs = jnp.where((c2 >= blk) | ((r2 & (blk - 1)) >= c2), 1.0, 0.0).astype(BF16)

    n_sub = group // sub
    causal_sub = jnp.concatenate([causal] * sub, axis=0)
    heads_of = lambda s: range(s * sub, (s + 1) * sub)
    cols = lambda g: slice(g * blk, (g + 1) * blk)

    def block(qbs, kb, carries, accs, masked):
        ks = pl.ds(pl.multiple_of(kb * blk, blk), blk)
        zs = [jnp.concatenate(
            [lax.dot_general(qbs[g], k_ref[ks, cols(g)], (((1,), (1,)), ((), ())),
                             preferred_element_type=F32) for g in heads_of(s)], axis=0) * scale
              for s in range(n_sub)]
        sums = []
        for z in zs:
            neg_z = -z
            lnb = jnp.minimum(neg_z, 0.0) - jnp.log(1.0 + jnp.exp(jnp.minimum(z, neg_z)))
            if masked:
                lnb = jnp.where(causal_sub, lnb, 0.0)
            hi = lnb.astype(BF16)
            lo = (lnb - hi.astype(F32)).astype(BF16)
            sums.append(jnp.dot(jnp.concatenate([hi, lo], axis=1), cum_rhs, preferred_element_type=F32))
        new_carries, new_accs = [], []
        for s in range(n_sub):
            incl = sums[s][:, :blk]
            total = sums[s][:, blk:]
            w = jnp.exp(zs[s] + incl + carries[s])
            if masked:
                w = jnp.where(causal_sub, w, 0.0)
            w = w.astype(BF16)
            for n, g in enumerate(heads_of(s)):
                new_accs.append(accs[g] + jnp.dot(w[n * blk:(n + 1) * blk], v_ref[ks, cols(g)],
                                                  preferred_element_type=F32))
            new_carries.append(carries[s] + total)
        return tuple(new_carries), tuple(new_accs)

    def q_body(qi, _):
        qs = pl.ds(pl.multiple_of(qi * blk, blk), blk)
        qbs = [q_ref[qs, cols(g)] for g in range(group)]
        state = block(qbs, qi, (jnp.zeros((sub * blk, blk), F32),) * n_sub,
                      (jnp.zeros((blk, blk), F32),) * group, True)

        def some_weight_left(carries):
            top = functools.reduce(jnp.maximum, carries)
            return (jnp.max(top) > EXP_UNDERFLOW).astype(jnp.int32)

        def live(st):
            t, more, _, _ = st
            return jnp.logical_and(t < qi, more > 0)

        def kb_body(st):
            t, _, carries, accs = st
            carries, accs = block(qbs, qi - 1 - t, carries, accs, False)
            return t + 1, some_weight_left(carries), carries, accs

        _, _, _, accs = lax.while_loop(live, kb_body, (0, 1, state[0], state[1]))
        for g in range(group):
            o_ref[qs, cols(g)] = accs[g].astype(o_ref.dtype)
        return 0

    lax.fori_loop(0, nq, q_body, 0)


def _stickbreak(proj, batch, seq, group=8, sub=4):
    t = batch * seq
    h = SB_HEADS
    n_groups = h // group
    width = group * HEAD_DIM
    kern = functools.partial(_sb_kernel, seq=seq, scale=1.0 / math.sqrt(HEAD_DIM), group=group,
                             sub=sub)
    spec = lambda off: pl.BlockSpec((seq, width), lambda b, hg: (b, off * n_groups + hg))
    return pl.pallas_call(
        kern,
        grid=(batch, n_groups),
        in_specs=[spec(0), spec(1), spec(2)],
        out_specs=pl.BlockSpec((seq, width), lambda b, hg: (b, hg)),
        out_shape=jax.ShapeDtypeStruct((t, h * HEAD_DIM), BF16),
        compiler_params=_cparams(("arbitrary", "arbitrary")),
        name="stickbreak",
    )(proj, proj, proj)


def _ret_kernel(q_ref, k_ref, v_ref, g_ref, cos_ref, sin_ref, intra_ref, qdec_ref, kdec_ref,
                cdec_ref, gain_ref, o_ref, *, seq, heads):
    n_chunks = seq // CHUNK
    half = HEAD_DIM // 2
    k_scale = HEAD_DIM ** -0.5
    cols = lambda h: slice(h * HEAD_DIM, (h + 1) * HEAD_DIM)
    contract_last = (((1,), (1,)), ((), ()))
    contract_rows = (((0,), (0,)), ((), ()))

    def chunk(n, states):
        rs = pl.ds(pl.multiple_of(n * CHUNK, CHUNK), CHUNK)
        cos = cos_ref[rs, :]
        sin = sin_ref[rs, :]
        qrs, krs, crosses, kvs = [], [], [], []
        for h in range(heads):
            q = q_ref[rs, cols(h)].astype(F32)
            k = k_ref[rs, cols(h)].astype(F32)
            qr = q * cos + pltpu.roll(q, half, 1) * sin
            kr = (k * cos + pltpu.roll(k, half, 1) * sin) * k_scale
            qrs.append(qr.astype(BF16))
            krs.append(kr.astype(BF16))
            crosses.append(jnp.dot((qr * qdec_ref[h]).astype(BF16), states[h].astype(BF16),
                                   preferred_element_type=F32))
            kvs.append(lax.dot_general((kr * kdec_ref[h]).astype(BF16), v_ref[rs, cols(h)], contract_rows,
                                       preferred_element_type=F32))
        scores = [lax.dot_general(qrs[h], krs[h], contract_last, preferred_element_type=F32) * intra_ref[h]
                  for h in range(heads)]
        outs = [crosses[h] + jnp.dot(scores[h].astype(BF16), v_ref[rs, cols(h)], preferred_element_type=F32)
                for h in range(heads)]
        new_states = []
        for h in range(heads):
            o = outs[h]
            mu = jnp.mean(o, axis=-1, keepdims=True)
            var = jnp.mean(jnp.square(o - mu), axis=-1, keepdims=True)
            on = (o - mu) * lax.rsqrt(var + GN_EPS)
            g = g_ref[rs, cols(h)].astype(F32)
            out = on * gain_ref[:, cols(h)] * (g * jax.nn.sigmoid(g))
            o_ref[rs, cols(h)] = out.astype(o_ref.dtype)
            new_states.append(states[h] * cdec_ref[h] + kvs[h])
        return tuple(new_states)

    lax.fori_loop(0, n_chunks, chunk, (jnp.zeros((HEAD_DIM, HEAD_DIM), F32),) * heads)


def _retention_tables(seq):
    d = HEAD_DIM
    inv_freq = ROPE_BASE ** (-jnp.arange(0, d, 2, dtype=F32) / d)
    ang = jnp.arange(seq, dtype=F32)[:, None] * inv_freq[None, :]
    cos, sin = jnp.cos(ang), jnp.sin(ang)
    cos_full = jnp.concatenate([cos, cos], axis=-1)
    sin_signed = jnp.concatenate([-sin, sin], axis=-1)
    log_gamma = jnp.log1p(-jnp.exp2(-5.0 - jnp.arange(RET_HEADS, dtype=F32)))
    i = jnp.arange(CHUNK, dtype=F32)
    intra = jnp.exp(log_gamma[:, None, None] * jnp.abs(i[:, None] - i[None, :]))
    k_decay = jnp.exp(log_gamma[:, None] * (CHUNK - 1 - i))
    q_decay = jnp.exp(log_gamma[:, None] * (i + 1.0))
    c_decay = jnp.exp(log_gamma * CHUNK)
    bc = lambda a: jnp.broadcast_to(a[..., None], a.shape + (d,))
    return cos_full, sin_signed, intra, bc(q_decay), bc(k_decay), bc(c_decay[:, None])


def _retention(proj, gn_gain, batch, seq):
    t = batch * seq
    h = RET_HEADS
    base = 3 * SB_HEADS
    cos_full, sin_signed, intra, qdec, kdec, cdec = _retention_tables(seq)
    width = h * HEAD_DIM
    group0 = base // h
    spec = lambda off: pl.BlockSpec((seq, width), lambda b: (b, group0 + off))
    full = pl.BlockSpec((seq, HEAD_DIM), lambda b: (0, 0))
    table = lambda r, c: pl.BlockSpec((h, r, c), lambda b: (0, 0, 0))
    return pl.pallas_call(
        functools.partial(_ret_kernel, seq=seq, heads=h),
        grid=(batch,),
        in_specs=[spec(0), spec(1), spec(2), spec(3), full, full,
                  table(CHUNK, CHUNK), table(CHUNK, HEAD_DIM), table(CHUNK, HEAD_DIM),
                  table(1, HEAD_DIM),
                  pl.BlockSpec((1, width), lambda b: (0, 0))],
        out_specs=pl.BlockSpec((seq, width), lambda b: (b, 0)),
        out_shape=jax.ShapeDtypeStruct((t, width), BF16),
        compiler_params=_cparams(("arbitrary",)),
        name="retention",
    )(proj, proj, proj, proj, cos_full, sin_signed, intra, qdec, kdec, cdec,
      gn_gain.reshape(1, h * HEAD_DIM).astype(F32))


def _layer_norm(hid, gain, bias):
    mu = jnp.mean(hid, axis=-1, keepdims=True)
    cen = hid - mu
    var = jnp.mean(jnp.square(cen), axis=-1, keepdims=True)
    return cen * lax.rsqrt(var + LN_EPS) * gain + bias


SLAB_PAD = 4


def _to_slabs(slab_ref, base, rows, value, pitch):
    for c in range(value.shape[1] // LANES):
        slab_ref[pl.ds(base + c, rows, stride=pitch), :] = value[:, c * LANES:(c + 1) * LANES]


def _bf16_bits(x):
    u = pltpu.bitcast(x, jnp.uint32)
    return lax.shift_right_logical(u + jnp.uint32(0x7FFF) + (lax.shift_right_logical(u, jnp.uint32(16))
                                                             & jnp.uint32(1)), jnp.uint32(16))


def _to_packed_slabs(slab_ref, rows, value):
    words = value.shape[1] // (2 * LANES)
    for c in range(words):
        lo = _bf16_bits(value[:, (2 * c) * LANES:(2 * c + 1) * LANES])
        hi = _bf16_bits(value[:, (2 * c + 1) * LANES:(2 * c + 2) * LANES])
        slab_ref[pl.ds(c, rows, stride=words), :] = lo | lax.shift_left(hi, jnp.uint32(16))


def _from_packed_slabs(slab_ref, base, rows, words, pitch):
    out = []
    for c in range(words):
        w = slab_ref[pl.ds(base + c, rows, stride=pitch), :]
        out.append(pltpu.bitcast(lax.shift_left(w, jnp.uint32(16)), F32).astype(BF16))
        out.append(pltpu.bitcast(w & jnp.uint32(0xFFFF0000), F32).astype(BF16))
    return jnp.concatenate(out, axis=1)


def _outproj_kernel(sb_ref, ret_ref, x_ref, w_ref, g_ref, b_ref, wr_ref, br_ref,
                    x1_ref, x1s_ref, meta_ref, cnt_ref, *, alpha, sb_width):
    tm = x_ref.shape[0]
    mix = jnp.dot(sb_ref[...], w_ref[:sb_width, :], preferred_element_type=F32)
    mix = mix + jnp.dot(ret_ref[...], w_ref[sb_width:, :], preferred_element_type=F32)
    x1 = _layer_norm(alpha * x_ref[...] + mix, g_ref[...], b_ref[...])
    x1_ref[...] = x1
    _to_packed_slabs(x1s_ref, tm, x1)

    logits = jnp.dot(x1.astype(BF16), wr_ref[...], preferred_element_type=F32) + br_ref[...]
    lane = lax.broadcasted_iota(jnp.int32, (tm, LANES), 1).astype(F32)
    vals = logits
    tops, idxs, hots = [], [], []
    for _ in range(TOP_K):
        m = jnp.max(vals, axis=-1, keepdims=True)
        idx = jnp.min(jnp.where(vals == m, lane, float(LANES)), axis=-1, keepdims=True)
        hot = lane == idx
        vals = jnp.where(hot, NEG_BIG * 2.0, vals)
        tops.append(m)
        idxs.append(idx)
        hots.append(hot)
    exps = [jnp.exp(m - tops[0]) for m in tops]
    denom = exps[0] + exps[1] + exps[2] + exps[3]
    gates = [e / denom for e in exps]

    @pl.when(pl.program_id(0) == 0)
    def _():
        cnt_ref[...] = jnp.zeros_like(cnt_ref)

    multi = jnp.zeros((tm, LANES), F32)
    for hot in hots:
        multi = multi + jnp.where(hot, 1.0, 0.0)
    r = lax.broadcasted_iota(jnp.int32, (tm, tm), 0)
    c = lax.broadcasted_iota(jnp.int32, (tm, tm), 1)
    strict_lower = jnp.where(c < r, 1.0, 0.0).astype(BF16)
    before = jnp.dot(strict_lower, multi.astype(BF16), preferred_element_type=F32) + cnt_ref[...]
    cnt_ref[...] = cnt_ref[...] + jnp.sum(multi, axis=0, keepdims=True)

    meta = jnp.zeros((tm, LANES), F32)
    for kk in range(TOP_K):
        rank = jnp.sum(jnp.where(hots[kk], before, 0.0), axis=-1, keepdims=True)
        meta = jnp.where(lane == kk, idxs[kk], meta)
        meta = jnp.where(lane == TOP_K + kk, gates[kk], meta)
        meta = jnp.where(lane == 2 * TOP_K + kk, rank, meta)
    meta_ref[...] = meta


def _outproj(sb, ret, xt, w_out_bf, ln_g, ln_b, w_router, b_router, alpha, tm=256):
    t, d = xt.shape
    tm = min(tm, t)
    sbw = sb.shape[1]
    wr = jnp.zeros((d, LANES), BF16).at[:, :N_EXPERTS].set(w_router.astype(BF16))
    br = jnp.full((1, LANES), NEG_BIG, F32).at[0, :N_EXPERTS].set(b_router.astype(F32))
    row = lambda w: pl.BlockSpec((tm, w), lambda i: (i, 0))
    const = lambda r, c: pl.BlockSpec((r, c), lambda i: (0, 0))
    return pl.pallas_call(
        functools.partial(_outproj_kernel, alpha=alpha, sb_width=sbw),
        grid=(t // tm,),
        in_specs=[row(sbw), row(ret.shape[1]), row(d), const(d, d), const(1, d), const(1, d),
                  const(d, LANES), const(1, LANES)],
        out_specs=[row(d), pl.BlockSpec((tm * (d // (2 * LANES)), LANES), lambda i: (i, 0)), row(LANES),
                   const(1, LANES)],
        out_shape=[jax.ShapeDtypeStruct((t, d), F32),
                   jax.ShapeDtypeStruct((t * (d // (2 * LANES)), LANES), jnp.uint32),
                   jax.ShapeDtypeStruct((t, LANES), F32),
                   jax.ShapeDtypeStruct((1, LANES), F32)],
        compiler_params=_cparams(("arbitrary",)),
        name="outproj_ln_router",
    )(sb, ret, xt, w_out_bf, ln_g.reshape(1, d).astype(F32), ln_b.reshape(1, d).astype(F32), wr, br)


DMA_UNROLL = 8


def _wait_slabs(hbm, buf, sem, slot, count, chunks):
    def body(n, _):
        pltpu.make_async_copy(hbm.at[pl.ds(0, chunks), :], buf.at[slot, pl.ds(0, chunks), :],
                              sem.at[slot]).wait()
        return 0
    lax.fori_loop(0, count, body, 0, unroll=DMA_UNROLL)


def _moe_kernel(ie_ref, ij_ref, ist_ref, ins_ref, tail_ref, live_ref, dest_ref, x_hbm, wgu_hbm, wd_hbm, bg_ref,
                bu_ref, bd_ref, zeros_hbm, y_hbm, x_vmem, acc, wg_f, wu_f, wd_f, wgu_bf, wd_bf, stage, gbuf, tok_ref,
                sem_g, sem_out, sem_w, *, n_f):
    del ij_ref
    i = pl.program_id(0)
    j = pl.program_id(1)
    nsub = ins_ref[i]
    start = ist_ref[i]
    d = acc.shape[1]
    tf = wd_bf.shape[0]
    d_ff = n_f * tf
    y_pitch = d // LANES + SLAB_PAD
    block_rows = ROW_BLOCK * y_pitch

    def weight_copies(e, jj, slot):
        col_g = pl.ds(pl.multiple_of(jj * tf, tf), tf)
        col_u = pl.ds(pl.multiple_of(d_ff + jj * tf, tf), tf)
        copies = []
        for half in range(2):
            rows_k = pl.ds(half * (d // 2), d // 2)
            rows_f = pl.ds(pl.multiple_of(jj * tf + half * (tf // 2), tf // 2), tf // 2)
            dst_f = pl.ds(half * (tf // 2), tf // 2)
            copies += [
                pltpu.make_async_copy(wgu_hbm.at[e, rows_k, col_g], wg_f.at[slot, rows_k, :], sem_w.at[slot]),
                pltpu.make_async_copy(wgu_hbm.at[e, rows_k, col_u], wu_f.at[slot, rows_k, :], sem_w.at[slot]),
                pltpu.make_async_copy(wd_hbm.at[e, rows_f, :], wd_f.at[slot, dst_f, :], sem_w.at[slot]),
            ]
        return copies

    def start_weights(e, jj, slot):
        for n, c in enumerate(weight_copies(e, jj, slot)):
            c.start(priority=n % 2)

    step = i * n_f + j
    w_slot = step % 2

    @pl.when(step == 0)
    def _():
        start_weights(ie_ref[0], 0, 0)

    @pl.when(step + 1 < live_ref[0])
    def _():
        last = j == n_f - 1
        start_weights(ie_ref[jnp.where(last, i + 1, i)], jnp.where(last, 0, j + 1), 1 - w_slot)

    def rows_of(r):
        return pl.ds(pl.multiple_of(r * ROW_BLOCK, ROW_BLOCK), ROW_BLOCK)

    def slabs_of(block):
        return pl.ds(pl.multiple_of(block * block_rows, block_rows), block_rows)

    def for_each(lo, hi, fn):
        def body(r, _):
            fn(r)
            return 0
        lax.fori_loop(lo, hi, body, 0)

    @pl.when((i == 0) & (j == 0))
    def _():
        stage[...] = jnp.zeros(stage.shape, F32)

        def zero_copy(bk):
            return pltpu.make_async_copy(stage.at[0], y_hbm.at[slabs_of(bk), :], sem_out.at[0])

        n_blocks = y_hbm.shape[0] // block_rows
        for_each(tail_ref[0], n_blocks, lambda bk: zero_copy(bk).start())
        for_each(tail_ref[0], n_blocks, lambda bk: zero_copy(bk).wait())

    first_block = start // ROW_BLOCK
    words = d // (2 * LANES)

    pitch = words + 1

    def build_row_tokens():
        n_tok = dest_ref.shape[0] // TOP_K
        clear = pltpu.make_async_copy(zeros_hbm, tok_ref, sem_g)
        clear.start()
        clear.wait()

        def fill(m, _):
            for u in range(DMA_UNROLL):
                t = m * DMA_UNROLL + u
                for kk in range(TOP_K):
                    tok_ref[dest_ref[kk * n_tok + t]] = t
            return 0
        lax.fori_loop(0, n_tok // DMA_UNROLL, fill, 0)

    def start_gather(item):
        row0 = ist_ref[item]

        def body(m, _):
            for u in range(DMA_UNROLL):
                r = m * DMA_UNROLL + u
                src = pl.ds(pl.multiple_of(tok_ref[row0 + r] * words, words), words)
                pltpu.make_async_copy(x_hbm.at[src, :], gbuf.at[pl.ds(r * pitch, words), :], sem_g).start(
                    priority=u % 2)
            return 0
        lax.fori_loop(0, ins_ref[item] * (ROW_BLOCK // DMA_UNROLL), body, 0)

    def wait_gather(item):
        def body(m, _):
            for _u in range(DMA_UNROLL):
                pltpu.make_async_copy(x_hbm.at[pl.ds(0, words), :], gbuf.at[pl.ds(0, words), :], sem_g).wait()
            return 0
        lax.fori_loop(0, ins_ref[item] * (ROW_BLOCK // DMA_UNROLL), body, 0)

    def out_copy(r, slot):
        return pltpu.make_async_copy(stage.at[slot], y_hbm.at[slabs_of(first_block + r), :],
                                     sem_out.at[slot])

    def span(r, n_blocks):
        return pl.ds(pl.multiple_of(r * ROW_BLOCK, ROW_BLOCK), n_blocks * ROW_BLOCK)

    def expert_mlp(xb, w_gate, w_up, w_down):
        gate = jnp.dot(xb, w_gate, preferred_element_type=F32) + bg_ref[...]
        up = jnp.dot(xb, w_up, preferred_element_type=F32) + bu_ref[...]
        gate = jnp.minimum(gate, SWIGLU_LIMIT)
        up = jnp.clip(up, -SWIGLU_LIMIT, SWIGLU_LIMIT)
        act = (up + 1.0) * (gate * jax.nn.sigmoid(SWIGLU_ALPHA * gate))
        return jnp.dot(act.astype(BF16), w_down, preferred_element_type=F32)

    def hidden(rows):
        return expert_mlp(x_vmem[rows, :], wgu_bf[:, :tf], wgu_bf[:, tf:], wd_bf[...])

    @pl.when(nsub > 0)
    def _():
        @pl.when(j == 0)
        def _():
            @pl.when(i == 0)
            def _():
                build_row_tokens()
                start_gather(0)
            wait_gather(i)
            bias_rows = jnp.broadcast_to(bd_ref[...], (ROW_BLOCK, d))

            def unpack(r):
                x_vmem[rows_of(r), :] = _from_packed_slabs(gbuf, r * (ROW_BLOCK * pitch), ROW_BLOCK, words, pitch)
                acc[rows_of(r), :] = bias_rows
            for_each(0, nsub, unpack)

            @pl.when(i + 1 < pl.num_programs(0))
            def _():
                start_gather(i + 1)

        for c in weight_copies(ie_ref[i], j, w_slot):
            c.wait()
        w_gate = wg_f[w_slot].astype(BF16)
        w_up = wu_f[w_slot].astype(BF16)
        w_down = wd_f[w_slot].astype(BF16)
        wgu_bf[:, :tf] = w_gate
        wgu_bf[:, tf:] = w_up
        wd_bf[...] = w_down
        first = expert_mlp(x_vmem[span(0, 1), :], w_gate, w_up, w_down)
        rest = nsub - 1
        n_pairs = rest // 2
        odd = rest % 2 == 1

        @pl.when(j < n_f - 1)
        def _():
            acc[span(0, 1), :] += first

            def pair(p):
                acc[span(1 + 2 * p, 2), :] += hidden(span(1 + 2 * p, 2))
            for_each(0, n_pairs, pair)

            @pl.when(odd)
            def _():
                acc[span(nsub - 1, 1), :] += hidden(span(nsub - 1, 1))

        @pl.when(j == n_f - 1)
        def _():
            def write_out(r, slot, final, reuse):
                @pl.when(reuse)
                def _():
                    out_copy(r, slot).wait()
                _to_slabs(stage.at[slot], 0, ROW_BLOCK, final, y_pitch)
                out_copy(r, slot).start()

            write_out(0, 0, acc[span(0, 1), :] + first, False)

            def pair(p):
                final = acc[span(1 + 2 * p, 2), :] + hidden(span(1 + 2 * p, 2))
                write_out(1 + 2 * p, 1, final[:ROW_BLOCK], p > 0)
                write_out(2 + 2 * p, 0, final[ROW_BLOCK:], True)
            for_each(0, n_pairs, pair)

            @pl.when(odd)
            def _():
                final = acc[span(nsub - 1, 1), :] + hidden(span(nsub - 1, 1))
                write_out(nsub - 1, 1, final, n_pairs > 0)

            out_copy(0, 0).wait()

            @pl.when(rest > 0)
            def _():
                out_copy(0, 1).wait()


def _moe_ffn(item_e, item_j, item_start, item_nsub, tail_block, live_steps, dest_flat, x1_packed, n_rows,
             w_gate_up, b_gate_up, w_down, b_down, r_max, tf=512):
    n_e, d, two_f = w_gate_up.shape
    chunks = d // LANES
    words = d // (2 * LANES)
    d_ff = two_f // 2
    n_f = d_ff // tf
    n_items = item_e.shape[0]

    def jf(i, j, ij):
        return jnp.where(ij[i] < 0, j, ij[i])

    hbm = pl.BlockSpec(memory_space=pl.ANY)
    in_specs = [
        hbm, hbm, hbm,
        pl.BlockSpec((None, 1, tf), lambda i, j, ie, ij, ist, ins, tl, lv, tk: (ie[i], 0, jf(i, j, ij))),
        pl.BlockSpec((None, 1, tf), lambda i, j, ie, ij, ist, ins, tl, lv, tk: (ie[i], 0, n_f + jf(i, j, ij))),
        pl.BlockSpec((None, 1, d), lambda i, j, ie, ij, ist, ins, tl, lv, tk: (ie[i], 0, 0)),
        hbm,
    ]
    return pl.pallas_call(
        functools.partial(_moe_kernel, n_f=n_f),
        grid_spec=pltpu.PrefetchScalarGridSpec(
            num_scalar_prefetch=7,
            grid=(n_items, n_f),
            in_specs=in_specs,
            out_specs=pl.BlockSpec(memory_space=pl.ANY),
            scratch_shapes=[pltpu.VMEM((r_max, d), BF16), pltpu.VMEM((r_max, d), F32),
                            pltpu.VMEM((2, d, tf), F32), pltpu.VMEM((2, d, tf), F32),
                            pltpu.VMEM((2, tf, d), F32),
                            pltpu.VMEM((d, 2 * tf), BF16), pltpu.VMEM((tf, d), BF16),
                            pltpu.VMEM((2, ROW_BLOCK * (chunks + SLAB_PAD), LANES), F32),
                            pltpu.VMEM((r_max * (words + 1), LANES), jnp.uint32),
                            pltpu.SMEM((n_rows,), jnp.int32),
                            pltpu.SemaphoreType.DMA(()), pltpu.SemaphoreType.DMA((2,)),
                            pltpu.SemaphoreType.DMA((2,))],
        ),
        out_shape=jax.ShapeDtypeStruct((n_rows * (chunks + SLAB_PAD), LANES), F32),
        compiler_params=_cparams(("arbitrary", "arbitrary")),
        name="moe_ffn",
    )(item_e, item_j, item_start, item_nsub, tail_block, live_steps, dest_flat, x1_packed, w_gate_up, w_down,
      b_gate_up.reshape(n_e, 1, two_f), b_gate_up.reshape(n_e, 1, two_f), b_down.reshape(n_e, 1, d),
      jnp.zeros((n_rows,), jnp.int32))


def _combine_kernel(dest_ref, y_hbm, x1_ref, meta_ref, g_ref, b_ref, o_ref, buf, sem, *, alpha, tm, chunks):
    i = pl.program_id(0)
    per_step = tm * TOP_K
    n_tok = dest_ref.shape[0] // TOP_K
    pitch = chunks + 1

    def issue_tile(step, slot):
        def body(tt, _):
            for kk in range(TOP_K):
                src = pl.ds(dest_ref[kk * n_tok + step * tm + tt] * (chunks + SLAB_PAD), chunks)
                dst = pl.ds((kk * tm + tt) * pitch, chunks)
                pltpu.make_async_copy(y_hbm.at[src, :], buf.at[slot, dst, :], sem.at[slot]).start(
                    priority=kk % 2)
            return 0
        lax.fori_loop(0, tm, body, 0, unroll=DMA_UNROLL // TOP_K)

    @pl.when(i == 0)
    def _():
        issue_tile(0, 0)

    @pl.when(i + 1 < pl.num_programs(0))
    def _():
        issue_tile(i + 1, (i + 1) % 2)

    slot = i % 2
    _wait_slabs(y_hbm, buf, sem, slot, per_step, chunks)
    meta = meta_ref[...]
    y = jnp.zeros(x1_ref.shape, F32)
    for kk in range(TOP_K):
        rows = jnp.concatenate(
            [buf[slot, pl.ds(kk * tm * pitch + c, tm, stride=pitch), :] for c in range(chunks)], axis=1)
        y = y + meta[:, TOP_K + kk:TOP_K + kk + 1] * rows
    o_ref[...] = _layer_norm(alpha * x1_ref[...] + y, g_ref[...], b_ref[...])


def _combine(dest_flat, y_rows, x1, meta, ln_g, ln_b, alpha, tm=128):
    t, d = x1.shape
    tm = min(tm, t)
    chunks = d // LANES
    row = lambda w: pl.BlockSpec((tm, w), lambda i, dest: (i, 0))
    const = pl.BlockSpec((1, d), lambda i, dest: (0, 0))
    return pl.pallas_call(
        functools.partial(_combine_kernel, alpha=alpha, tm=tm, chunks=chunks),
        grid_spec=pltpu.PrefetchScalarGridSpec(
            num_scalar_prefetch=1,
            grid=(t // tm,),
            in_specs=[pl.BlockSpec(memory_space=pl.ANY), row(d), row(LANES), const, const],
            out_specs=row(d),
            scratch_shapes=[pltpu.VMEM((2, TOP_K * tm * (chunks + 1), LANES), F32),
                            pltpu.SemaphoreType.DMA((2,))],
        ),
        out_shape=jax.ShapeDtypeStruct((t, d), F32),
        compiler_params=_cparams(("arbitrary",)),
        name="combine_ln",
    )(dest_flat, y_rows, x1, meta, ln_g.reshape(1, d).astype(F32), ln_b.reshape(1, d).astype(F32))


def _routing_plan(meta, counts_f, t, r_max):
    meta_t = meta.T
    idx = meta_t[0:TOP_K].astype(jnp.int32)
    rank = meta_t[2 * TOP_K:3 * TOP_K].astype(jnp.int32)
    counts = counts_f[0, :N_EXPERTS].astype(jnp.int32)
    n128 = (counts + ROW_BLOCK - 1) // ROW_BLOCK
    padded = n128 * ROW_BLOCK
    pad_start = jnp.cumsum(padded) - padded
    experts = jnp.arange(N_EXPERTS, dtype=jnp.int32)[:, None, None]
    region = jnp.sum(jnp.where(idx[None] == experts, pad_start[:, None, None], 0), axis=0)
    dest = region + rank
    tk = t * TOP_K
    n_rows = (tk + ROW_BLOCK - 1) // ROW_BLOCK * ROW_BLOCK + N_EXPERTS * ROW_BLOCK

    subs = r_max // ROW_BLOCK
    n_items = N_EXPERTS + (n_rows // ROW_BLOCK - N_EXPERTS) // subs
    items_e = (n128 + subs - 1) // subs
    items_end = jnp.cumsum(items_e)
    total = items_end[-1]
    slot = jnp.arange(n_items, dtype=jnp.int32)
    live = slot < total
    s_eff = jnp.minimum(slot, total - 1)
    e = jnp.minimum(jnp.searchsorted(items_end, s_eff, side='right'), N_EXPERTS - 1).astype(jnp.int32)
    local = s_eff - (items_end[e] - items_e[e])
    item_start = (pad_start[e] + local * r_max).astype(jnp.int32)
    item_nsub = jnp.where(live, jnp.clip(n128[e] - local * subs, 0, subs), 0).astype(jnp.int32)
    tail_block = jnp.sum(n128).reshape(1).astype(jnp.int32)
    return dest.reshape(tk), n_rows, e, live, item_start, item_nsub, tail_block


def kernel(x, w_in, ret_gn_gain, w_out, ln1_gain, ln1_bias, w_router, b_router, w_gate_up, b_gate_up,
           w_down, b_down, ln2_gain, ln2_bias):
    b, s, d = x.shape
    t = b * s
    depth = w_in.shape[0]
    alpha = (2 * depth) ** 0.25
    r_max = 1280
    tf = 512
    n_f = (w_gate_up.shape[-1] // 2) // tf
    xt = x.reshape(t, d)
    for layer in range(depth):
        proj = _in_proj(xt, w_in[layer].astype(BF16))
        sb = _stickbreak(proj, b, s)
        ret = _retention(proj, ret_gn_gain[layer], b, s)
        x1, x1_slabs, meta, counts = _outproj(sb, ret, xt, w_out[layer].astype(BF16), ln1_gain[layer],
                                              ln1_bias[layer], w_router[layer], b_router[layer], alpha)
        dest_flat, n_rows, item_e, live, item_start, item_nsub, tail_block = _routing_plan(
            meta, counts, t, r_max)
        item_j = jnp.where(live, -1, n_f - 1).astype(jnp.int32)
        live_steps = (jnp.sum(live.astype(jnp.int32)) * n_f).reshape(1)
        y_rows = _moe_ffn(item_e, item_j, item_start, item_nsub, tail_block, live_steps, dest_flat, x1_slabs,
                          n_rows, w_gate_up[layer], b_gate_up[layer], w_down[layer], b_down[layer], r_max, tf)
        xt = _combine(dest_flat, y_rows, x1, meta, ln2_gain[layer], ln2_bias[layer], alpha)
    return xt.reshape(b, s, d)
```

```python
import functools
import math

import jax
import jax.numpy as jnp
from jax import lax
from jax.experimental import pallas as pl
from jax.experimental.pallas import tpu as pltpu

F32 = jnp.float32
BF16 = jnp.bfloat16

HEAD_DIM = 128
SB_HEADS = 8
RET_HEADS = 8
CHUNK = 64
ROPE_BASE = 10000.0
N_EXPERTS = 32
TOP_K = 4
SWIGLU_LIMIT = 7.0
SWIGLU_ALPHA = 1.702
LN_EPS = 1e-5
GN_EPS = 1e-5

V7X_VMEM_LIMIT_BYTES = 56 * 1024 * 1024
LANES = 128
ROW_BLOCK = 128
NEG_BIG = -1e30
EXP_UNDERFLOW = -105.0


def _cparams(sem, vmem=V7X_VMEM_LIMIT_BYTES):
    return pltpu.CompilerParams(dimension_semantics=sem, vmem_limit_bytes=vmem)


def _matmul_kernel(x_ref, w_ref, o_ref, x_bf):
    @pl.when(pl.program_id(1) == 0)
    def _():
        x_bf[...] = x_ref[...].astype(BF16)
    o_ref[...] = jnp.dot(x_bf[...], w_ref[...], preferred_element_type=F32).astype(o_ref.dtype)


def _in_proj(x, w_bf, tm=1024, tn=1024):
    t, d = x.shape
    n = w_bf.shape[1]
    tm = min(tm, t)
    return pl.pallas_call(
        _matmul_kernel,
        grid=(t // tm, n // tn),
        in_specs=[pl.BlockSpec((tm, d), lambda i, j: (i, 0)),
                  pl.BlockSpec((d, tn), lambda i, j: (0, j))],
        out_specs=pl.BlockSpec((tm, tn), lambda i, j: (i, j)),
        out_shape=jax.ShapeDtypeStruct((t, n), BF16),
        scratch_shapes=[pltpu.VMEM((tm, d), BF16)],
        compiler_params=_cparams(("arbitrary", "arbitrary")),
        name="in_proj",
    )(x, w_bf)


def _sb_kernel(q_ref, k_ref, v_ref, o_ref, *, seq, scale, group, sub):
    blk = 128
    nq = seq // blk
    row = lax.broadcasted_iota(jnp.int32, (blk, blk), 0)
    col = lax.broadcasted_iota(jnp.int32, (blk, blk), 1)
    causal = col < row
    r2 = lax.broadcasted_iota(jnp.int32, (2 * blk, 2 * blk), 0)
    c2 = lax.broadcasted_iota(jnp.int32, (2 * blk, 2 * blk), 1)
    cum_rhs = jnp.where((c2 >= blk) | ((r2 & (blk - 1)) >= c2), 1.0, 0.0).astype(BF16)

    n_sub = group // sub
    causal_sub = jnp.concatenate([causal] * sub, axis=0)
    heads_of = lambda s: range(s * sub, (s + 1) * sub)
    cols = lambda g: slice(g * blk, (g + 1) * blk)

    def block(qbs, kb, carries, accs, masked):
        ks = pl.ds(pl.multiple_of(kb * blk, blk), blk)
        zs = [jnp.concatenate(
            [lax.dot_general(qbs[g], k_ref[ks, cols(g)], (((1,), (1,)), ((), ())),
                             preferred_element_type=F32) for g in heads_of(s)], axis=0) * scale
              for s in range(n_sub)]
        sums = []
        for z in zs:
            neg_z = -z
            lnb = jnp.minimum(neg_z, 0.0) - jnp.log(1.0 + jnp.exp(jnp.minimum(z, neg_z)))
            if masked:
                lnb = jnp.where(causal_sub, lnb, 0.0)
            hi = lnb.astype(BF16)
            lo = (lnb - hi.astype(F32)).astype(BF16)
            sums.append(jnp.dot(jnp.concatenate([hi, lo], axis=1), cum_rhs, preferred_element_type=F32))
        new_carries, new_accs = [], []
        for s in range(n_sub):
            incl = sums[s][:, :blk]
            total = sums[s][:, blk:]
            w = jnp.exp(zs[s] + incl + carries[s])
            if masked:
                w = jnp.where(causal_sub, w, 0.0)
            w = w.astype(BF16)
            for n, g in enumerate(heads_of(s)):
                new_accs.append(accs[g] + jnp.dot(w[n * blk:(n + 1) * blk], v_ref[ks, cols(g)],
                                                  preferred_element_type=F32))
            new_carries.append(carries[s] + total)
        return tuple(new_carries), tuple(new_accs)

    def q_body(qi, _):
        qs = pl.ds(pl.multiple_of(qi * blk, blk), blk)
        qbs = [q_ref[qs, cols(g)] for g in range(group)]
        state = block(qbs, qi, (jnp.zeros((sub * blk, blk), F32),) * n_sub,
                      (jnp.zeros((blk, blk), F32),) * group, True)

        def some_weight_left(carries):
            top = functools.reduce(jnp.maximum, carries)
            return (jnp.max(top) > EXP_UNDERFLOW).astype(jnp.int32)

        def live(st):
            t, more, _, _ = st
            return jnp.logical_and(t < qi, more > 0)

        def kb_body(st):
            t, _, carries, accs = st
            carries, accs = block(qbs, qi - 1 - t, carries, accs, False)
            return t + 1, some_weight_left(carries), carries, accs

        _, _, _, accs = lax.while_loop(live, kb_body, (0, 1, state[0], state[1]))
        for g in range(group):
            o_ref[qs, cols(g)] = accs[g].astype(o_ref.dtype)
        return 0

    lax.fori_loop(0, nq, q_body, 0)


def _stickbreak(proj, batch, seq, group=8, sub=4):
    t = batch * seq
    h = SB_HEADS
    n_groups = h // group
    width = group * HEAD_DIM
    kern = functools.partial(_sb_kernel, seq=seq, scale=1.0 / math.sqrt(HEAD_DIM), group=group,
                             sub=sub)
    spec = lambda off: pl.BlockSpec((seq, width), lambda b, hg: (b, off * n_groups + hg))
    return pl.pallas_call(
        kern,
        grid=(batch, n_groups),
        in_specs=[spec(0), spec(1), spec(2)],
        out_specs=pl.BlockSpec((seq, width), lambda b, hg: (b, hg)),
        out_shape=jax.ShapeDtypeStruct((t, h * HEAD_DIM), BF16),
        compiler_params=_cparams(("arbitrary", "arbitrary")),
        name="stickbreak",
    )(proj, proj, proj)


def _ret_kernel(q_ref, k_ref, v_ref, g_ref, cos_ref, sin_ref, intra_ref, qdec_ref, kdec_ref,
                cdec_ref, gain_ref, o_ref, *, seq, heads):
    n_chunks = seq // CHUNK
    half = HEAD_DIM // 2
    k_scale = HEAD_DIM ** -0.5
    cols = lambda h: slice(h * HEAD_DIM, (h + 1) * HEAD_DIM)
    contract_last = (((1,), (1,)), ((), ()))
    contract_rows = (((0,), (0,)), ((), ()))

    def chunk(n, states):
        rs = pl.ds(pl.multiple_of(n * CHUNK, CHUNK), CHUNK)
        cos = cos_ref[rs, :]
        sin = sin_ref[rs, :]
        qrs, krs, crosses, kvs = [], [], [], []
        for h in range(heads):
            q = q_ref[rs, cols(h)].astype(F32)
            k = k_ref[rs, cols(h)].astype(F32)
            qr = q * cos + pltpu.roll(q, half, 1) * sin
            kr = (k * cos + pltpu.roll(k, half, 1) * sin) * k_scale
            qrs.append(qr.astype(BF16))
            krs.append(kr.astype(BF16))
            crosses.append(jnp.dot((qr * qdec_ref[h]).astype(BF16), states[h].astype(BF16),
                                   preferred_element_type=F32))
            kvs.append(lax.dot_general((kr * kdec_ref[h]).astype(BF16), v_ref[rs, cols(h)], contract_rows,
                                       preferred_element_type=F32))
        scores = [lax.dot_general(qrs[h], krs[h], contract_last, preferred_element_type=F32) * intra_ref[h]
                  for h in range(heads)]
        outs = [crosses[h] + jnp.dot(scores[h].astype(BF16), v_ref[rs, cols(h)], preferred_element_type=F32)
                for h in range(heads)]
        new_states = []
        for h in range(heads):
            o = outs[h]
            mu = jnp.mean(o, axis=-1, keepdims=True)
            var = jnp.mean(jnp.square(o - mu), axis=-1, keepdims=True)
            on = (o - mu) * lax.rsqrt(var + GN_EPS)
            g = g_ref[rs, cols(h)].astype(F32)
            out = on * gain_ref[:, cols(h)] * (g * jax.nn.sigmoid(g))
            o_ref[rs, cols(h)] = out.astype(o_ref.dtype)
            new_states.append(states[h] * cdec_ref[h] + kvs[h])
        return tuple(new_states)

    lax.fori_loop(0, n_chunks, chunk, (jnp.zeros((HEAD_DIM, HEAD_DIM), F32),) * heads)


def _retention_tables(seq):
    d = HEAD_DIM
    inv_freq = ROPE_BASE ** (-jnp.arange(0, d, 2, dtype=F32) / d)
    ang = jnp.arange(seq, dtype=F32)[:, None] * inv_freq[None, :]
    cos, sin = jnp.cos(ang), jnp.sin(ang)
    cos_full = jnp.concatenate([cos, cos], axis=-1)
    sin_signed = jnp.concatenate([-sin, sin], axis=-1)
    log_gamma = jnp.log1p(-jnp.exp2(-5.0 - jnp.arange(RET_HEADS, dtype=F32)))
    i = jnp.arange(CHUNK, dtype=F32)
    intra = jnp.exp(log_gamma[:, None, None] * jnp.abs(i[:, None] - i[None, :]))
    k_decay = jnp.exp(log_gamma[:, None] * (CHUNK - 1 - i))
    q_decay = jnp.exp(log_gamma[:, None] * (i + 1.0))
    c_decay = jnp.exp(log_gamma * CHUNK)
    bc = lambda a: jnp.broadcast_to(a[..., None], a.shape + (d,))
    return cos_full, sin_signed, intra, bc(q_decay), bc(k_decay), bc(c_decay[:, None])


def _retention(proj, gn_gain, batch, seq):
    t = batch * seq
    h = RET_HEADS
    base = 3 * SB_HEADS
    cos_full, sin_signed, intra, qdec, kdec, cdec = _retention_tables(seq)
    width = h * HEAD_DIM
    group0 = base // h
    spec = lambda off: pl.BlockSpec((seq, width), lambda b: (b, group0 + off))
    full = pl.BlockSpec((seq, HEAD_DIM), lambda b: (0, 0))
    table = lambda r, c: pl.BlockSpec((h, r, c), lambda b: (0, 0, 0))
    return pl.pallas_call(
        functools.partial(_ret_kernel, seq=seq, heads=h),
        grid=(batch,),
        in_specs=[spec(0), spec(1), spec(2), spec(3), full, full,
                  table(CHUNK, CHUNK), table(CHUNK, HEAD_DIM), table(CHUNK, HEAD_DIM),
                  table(1, HEAD_DIM),
                  pl.BlockSpec((1, width), lambda b: (0, 0))],
        out_specs=pl.BlockSpec((seq, width), lambda b: (b, 0)),
        out_shape=jax.ShapeDtypeStruct((t, width), BF16),
        compiler_params=_cparams(("arbitrary",)),
        name="retention",
    )(proj, proj, proj, proj, cos_full, sin_signed, intra, qdec, kdec, cdec,
      gn_gain.reshape(1, h * HEAD_DIM).astype(F32))


def _layer_norm(hid, gain, bias):
    mu = jnp.mean(hid, axis=-1, keepdims=True)
    cen = hid - mu
    var = jnp.mean(jnp.square(cen), axis=-1, keepdims=True)
    return cen * lax.rsqrt(var + LN_EPS) * gain + bias


SLAB_PAD = 4


def _to_slabs(slab_ref, base, rows, value, pitch):
    for c in range(value.shape[1] // LANES):
        slab_ref[pl.ds(base + c, rows, stride=pitch), :] = value[:, c * LANES:(c + 1) * LANES]


def _bf16_bits(x):
    u = pltpu.bitcast(x, jnp.uint32)
    return lax.shift_right_logical(u + jnp.uint32(0x7FFF) + (lax.shift_right_logical(u, jnp.uint32(16))
                                                             & jnp.uint32(1)), jnp.uint32(16))


def _to_packed_slabs(slab_ref, rows, value):
    words = value.shape[1] // (2 * LANES)
    for c in range(words):
        lo = _bf16_bits(value[:, (2 * c) * LANES:(2 * c + 1) * LANES])
        hi = _bf16_bits(value[:, (2 * c + 1) * LANES:(2 * c + 2) * LANES])
        slab_ref[pl.ds(c, rows, stride=words), :] = lo | lax.shift_left(hi, jnp.uint32(16))


def _from_packed_slabs(slab_ref, base, rows, words, pitch):
    out = []
    for c in range(words):
        w = slab_ref[pl.ds(base + c, rows, stride=pitch), :]
        out.append(pltpu.bitcast(lax.shift_left(w, jnp.uint32(16)), F32).astype(BF16))
        out.append(pltpu.bitcast(w & jnp.uint32(0xFFFF0000), F32).astype(BF16))
    return jnp.concatenate(out, axis=1)


def _outproj_kernel(sb_ref, ret_ref, x_ref, w_ref, g_ref, b_ref, wr_ref, br_ref,
                    x1_ref, x1s_ref, meta_ref, cnt_ref, *, alpha, sb_width):
    tm = x_ref.shape[0]
    mix = jnp.dot(sb_ref[...], w_ref[:sb_width, :], preferred_element_type=F32)
    mix = mix + jnp.dot(ret_ref[...], w_ref[sb_width:, :], preferred_element_type=F32)
    x1 = _layer_norm(alpha * x_ref[...] + mix, g_ref[...], b_ref[...])
    x1_ref[...] = x1
    _to_packed_slabs(x1s_ref, tm, x1)

    logits = jnp.dot(x1.astype(BF16), wr_ref[...], preferred_element_type=F32) + br_ref[...]
    lane = lax.broadcasted_iota(jnp.int32, (tm, LANES), 1).astype(F32)
    vals = logits
    tops, idxs, hots = [], [], []
    for _ in range(TOP_K):
        m = jnp.max(vals, axis=-1, keepdims=True)
        idx = jnp.min(jnp.where(vals == m, lane, float(LANES)), axis=-1, keepdims=True)
        hot = lane == idx
        vals = jnp.where(hot, NEG_BIG * 2.0, vals)
        tops.append(m)
        idxs.append(idx)
        hots.append(hot)
    exps = [jnp.exp(m - tops[0]) for m in tops]
    denom = exps[0] + exps[1] + exps[2] + exps[3]
    gates = [e / denom for e in exps]

    @pl.when(pl.program_id(0) == 0)
    def _():
        cnt_ref[...] = jnp.zeros_like(cnt_ref)

    multi = jnp.zeros((tm, LANES), F32)
    for hot in hots:
        multi = multi + jnp.where(hot, 1.0, 0.0)
    r = lax.broadcasted_iota(jnp.int32, (tm, tm), 0)
    c = lax.broadcasted_iota(jnp.int32, (tm, tm), 1)
    strict_lower = jnp.where(c < r, 1.0, 0.0).astype(BF16)
    before = jnp.dot(strict_lower, multi.astype(BF16), preferred_element_type=F32) + cnt_ref[...]
    cnt_ref[...] = cnt_ref[...] + jnp.sum(multi, axis=0, keepdims=True)

    meta = jnp.zeros((tm, LANES), F32)
    for kk in range(TOP_K):
        rank = jnp.sum(jnp.where(hots[kk], before, 0.0), axis=-1, keepdims=True)
        meta = jnp.where(lane == kk, idxs[kk], meta)
        meta = jnp.where(lane == TOP_K + kk, gates[kk], meta)
        meta = jnp.where(lane == 2 * TOP_K + kk, rank, meta)
    meta_ref[...] = meta


def _outproj(sb, ret, xt, w_out_bf, ln_g, ln_b, w_router, b_router, alpha, tm=512):
    t, d = xt.shape
    tm = min(tm, t)
    sbw = sb.shape[1]
    wr = jnp.zeros((d, LANES), BF16).at[:, :N_EXPERTS].set(w_router.astype(BF16))
    br = jnp.full((1, LANES), NEG_BIG, F32).at[0, :N_EXPERTS].set(b_router.astype(F32))
    row = lambda w: pl.BlockSpec((tm, w), lambda i: (i, 0))
    const = lambda r, c: pl.BlockSpec((r, c), lambda i: (0, 0))
    return pl.pallas_call(
        functools.partial(_outproj_kernel, alpha=alpha, sb_width=sbw),
        grid=(t // tm,),
        in_specs=[row(sbw), row(ret.shape[1]), row(d), const(d, d), const(1, d), const(1, d),
                  const(d, LANES), const(1, LANES)],
        out_specs=[row(d), pl.BlockSpec((tm * (d // (2 * LANES)), LANES), lambda i: (i, 0)), row(LANES),
                   const(1, LANES)],
        out_shape=[jax.ShapeDtypeStruct((t, d), F32),
                   jax.ShapeDtypeStruct((t * (d // (2 * LANES)), LANES), jnp.uint32),
                   jax.ShapeDtypeStruct((t, LANES), F32),
                   jax.ShapeDtypeStruct((1, LANES), F32)],
        compiler_params=_cparams(("arbitrary",)),
        name="outproj_ln_router",
    )(sb, ret, xt, w_out_bf, ln_g.reshape(1, d).astype(F32), ln_b.reshape(1, d).astype(F32), wr, br)


DMA_UNROLL = 8


def _wait_slabs(hbm, buf, sem, slot, count, chunks):
    def body(n, _):
        pltpu.make_async_copy(hbm.at[pl.ds(0, chunks), :], buf.at[slot, pl.ds(0, chunks), :],
                              sem.at[slot]).wait()
        return 0
    lax.fori_loop(0, count, body, 0, unroll=DMA_UNROLL)


def _moe_kernel(ie_ref, ij_ref, ist_ref, ins_ref, tail_ref, live_ref, dest_ref, x_hbm, wgu_hbm, wd_hbm, bg_ref,
                bu_ref, bd_ref, zeros_hbm, y_hbm, x_vmem, acc, wg_f, wu_f, wd_f, wgu_bf, wd_bf, stage, gbuf, tok_ref,
                sem_g, sem_out, sem_w, *, n_f):
    del ij_ref
    i = pl.program_id(0)
    j = pl.program_id(1)
    nsub = ins_ref[i]
    start = ist_ref[i]
    d = acc.shape[1]
    tf = wd_bf.shape[0]
    d_ff = n_f * tf
    y_pitch = d // LANES + SLAB_PAD
    block_rows = ROW_BLOCK * y_pitch

    def weight_copies(e, jj, slot):
        col_g = pl.ds(pl.multiple_of(jj * tf, tf), tf)
        col_u = pl.ds(pl.multiple_of(d_ff + jj * tf, tf), tf)
        copies = []
        for half in range(2):
            rows_k = pl.ds(half * (d // 2), d // 2)
            rows_f = pl.ds(pl.multiple_of(jj * tf + half * (tf // 2), tf // 2), tf // 2)
            dst_f = pl.ds(half * (tf // 2), tf // 2)
            copies += [
                pltpu.make_async_copy(wgu_hbm.at[e, rows_k, col_g], wg_f.at[slot, rows_k, :], sem_w.at[slot]),
                pltpu.make_async_copy(wgu_hbm.at[e, rows_k, col_u], wu_f.at[slot, rows_k, :], sem_w.at[slot]),
                pltpu.make_async_copy(wd_hbm.at[e, rows_f, :], wd_f.at[slot, dst_f, :], sem_w.at[slot]),
            ]
        return copies

    def start_weights(e, jj, slot):
        for n, c in enumerate(weight_copies(e, jj, slot)):
            c.start(priority=n % 2)

    step = i * n_f + j
    w_slot = step % 2

    @pl.when(step == 0)
    def _():
        start_weights(ie_ref[0], 0, 0)

    @pl.when(step + 1 < live_ref[0])
    def _():
        last = j == n_f - 1
        start_weights(ie_ref[jnp.where(last, i + 1, i)], jnp.where(last, 0, j + 1), 1 - w_slot)

    def rows_of(r):
        return pl.ds(pl.multiple_of(r * ROW_BLOCK, ROW_BLOCK), ROW_BLOCK)

    def slabs_of(block):
        return pl.ds(pl.multiple_of(block * block_rows, block_rows), block_rows)

    def for_each(lo, hi, fn):
        def body(r, _):
            fn(r)
            return 0
        lax.fori_loop(lo, hi, body, 0)

    @pl.when((i == 0) & (j == 0))
    def _():
        stage[...] = jnp.zeros(stage.shape, F32)

        def zero_copy(bk):
            return pltpu.make_async_copy(stage.at[0], y_hbm.at[slabs_of(bk), :], sem_out.at[0])

        n_blocks = y_hbm.shape[0] // block_rows
        for_each(tail_ref[0], n_blocks, lambda bk: zero_copy(bk).start())
        for_each(tail_ref[0], n_blocks, lambda bk: zero_copy(bk).wait())

    first_block = start // ROW_BLOCK
    words = d // (2 * LANES)

    pitch = words + 1

    def build_row_tokens():
        n_tok = dest_ref.shape[0] // TOP_K
        clear = pltpu.make_async_copy(zeros_hbm, tok_ref, sem_g)
        clear.start()
        clear.wait()

        def fill(m, _):
            for u in range(DMA_UNROLL):
                t = m * DMA_UNROLL + u
                for kk in range(TOP_K):
                    tok_ref[dest_ref[kk * n_tok + t]] = t
            return 0
        lax.fori_loop(0, n_tok // DMA_UNROLL, fill, 0)

    def start_gather(item):
        row0 = ist_ref[item]

        def body(m, _):
            for u in range(DMA_UNROLL):
                r = m * DMA_UNROLL + u
                src = pl.ds(pl.multiple_of(tok_ref[row0 + r] * words, words), words)
                pltpu.make_async_copy(x_hbm.at[src, :], gbuf.at[pl.ds(r * pitch, words), :], sem_g).start(
                    priority=u % 2)
            return 0
        lax.fori_loop(0, ins_ref[item] * (ROW_BLOCK // DMA_UNROLL), body, 0)

    def wait_gather(item):
        def body(m, _):
            for _u in range(DMA_UNROLL):
                pltpu.make_async_copy(x_hbm.at[pl.ds(0, words), :], gbuf.at[pl.ds(0, words), :], sem_g).wait()
            return 0
        lax.fori_loop(0, ins_ref[item] * (ROW_BLOCK // DMA_UNROLL), body, 0)

    def out_copy(r, slot):
        return pltpu.make_async_copy(stage.at[slot], y_hbm.at[slabs_of(first_block + r), :],
                                     sem_out.at[slot])

    def span(r, n_blocks):
        return pl.ds(pl.multiple_of(r * ROW_BLOCK, ROW_BLOCK), n_blocks * ROW_BLOCK)

    def hidden(rows):
        xb = x_vmem[rows, :]
        gate_up = jnp.dot(xb, wgu_bf[...], preferred_element_type=F32)
        gate = gate_up[:, :tf] + bg_ref[...]
        up = gate_up[:, tf:] + bu_ref[...]
        gate = jnp.minimum(gate, SWIGLU_LIMIT)
        up = jnp.clip(up, -SWIGLU_LIMIT, SWIGLU_LIMIT)
        act = (up + 1.0) * (gate * jax.nn.sigmoid(SWIGLU_ALPHA * gate))
        return jnp.dot(act.astype(BF16), wd_bf[...], preferred_element_type=F32)

    @pl.when(nsub > 0)
    def _():
        @pl.when(j == 0)
        def _():
            @pl.when(i == 0)
            def _():
                build_row_tokens()
                start_gather(0)
            wait_gather(i)
            bias_rows = jnp.broadcast_to(bd_ref[...], (ROW_BLOCK, d))

            def unpack(r):
                x_vmem[rows_of(r), :] = _from_packed_slabs(gbuf, r * (ROW_BLOCK * pitch), ROW_BLOCK, words, pitch)
                acc[rows_of(r), :] = bias_rows
            for_each(0, nsub, unpack)

            @pl.when(i + 1 < pl.num_programs(0))
            def _():
                start_gather(i + 1)

        for c in weight_copies(ie_ref[i], j, w_slot):
            c.wait()
        wgu_bf[:, :tf] = wg_f[w_slot].astype(BF16)
        wgu_bf[:, tf:] = wu_f[w_slot].astype(BF16)
        wd_bf[...] = wd_f[w_slot].astype(BF16)

        n_pairs = nsub // 2
        odd = nsub % 2 == 1

        @pl.when(j < n_f - 1)
        def _():
            def pair(p):
                acc[span(2 * p, 2), :] += hidden(span(2 * p, 2))

            def two_pairs(q):
                pair(2 * q)
                pair(2 * q + 1)
            for_each(0, n_pairs // 2, two_pairs)

            @pl.when(n_pairs % 2 == 1)
            def _():
                pair(n_pairs - 1)

            @pl.when(odd)
            def _():
                acc[span(nsub - 1, 1), :] += hidden(span(nsub - 1, 1))

        @pl.when(j == n_f - 1)
        def _():
            def write_out(r, slot, final, reuse):
                @pl.when(reuse)
                def _():
                    out_copy(r, slot).wait()
                _to_slabs(stage.at[slot], 0, ROW_BLOCK, final, y_pitch)
                out_copy(r, slot).start()

            def pair(p):
                final = acc[span(2 * p, 2), :] + hidden(span(2 * p, 2))
                write_out(2 * p, 0, final[:ROW_BLOCK], p > 0)
                write_out(2 * p + 1, 1, final[ROW_BLOCK:], p > 0)
            for_each(0, n_pairs, pair)

            @pl.when(odd)
            def _():
                final = acc[span(nsub - 1, 1), :] + hidden(span(nsub - 1, 1))
                write_out(nsub - 1, 0, final, n_pairs > 0)

            out_copy(0, 0).wait()

            @pl.when(n_pairs > 0)
            def _():
                out_copy(0, 1).wait()


def _moe_ffn(item_e, item_j, item_start, item_nsub, tail_block, live_steps, dest_flat, x1_packed, n_rows,
             w_gate_up, b_gate_up, w_down, b_down, r_max, tf=512):
    n_e, d, two_f = w_gate_up.shape
    chunks = d // LANES
    words = d // (2 * LANES)
    d_ff = two_f // 2
    n_f = d_ff // tf
    n_items = item_e.shape[0]

    def jf(i, j, ij):
        return jnp.where(ij[i] < 0, j, ij[i])

    hbm = pl.BlockSpec(memory_space=pl.ANY)
    in_specs = [
        hbm, hbm, hbm,
        pl.BlockSpec((None, 1, tf), lambda i, j, ie, ij, ist, ins, tl, lv, tk: (ie[i], 0, jf(i, j, ij))),
        pl.BlockSpec((None, 1, tf), lambda i, j, ie, ij, ist, ins, tl, lv, tk: (ie[i], 0, n_f + jf(i, j, ij))),
        pl.BlockSpec((None, 1, d), lambda i, j, ie, ij, ist, ins, tl, lv, tk: (ie[i], 0, 0)),
        hbm,
    ]
    return pl.pallas_call(
        functools.partial(_moe_kernel, n_f=n_f),
        grid_spec=pltpu.PrefetchScalarGridSpec(
            num_scalar_prefetch=7,
            grid=(n_items, n_f),
            in_specs=in_specs,
            out_specs=pl.BlockSpec(memory_space=pl.ANY),
            scratch_shapes=[pltpu.VMEM((r_max, d), BF16), pltpu.VMEM((r_max, d), F32),
                            pltpu.VMEM((2, d, tf), F32), pltpu.VMEM((2, d, tf), F32),
                            pltpu.VMEM((2, tf, d), F32),
                            pltpu.VMEM((d, 2 * tf), BF16), pltpu.VMEM((tf, d), BF16),
                            pltpu.VMEM((2, ROW_BLOCK * (chunks + SLAB_PAD), LANES), F32),
                            pltpu.VMEM((r_max * (words + 1), LANES), jnp.uint32),
                            pltpu.SMEM((n_rows,), jnp.int32),
                            pltpu.SemaphoreType.DMA(()), pltpu.SemaphoreType.DMA((2,)),
                            pltpu.SemaphoreType.DMA((2,))],
        ),
        out_shape=jax.ShapeDtypeStruct((n_rows * (chunks + SLAB_PAD), LANES), F32),
        compiler_params=_cparams(("arbitrary", "arbitrary")),
        name="moe_ffn",
    )(item_e, item_j, item_start, item_nsub, tail_block, live_steps, dest_flat, x1_packed, w_gate_up, w_down,
      b_gate_up.reshape(n_e, 1, two_f), b_gate_up.reshape(n_e, 1, two_f), b_down.reshape(n_e, 1, d),
      jnp.zeros((n_rows,), jnp.int32))


def _combine_kernel(dest_ref, y_hbm, x1_ref, meta_ref, g_ref, b_ref, o_ref, buf, sem, *, alpha, tm, chunks):
    i = pl.program_id(0)
    per_step = tm * TOP_K
    n_tok = dest_ref.shape[0] // TOP_K
    pitch = chunks + 1

    def issue_tile(step, slot):
        def body(tt, _):
            for kk in range(TOP_K):
                src = pl.ds(dest_ref[kk * n_tok + step * tm + tt] * (chunks + SLAB_PAD), chunks)
                dst = pl.ds((kk * tm + tt) * pitch, chunks)
                pltpu.make_async_copy(y_hbm.at[src, :], buf.at[slot, dst, :], sem.at[slot]).start(
                    priority=kk % 2)
            return 0
        lax.fori_loop(0, tm, body, 0, unroll=DMA_UNROLL // TOP_K)

    @pl.when(i == 0)
    def _():
        issue_tile(0, 0)

    @pl.when(i + 1 < pl.num_programs(0))
    def _():
        issue_tile(i + 1, (i + 1) % 2)

    slot = i % 2
    _wait_slabs(y_hbm, buf, sem, slot, per_step, chunks)
    meta = meta_ref[...]
    y = jnp.zeros(x1_ref.shape, F32)
    for kk in range(TOP_K):
        rows = jnp.concatenate(
            [buf[slot, pl.ds(kk * tm * pitch + c, tm, stride=pitch), :] for c in range(chunks)], axis=1)
        y = y + meta[:, TOP_K + kk:TOP_K + kk + 1] * rows
    o_ref[...] = _layer_norm(alpha * x1_ref[...] + y, g_ref[...], b_ref[...])


def _combine(dest_flat, y_rows, x1, meta, ln_g, ln_b, alpha, tm=256):
    t, d = x1.shape
    tm = min(tm, t)
    chunks = d // LANES
    row = lambda w: pl.BlockSpec((tm, w), lambda i, dest: (i, 0))
    const = pl.BlockSpec((1, d), lambda i, dest: (0, 0))
    return pl.pallas_call(
        functools.partial(_combine_kernel, alpha=alpha, tm=tm, chunks=chunks),
        grid_spec=pltpu.PrefetchScalarGridSpec(
            num_scalar_prefetch=1,
            grid=(t // tm,),
            in_specs=[pl.BlockSpec(memory_space=pl.ANY), row(d), row(LANES), const, const],
            out_specs=row(d),
            scratch_shapes=[pltpu.VMEM((2, TOP_K * tm * (chunks + 1), LANES), F32),
                            pltpu.SemaphoreType.DMA((2,))],
        ),
        out_shape=jax.ShapeDtypeStruct((t, d), F32),
        compiler_params=_cparams(("arbitrary",)),
        name="combine_ln",
    )(dest_flat, y_rows, x1, meta, ln_g.reshape(1, d).astype(F32), ln_b.reshape(1, d).astype(F32))


def _routing_plan(meta, counts_f, t, r_max):
    meta_t = meta.T
    idx = meta_t[0:TOP_K].astype(jnp.int32)
    rank = meta_t[2 * TOP_K:3 * TOP_K].astype(jnp.int32)
    counts = counts_f[0, :N_EXPERTS].astype(jnp.int32)
    n128 = (counts + ROW_BLOCK - 1) // ROW_BLOCK
    padded = n128 * ROW_BLOCK
    pad_start = jnp.cumsum(padded) - padded
    experts = jnp.arange(N_EXPERTS, dtype=jnp.int32)[:, None, None]
    region = jnp.sum(jnp.where(idx[None] == experts, pad_start[:, None, None], 0), axis=0)
    dest = region + rank
    tk = t * TOP_K
    n_rows = (tk + ROW_BLOCK - 1) // ROW_BLOCK * ROW_BLOCK + N_EXPERTS * ROW_BLOCK

    subs = r_max // ROW_BLOCK
    n_items = N_EXPERTS + (n_rows // ROW_BLOCK - N_EXPERTS) // subs
    items_e = (n128 + subs - 1) // subs
    items_end = jnp.cumsum(items_e)
    total = items_end[-1]
    slot = jnp.arange(n_items, dtype=jnp.int32)
    live = slot < total
    s_eff = jnp.minimum(slot, total - 1)
    e = jnp.minimum(jnp.searchsorted(items_end, s_eff, side='right'), N_EXPERTS - 1).astype(jnp.int32)
    local = s_eff - (items_end[e] - items_e[e])
    item_start = (pad_start[e] + local * r_max).astype(jnp.int32)
    item_nsub = jnp.where(live, jnp.clip(n128[e] - local * subs, 0, subs), 0).astype(jnp.int32)
    tail_block = jnp.sum(n128).reshape(1).astype(jnp.int32)
    return dest.reshape(tk), n_rows, e, live, item_start, item_nsub, tail_block


def kernel(x, w_in, ret_gn_gain, w_out, ln1_gain, ln1_bias, w_router, b_router, w_gate_up, b_gate_up,
           w_down, b_down, ln2_gain, ln2_bias):
    b, s, d = x.shape
    t = b * s
    depth = w_in.shape[0]
    alpha = (2 * depth) ** 0.25
    r_max = 1280
    tf = 512
    n_f = (w_gate_up.shape[-1] // 2) // tf
    xt = x.reshape(t, d)
    for layer in range(depth):
        proj = _in_proj(xt, w_in[layer].astype(BF16))
        sb = _stickbreak(proj, b, s)
        ret = _retention(proj, ret_gn_gain[layer], b, s)
        x1, x1_slabs, meta, counts = _outproj(sb, ret, xt, w_out[layer].astype(BF16), ln1_gain[layer],
                                              ln1_bias[layer], w_router[layer], b_router[layer], alpha)
        dest_flat, n_rows, item_e, live, item_start, item_nsub, tail_block = _routing_plan(
            meta, counts, t, r_max)
        item_j = jnp.where(live, -1, n_f - 1).astype(jnp.int32)
        live_steps = (jnp.sum(live.astype(jnp.int32)) * n_f).reshape(1)
        y_rows = _moe_ffn(item_e, item_j, item_start, item_nsub, tail_block, live_steps, dest_flat, x1_slabs,
                          n_rows, w_gate_up[layer], b_gate_up[layer], w_down[layer], b_down[layer], r_max, tf)
        xt = _combine(dest_flat, y_rows, x1, meta, ln2_gain[layer], ln2_bias[layer], alpha)
    return xt.reshape(b, s, d)
```

```python
import functools
import math

import jax
import jax.numpy as jnp
from jax import lax
from jax.experimental import pallas as pl
from jax.experimental.pallas import tpu as pltpu

F32 = jnp.float32
BF16 = jnp.bfloat16

HEAD_DIM = 128
SB_HEADS = 8
RET_HEADS = 8
CHUNK = 64
ROPE_BASE = 10000.0
N_EXPERTS = 32
TOP_K = 4
SWIGLU_LIMIT = 7.0
SWIGLU_ALPHA = 1.702
LN_EPS = 1e-5
GN_EPS = 1e-5

V7X_VMEM_LIMIT_BYTES = 56 * 1024 * 1024
LANES = 128
ROW_BLOCK = 128
NEG_BIG = -1e30
EXP_UNDERFLOW = -105.0


def _cparams(sem, vmem=V7X_VMEM_LIMIT_BYTES):
    return pltpu.CompilerParams(dimension_semantics=sem, vmem_limit_bytes=vmem)


def _matmul_kernel(x_ref, w_ref, o_ref, x_bf):
    @pl.when(pl.program_id(1) == 0)
    def _():
        x_bf[...] = x_ref[...].astype(BF16)
    o_ref[...] = jnp.dot(x_bf[...], w_ref[...], preferred_element_type=F32).astype(o_ref.dtype)


def _in_proj(x, w_bf, tm=1024, tn=1792):
    t, d = x.shape
    n = w_bf.shape[1]
    tm = min(tm, t)
    return pl.pallas_call(
        _matmul_kernel,
        grid=(t // tm, n // tn),
        in_specs=[pl.BlockSpec((tm, d), lambda i, j: (i, 0)),
                  pl.BlockSpec((d, tn), lambda i, j: (0, j))],
        out_specs=pl.BlockSpec((tm, tn), lambda i, j: (i, j)),
        out_shape=jax.ShapeDtypeStruct((t, n), BF16),
        scratch_shapes=[pltpu.VMEM((tm, d), BF16)],
        compiler_params=_cparams(("arbitrary", "arbitrary")),
        name="in_proj",
    )(x, w_bf)


def _sb_kernel(q_ref, k_ref, v_ref, o_ref, *, seq, scale, group, sub):
    blk = 128
    nq = seq // blk
    row = lax.broadcasted_iota(jnp.int32, (blk, blk), 0)
    col = lax.broadcasted_iota(jnp.int32, (blk, blk), 1)
    causal = col < row
    r2 = lax.broadcasted_iota(jnp.int32, (2 * blk, 2 * blk), 0)
    c2 = lax.broadcasted_iota(jnp.int32, (2 * blk, 2 * blk), 1)
    cum_rhs = jnp.where((c2 >= blk) | ((r2 & (blk - 1)) >= c2), 1.0, 0.0).astype(BF16)

    n_sub = group // sub
    causal_sub = jnp.concatenate([causal] * sub, axis=0)
    heads_of = lambda s: range(s * sub, (s + 1) * sub)
    cols = lambda g: slice(g * blk, (g + 1) * blk)

    def block(qbs, kb, carries, accs, masked):
        ks = pl.ds(pl.multiple_of(kb * blk, blk), blk)
        zs = [jnp.concatenate(
            [lax.dot_general(qbs[g], k_ref[ks, cols(g)], (((1,), (1,)), ((), ())),
                             preferred_element_type=F32) for g in heads_of(s)], axis=0) * scale
              for s in range(n_sub)]
        sums = []
        for z in zs:
            neg_z = -z
            lnb = jnp.minimum(neg_z, 0.0) - jnp.log(1.0 + jnp.exp(jnp.minimum(z, neg_z)))
            if masked:
                lnb = jnp.where(causal_sub, lnb, 0.0)
            hi = lnb.astype(BF16)
            lo = (lnb - hi.astype(F32)).astype(BF16)
            sums.append(jnp.dot(jnp.concatenate([hi, lo], axis=1), cum_rhs, preferred_element_type=F32))
        new_carries, new_accs = [], []
        for s in range(n_sub):
            incl = sums[s][:, :blk]
            total = sums[s][:, blk:]
            w = jnp.exp(zs[s] + incl + carries[s])
            if masked:
                w = jnp.where(causal_sub, w, 0.0)
            w = w.astype(BF16)
            for n, g in enumerate(heads_of(s)):
                new_accs.append(accs[g] + jnp.dot(w[n * blk:(n + 1) * blk], v_ref[ks, cols(g)],
                                                  preferred_element_type=F32))
            new_carries.append(carries[s] + total)
        return tuple(new_carries), tuple(new_accs)

    def q_body(qi, _):
        qs = pl.ds(pl.multiple_of(qi * blk, blk), blk)
        qbs = [q_ref[qs, cols(g)] for g in range(group)]
        state = block(qbs, qi, (jnp.zeros((sub * blk, blk), F32),) * n_sub,
                      (jnp.zeros((blk, blk), F32),) * group, True)

        def some_weight_left(carries):
            top = functools.reduce(jnp.maximum, carries)
            return (jnp.max(top) > EXP_UNDERFLOW).astype(jnp.int32)

        def live(st):
            t, more, _, _ = st
            return jnp.logical_and(t < qi, more > 0)

        def kb_body(st):
            t, _, carries, accs = st
            carries, accs = block(qbs, qi - 1 - t, carries, accs, False)
            return t + 1, some_weight_left(carries), carries, accs

        _, _, _, accs = lax.while_loop(live, kb_body, (0, 1, state[0], state[1]))
        for g in range(group):
            o_ref[qs, cols(g)] = accs[g].astype(o_ref.dtype)
        return 0

    lax.fori_loop(0, nq, q_body, 0)


def _stickbreak(proj, batch, seq, group=8, sub=4):
    t = batch * seq
    h = SB_HEADS
    n_groups = h // group
    width = group * HEAD_DIM
    kern = functools.partial(_sb_kernel, seq=seq, scale=1.0 / math.sqrt(HEAD_DIM), group=group,
                             sub=sub)
    spec = lambda off: pl.BlockSpec((seq, width), lambda b, hg: (b, off * n_groups + hg))
    return pl.pallas_call(
        kern,
        grid=(batch, n_groups),
        in_specs=[spec(0), spec(1), spec(2)],
        out_specs=pl.BlockSpec((seq, width), lambda b, hg: (b, hg)),
        out_shape=jax.ShapeDtypeStruct((t, h * HEAD_DIM), BF16),
        compiler_params=_cparams(("arbitrary", "arbitrary")),
        name="stickbreak",
    )(proj, proj, proj)


def _ret_kernel(q_ref, k_ref, v_ref, g_ref, cos_ref, sin_ref, intra_ref, qdec_ref, kdec_ref,
                cdec_ref, gain_ref, o_ref, *, seq, heads):
    n_chunks = seq // CHUNK
    half = HEAD_DIM // 2
    k_scale = HEAD_DIM ** -0.5
    cols = lambda h: slice(h * HEAD_DIM, (h + 1) * HEAD_DIM)
    contract_last = (((1,), (1,)), ((), ()))
    contract_rows = (((0,), (0,)), ((), ()))

    def chunk(n, states):
        rs = pl.ds(pl.multiple_of(n * CHUNK, CHUNK), CHUNK)
        cos = cos_ref[rs, :]
        sin = sin_ref[rs, :]
        qrs, krs, crosses, kvs = [], [], [], []
        for h in range(heads):
            q = q_ref[rs, cols(h)].astype(F32)
            k = k_ref[rs, cols(h)].astype(F32)
            qr = q * cos + pltpu.roll(q, half, 1) * sin
            kr = (k * cos + pltpu.roll(k, half, 1) * sin) * k_scale
            qrs.append(qr.astype(BF16))
            krs.append(kr.astype(BF16))
            crosses.append(jnp.dot((qr * qdec_ref[h]).astype(BF16), states[h].astype(BF16),
                                   preferred_element_type=F32))
            kvs.append(lax.dot_general((kr * kdec_ref[h]).astype(BF16), v_ref[rs, cols(h)], contract_rows,
                                       preferred_element_type=F32))
        scores = [lax.dot_general(qrs[h], krs[h], contract_last, preferred_element_type=F32) * intra_ref[h]
                  for h in range(heads)]
        outs = [crosses[h] + jnp.dot(scores[h].astype(BF16), v_ref[rs, cols(h)], preferred_element_type=F32)
                for h in range(heads)]
        new_states = []
        for h in range(heads):
            o = outs[h]
            mu = jnp.mean(o, axis=-1, keepdims=True)
            var = jnp.mean(jnp.square(o - mu), axis=-1, keepdims=True)
            on = (o - mu) * lax.rsqrt(var + GN_EPS)
            g = g_ref[rs, cols(h)].astype(F32)
            out = on * gain_ref[:, cols(h)] * (g * jax.nn.sigmoid(g))
            o_ref[rs, cols(h)] = out.astype(o_ref.dtype)
            new_states.append(states[h] * cdec_ref[h] + kvs[h])
        return tuple(new_states)

    lax.fori_loop(0, n_chunks, chunk, (jnp.zeros((HEAD_DIM, HEAD_DIM), F32),) * heads)


def _retention_tables(seq):
    d = HEAD_DIM
    inv_freq = ROPE_BASE ** (-jnp.arange(0, d, 2, dtype=F32) / d)
    ang = jnp.arange(seq, dtype=F32)[:, None] * inv_freq[None, :]
    cos, sin = jnp.cos(ang), jnp.sin(ang)
    cos_full = jnp.concatenate([cos, cos], axis=-1)
    sin_signed = jnp.concatenate([-sin, sin], axis=-1)
    log_gamma = jnp.log1p(-jnp.exp2(-5.0 - jnp.arange(RET_HEADS, dtype=F32)))
    i = jnp.arange(CHUNK, dtype=F32)
    intra = jnp.exp(log_gamma[:, None, None] * jnp.abs(i[:, None] - i[None, :]))
    k_decay = jnp.exp(log_gamma[:, None] * (CHUNK - 1 - i))
    q_decay = jnp.exp(log_gamma[:, None] * (i + 1.0))
    c_decay = jnp.exp(log_gamma * CHUNK)
    bc = lambda a: jnp.broadcast_to(a[..., None], a.shape + (d,))
    return cos_full, sin_signed, intra, bc(q_decay), bc(k_decay), bc(c_decay[:, None])


def _retention(proj, gn_gain, batch, seq):
    t = batch * seq
    h = RET_HEADS
    base = 3 * SB_HEADS
    cos_full, sin_signed, intra, qdec, kdec, cdec = _retention_tables(seq)
    width = h * HEAD_DIM
    group0 = base // h
    spec = lambda off: pl.BlockSpec((seq, width), lambda b: (b, group0 + off))
    full = pl.BlockSpec((seq, HEAD_DIM), lambda b: (0, 0))
    table = lambda r, c: pl.BlockSpec((h, r, c), lambda b: (0, 0, 0))
    return pl.pallas_call(
        functools.partial(_ret_kernel, seq=seq, heads=h),
        grid=(batch,),
        in_specs=[spec(0), spec(1), spec(2), spec(3), full, full,
                  table(CHUNK, CHUNK), table(CHUNK, HEAD_DIM), table(CHUNK, HEAD_DIM),
                  table(1, HEAD_DIM),
                  pl.BlockSpec((1, width), lambda b: (0, 0))],
        out_specs=pl.BlockSpec((seq, width), lambda b: (b, 0)),
        out_shape=jax.ShapeDtypeStruct((t, width), BF16),
        compiler_params=_cparams(("arbitrary",)),
        name="retention",
    )(proj, proj, proj, proj, cos_full, sin_signed, intra, qdec, kdec, cdec,
      gn_gain.reshape(1, h * HEAD_DIM).astype(F32))


def _layer_norm(hid, gain, bias):
    mu = jnp.mean(hid, axis=-1, keepdims=True)
    cen = hid - mu
    var = jnp.mean(jnp.square(cen), axis=-1, keepdims=True)
    return cen * lax.rsqrt(var + LN_EPS) * gain + bias


SLAB_PAD = 4


def _to_slabs(slab_ref, base, rows, value, pitch):
    for c in range(value.shape[1] // LANES):
        slab_ref[pl.ds(base + c, rows, stride=pitch), :] = value[:, c * LANES:(c + 1) * LANES]


def _bf16_bits(x):
    u = pltpu.bitcast(x, jnp.uint32)
    return lax.shift_right_logical(u + jnp.uint32(0x7FFF) + (lax.shift_right_logical(u, jnp.uint32(16))
                                                             & jnp.uint32(1)), jnp.uint32(16))


def _to_packed_slabs(slab_ref, rows, value):
    words = value.shape[1] // (2 * LANES)
    for c in range(words):
        lo = _bf16_bits(value[:, (2 * c) * LANES:(2 * c + 1) * LANES])
        hi = _bf16_bits(value[:, (2 * c + 1) * LANES:(2 * c + 2) * LANES])
        slab_ref[pl.ds(c, rows, stride=words), :] = lo | lax.shift_left(hi, jnp.uint32(16))


def _from_packed_slabs(slab_ref, base, rows, words, pitch):
    out = []
    for c in range(words):
        w = slab_ref[pl.ds(base + c, rows, stride=pitch), :]
        out.append(pltpu.bitcast(lax.shift_left(w, jnp.uint32(16)), F32).astype(BF16))
        out.append(pltpu.bitcast(w & jnp.uint32(0xFFFF0000), F32).astype(BF16))
    return jnp.concatenate(out, axis=1)


def _outproj_kernel(sb_ref, ret_ref, x_ref, w_ref, g_ref, b_ref, wr_ref, br_ref,
                    x1_ref, x1s_ref, meta_ref, cnt_ref, *, alpha, sb_width):
    tm = x_ref.shape[0]
    mix = jnp.dot(sb_ref[...], w_ref[:sb_width, :], preferred_element_type=F32)
    mix = mix + jnp.dot(ret_ref[...], w_ref[sb_width:, :], preferred_element_type=F32)
    x1 = _layer_norm(alpha * x_ref[...] + mix, g_ref[...], b_ref[...])
    x1_ref[...] = x1
    _to_packed_slabs(x1s_ref, tm, x1)

    logits = jnp.dot(x1.astype(BF16), wr_ref[...], preferred_element_type=F32) + br_ref[...]
    lane = lax.broadcasted_iota(jnp.int32, (tm, LANES), 1).astype(F32)
    vals = logits
    tops, idxs, hots = [], [], []
    for _ in range(TOP_K):
        m = jnp.max(vals, axis=-1, keepdims=True)
        idx = jnp.min(jnp.where(vals == m, lane, float(LANES)), axis=-1, keepdims=True)
        hot = lane == idx
        vals = jnp.where(hot, NEG_BIG * 2.0, vals)
        tops.append(m)
        idxs.append(idx)
        hots.append(hot)
    exps = [jnp.exp(m - tops[0]) for m in tops]
    denom = exps[0] + exps[1] + exps[2] + exps[3]
    gates = [e / denom for e in exps]

    @pl.when(pl.program_id(0) == 0)
    def _():
        cnt_ref[...] = jnp.zeros_like(cnt_ref)

    multi = jnp.zeros((tm, LANES), F32)
    for hot in hots:
        multi = multi + jnp.where(hot, 1.0, 0.0)
    r = lax.broadcasted_iota(jnp.int32, (tm, tm), 0)
    c = lax.broadcasted_iota(jnp.int32, (tm, tm), 1)
    strict_lower = jnp.where(c < r, 1.0, 0.0).astype(BF16)
    before = jnp.dot(strict_lower, multi.astype(BF16), preferred_element_type=F32) + cnt_ref[...]
    cnt_ref[...] = cnt_ref[...] + jnp.sum(multi, axis=0, keepdims=True)

    meta = jnp.zeros((tm, LANES), F32)
    for kk in range(TOP_K):
        rank = jnp.sum(jnp.where(hots[kk], before, 0.0), axis=-1, keepdims=True)
        meta = jnp.where(lane == kk, idxs[kk], meta)
        meta = jnp.where(lane == TOP_K + kk, gates[kk], meta)
        meta = jnp.where(lane == 2 * TOP_K + kk, rank, meta)
    meta_ref[...] = meta


def _outproj(sb, ret, xt, w_out_bf, ln_g, ln_b, w_router, b_router, alpha, tm=512):
    t, d = xt.shape
    tm = min(tm, t)
    sbw = sb.shape[1]
    wr = jnp.zeros((d, LANES), BF16).at[:, :N_EXPERTS].set(w_router.astype(BF16))
    br = jnp.full((1, LANES), NEG_BIG, F32).at[0, :N_EXPERTS].set(b_router.astype(F32))
    row = lambda w: pl.BlockSpec((tm, w), lambda i: (i, 0))
    const = lambda r, c: pl.BlockSpec((r, c), lambda i: (0, 0))
    return pl.pallas_call(
        functools.partial(_outproj_kernel, alpha=alpha, sb_width=sbw),
        grid=(t // tm,),
        in_specs=[row(sbw), row(ret.shape[1]), row(d), const(d, d), const(1, d), const(1, d),
                  const(d, LANES), const(1, LANES)],
        out_specs=[row(d), pl.BlockSpec((tm * (d // (2 * LANES)), LANES), lambda i: (i, 0)), row(LANES),
                   const(1, LANES)],
        out_shape=[jax.ShapeDtypeStruct((t, d), F32),
                   jax.ShapeDtypeStruct((t * (d // (2 * LANES)), LANES), jnp.uint32),
                   jax.ShapeDtypeStruct((t, LANES), F32),
                   jax.ShapeDtypeStruct((1, LANES), F32)],
        compiler_params=_cparams(("arbitrary",)),
        name="outproj_ln_router",
    )(sb, ret, xt, w_out_bf, ln_g.reshape(1, d).astype(F32), ln_b.reshape(1, d).astype(F32), wr, br)


DMA_UNROLL = 8
PAIRS_PER_TRIP = 4


def _wait_slabs(hbm, buf, sem, slot, count, chunks):
    def body(n, _):
        pltpu.make_async_copy(hbm.at[pl.ds(0, chunks), :], buf.at[slot, pl.ds(0, chunks), :],
                              sem.at[slot]).wait()
        return 0
    lax.fori_loop(0, count, body, 0, unroll=DMA_UNROLL)


def _moe_kernel(ie_ref, ij_ref, ist_ref, ins_ref, tail_ref, live_ref, dest_ref, x_hbm, wgu_hbm, wd_hbm, bg_ref,
                bu_ref, bd_ref, zeros_hbm, y_hbm, x_vmem, acc, wg_f, wu_f, wd_f, wgu_bf, wd_bf, stage, gbuf, tok_ref,
                sem_g, sem_out, sem_w, *, n_f):
    del ij_ref
    i = pl.program_id(0)
    j = pl.program_id(1)
    nsub = ins_ref[i]
    start = ist_ref[i]
    d = acc.shape[1]
    tf = wd_bf.shape[0]
    d_ff = n_f * tf
    y_pitch = d // LANES + SLAB_PAD
    block_rows = ROW_BLOCK * y_pitch

    def weight_copies(e, jj, slot):
        col_g = pl.ds(pl.multiple_of(jj * tf, tf), tf)
        col_u = pl.ds(pl.multiple_of(d_ff + jj * tf, tf), tf)
        copies = []
        for half in range(2):
            rows_k = pl.ds(half * (d // 2), d // 2)
            rows_f = pl.ds(pl.multiple_of(jj * tf + half * (tf // 2), tf // 2), tf // 2)
            dst_f = pl.ds(half * (tf // 2), tf // 2)
            copies += [
                pltpu.make_async_copy(wgu_hbm.at[e, rows_k, col_g], wg_f.at[slot, rows_k, :], sem_w.at[slot]),
                pltpu.make_async_copy(wgu_hbm.at[e, rows_k, col_u], wu_f.at[slot, rows_k, :], sem_w.at[slot]),
                pltpu.make_async_copy(wd_hbm.at[e, rows_f, :], wd_f.at[slot, dst_f, :], sem_w.at[slot]),
            ]
        return copies

    def start_weights(e, jj, slot):
        for n, c in enumerate(weight_copies(e, jj, slot)):
            c.start(priority=n % 2)

    step = i * n_f + j
    w_slot = step % 2

    @pl.when(step == 0)
    def _():
        start_weights(ie_ref[0], 0, 0)

    @pl.when(step + 1 < live_ref[0])
    def _():
        last = j == n_f - 1
        start_weights(ie_ref[jnp.where(last, i + 1, i)], jnp.where(last, 0, j + 1), 1 - w_slot)

    def rows_of(r):
        return pl.ds(pl.multiple_of(r * ROW_BLOCK, ROW_BLOCK), ROW_BLOCK)

    def slabs_of(block):
        return pl.ds(pl.multiple_of(block * block_rows, block_rows), block_rows)

    def for_each(lo, hi, fn):
        def body(r, _):
            fn(r)
            return 0
        lax.fori_loop(lo, hi, body, 0)

    @pl.when((i == 0) & (j == 0))
    def _():
        stage[...] = jnp.zeros(stage.shape, F32)

        def zero_copy(bk):
            return pltpu.make_async_copy(stage.at[0], y_hbm.at[slabs_of(bk), :], sem_out.at[0])

        n_blocks = y_hbm.shape[0] // block_rows
        for_each(tail_ref[0], n_blocks, lambda bk: zero_copy(bk).start())
        for_each(tail_ref[0], n_blocks, lambda bk: zero_copy(bk).wait())

    first_block = start // ROW_BLOCK
    words = d // (2 * LANES)

    pitch = words + 1

    def build_row_tokens():
        n_tok = dest_ref.shape[0] // TOP_K
        clear = pltpu.make_async_copy(zeros_hbm, tok_ref, sem_g)
        clear.start()
        clear.wait()

        def fill(m, _):
            for u in range(DMA_UNROLL):
                t = m * DMA_UNROLL + u
                for kk in range(TOP_K):
                    tok_ref[dest_ref[kk * n_tok + t]] = t
            return 0
        lax.fori_loop(0, n_tok // DMA_UNROLL, fill, 0)

    def start_gather(item):
        row0 = ist_ref[item]

        def body(m, _):
            for u in range(DMA_UNROLL):
                r = m * DMA_UNROLL + u
                src = pl.ds(pl.multiple_of(tok_ref[row0 + r] * words, words), words)
                pltpu.make_async_copy(x_hbm.at[src, :], gbuf.at[pl.ds(r * pitch, words), :], sem_g).start(
                    priority=u % 2)
            return 0
        lax.fori_loop(0, ins_ref[item] * (ROW_BLOCK // DMA_UNROLL), body, 0)

    def wait_gather(item):
        def body(m, _):
            for _u in range(DMA_UNROLL):
                pltpu.make_async_copy(x_hbm.at[pl.ds(0, words), :], gbuf.at[pl.ds(0, words), :], sem_g).wait()
            return 0
        lax.fori_loop(0, ins_ref[item] * (ROW_BLOCK // DMA_UNROLL), body, 0)

    def out_copy(r, slot):
        return pltpu.make_async_copy(stage.at[slot], y_hbm.at[slabs_of(first_block + r), :],
                                     sem_out.at[slot])

    def span(r, n_blocks):
        return pl.ds(pl.multiple_of(r * ROW_BLOCK, ROW_BLOCK), n_blocks * ROW_BLOCK)

    def hidden(rows):
        xb = x_vmem[rows, :]
        gate_up = jnp.dot(xb, wgu_bf[...], preferred_element_type=F32)
        gate = gate_up[:, :tf] + bg_ref[...]
        up = gate_up[:, tf:] + bu_ref[...]
        gate = jnp.minimum(gate, SWIGLU_LIMIT)
        up = jnp.clip(up, -SWIGLU_LIMIT, SWIGLU_LIMIT)
        act = (up + 1.0) * (gate * jax.nn.sigmoid(SWIGLU_ALPHA * gate))
        return jnp.dot(act.astype(BF16), wd_bf[...], preferred_element_type=F32)

    @pl.when(nsub > 0)
    def _():
        @pl.when(j == 0)
        def _():
            @pl.when(i == 0)
            def _():
                build_row_tokens()
                start_gather(0)
            wait_gather(i)
            bias_rows = jnp.broadcast_to(bd_ref[...], (ROW_BLOCK, d))

            def unpack(r):
                x_vmem[rows_of(r), :] = _from_packed_slabs(gbuf, r * (ROW_BLOCK * pitch), ROW_BLOCK, words, pitch)
                acc[rows_of(r), :] = bias_rows
            for_each(0, nsub, unpack)

            @pl.when(i + 1 < pl.num_programs(0))
            def _():
                start_gather(i + 1)

        for c in weight_copies(ie_ref[i], j, w_slot):
            c.wait()
        wgu_bf[:, :tf] = wg_f[w_slot].astype(BF16)
        wgu_bf[:, tf:] = wu_f[w_slot].astype(BF16)
        wd_bf[...] = wd_f[w_slot].astype(BF16)

        n_pairs = nsub // 2
        odd = nsub % 2 == 1

        def run_pairs(pair, group):
            def trip(q):
                for u in range(group):
                    pair(q * group + u)
            for_each(0, n_pairs // group, trip)
            for_each((n_pairs // group) * group, n_pairs, pair)

        @pl.when(j < n_f - 1)
        def _():
            def pair(p):
                acc[span(2 * p, 2), :] += hidden(span(2 * p, 2))
            run_pairs(pair, PAIRS_PER_TRIP)

            @pl.when(odd)
            def _():
                acc[span(nsub - 1, 1), :] += hidden(span(nsub - 1, 1))

        @pl.when(j == n_f - 1)
        def _():
            def write_out(r, slot, final, reuse):
                @pl.when(reuse)
                def _():
                    out_copy(r, slot).wait()
                _to_slabs(stage.at[slot], 0, ROW_BLOCK, final, y_pitch)
                out_copy(r, slot).start()

            def pair(p):
                final = acc[span(2 * p, 2), :] + hidden(span(2 * p, 2))
                write_out(2 * p, 0, final[:ROW_BLOCK], p > 0)
                write_out(2 * p + 1, 1, final[ROW_BLOCK:], p > 0)
            run_pairs(pair, 2)

            @pl.when(odd)
            def _():
                final = acc[span(nsub - 1, 1), :] + hidden(span(nsub - 1, 1))
                write_out(nsub - 1, 0, final, n_pairs > 0)

            out_copy(0, 0).wait()

            @pl.when(n_pairs > 0)
            def _():
                out_copy(0, 1).wait()


def _moe_ffn(item_e, item_j, item_start, item_nsub, tail_block, live_steps, dest_flat, x1_packed, n_rows,
             w_gate_up, b_gate_up, w_down, b_down, r_max, tf=512):
    n_e, d, two_f = w_gate_up.shape
    chunks = d // LANES
    words = d // (2 * LANES)
    d_ff = two_f // 2
    n_f = d_ff // tf
    n_items = item_e.shape[0]

    def jf(i, j, ij):
        return jnp.where(ij[i] < 0, j, ij[i])

    hbm = pl.BlockSpec(memory_space=pl.ANY)
    in_specs = [
        hbm, hbm, hbm,
        pl.BlockSpec((None, 1, tf), lambda i, j, ie, ij, ist, ins, tl, lv, tk: (ie[i], 0, jf(i, j, ij))),
        pl.BlockSpec((None, 1, tf), lambda i, j, ie, ij, ist, ins, tl, lv, tk: (ie[i], 0, n_f + jf(i, j, ij))),
        pl.BlockSpec((None, 1, d), lambda i, j, ie, ij, ist, ins, tl, lv, tk: (ie[i], 0, 0)),
        hbm,
    ]
    return pl.pallas_call(
        functools.partial(_moe_kernel, n_f=n_f),
        grid_spec=pltpu.PrefetchScalarGridSpec(
            num_scalar_prefetch=7,
            grid=(n_items, n_f),
            in_specs=in_specs,
            out_specs=pl.BlockSpec(memory_space=pl.ANY),
            scratch_shapes=[pltpu.VMEM((r_max, d), BF16), pltpu.VMEM((r_max, d), F32),
                            pltpu.VMEM((2, d, tf), F32), pltpu.VMEM((2, d, tf), F32),
                            pltpu.VMEM((2, tf, d), F32),
                            pltpu.VMEM((d, 2 * tf), BF16), pltpu.VMEM((tf, d), BF16),
                            pltpu.VMEM((2, ROW_BLOCK * (chunks + SLAB_PAD), LANES), F32),
                            pltpu.VMEM((r_max * (words + 1), LANES), jnp.uint32),
                            pltpu.SMEM((n_rows,), jnp.int32),
                            pltpu.SemaphoreType.DMA(()), pltpu.SemaphoreType.DMA((2,)),
                            pltpu.SemaphoreType.DMA((2,))],
        ),
        out_shape=jax.ShapeDtypeStruct((n_rows * (chunks + SLAB_PAD), LANES), F32),
        compiler_params=_cparams(("arbitrary", "arbitrary")),
        name="moe_ffn",
    )(item_e, item_j, item_start, item_nsub, tail_block, live_steps, dest_flat, x1_packed, w_gate_up, w_down,
      b_gate_up.reshape(n_e, 1, two_f), b_gate_up.reshape(n_e, 1, two_f), b_down.reshape(n_e, 1, d),
      jnp.zeros((n_rows,), jnp.int32))


def _combine_kernel(dest_ref, y_hbm, x1_ref, meta_ref, g_ref, b_ref, o_ref, buf, sem, *, alpha, tm, chunks):
    i = pl.program_id(0)
    per_step = tm * TOP_K
    n_tok = dest_ref.shape[0] // TOP_K
    pitch = chunks + 1

    def issue_tile(step, slot):
        def body(tt, _):
            for kk in range(TOP_K):
                src = pl.ds(dest_ref[kk * n_tok + step * tm + tt] * (chunks + SLAB_PAD), chunks)
                dst = pl.ds((kk * tm + tt) * pitch, chunks)
                pltpu.make_async_copy(y_hbm.at[src, :], buf.at[slot, dst, :], sem.at[slot]).start(
                    priority=kk % 2)
            return 0
        lax.fori_loop(0, tm, body, 0, unroll=DMA_UNROLL // TOP_K)

    @pl.when(i == 0)
    def _():
        issue_tile(0, 0)

    @pl.when(i + 1 < pl.num_programs(0))
    def _():
        issue_tile(i + 1, (i + 1) % 2)

    slot = i % 2
    _wait_slabs(y_hbm, buf, sem, slot, per_step, chunks)
    meta = meta_ref[...]
    y = jnp.zeros(x1_ref.shape, F32)
    for kk in range(TOP_K):
        rows = jnp.concatenate(
            [buf[slot, pl.ds(kk * tm * pitch + c, tm, stride=pitch), :] for c in range(chunks)], axis=1)
        y = y + meta[:, TOP_K + kk:TOP_K + kk + 1] * rows
    o_ref[...] = _layer_norm(alpha * x1_ref[...] + y, g_ref[...], b_ref[...])


def _combine(dest_flat, y_rows, x1, meta, ln_g, ln_b, alpha, tm=256):
    t, d = x1.shape
    tm = min(tm, t)
    chunks = d // LANES
    row = lambda w: pl.BlockSpec((tm, w), lambda i, dest: (i, 0))
    const = pl.BlockSpec((1, d), lambda i, dest: (0, 0))
    return pl.pallas_call(
        functools.partial(_combine_kernel, alpha=alpha, tm=tm, chunks=chunks),
        grid_spec=pltpu.PrefetchScalarGridSpec(
            num_scalar_prefetch=1,
            grid=(t // tm,),
            in_specs=[pl.BlockSpec(memory_space=pl.ANY), row(d), row(LANES), const, const],
            out_specs=row(d),
            scratch_shapes=[pltpu.VMEM((2, TOP_K * tm * (chunks + 1), LANES), F32),
                            pltpu.SemaphoreType.DMA((2,))],
        ),
        out_shape=jax.ShapeDtypeStruct((t, d), F32),
        compiler_params=_cparams(("arbitrary",)),
        name="combine_ln",
    )(dest_flat, y_rows, x1, meta, ln_g.reshape(1, d).astype(F32), ln_b.reshape(1, d).astype(F32))


def _routing_plan(meta, counts_f, t, r_max):
    meta_t = meta.T
    idx = meta_t[0:TOP_K].astype(jnp.int32)
    rank = meta_t[2 * TOP_K:3 * TOP_K].astype(jnp.int32)
    counts = counts_f[0, :N_EXPERTS].astype(jnp.int32)
    n128 = (counts + ROW_BLOCK - 1) // ROW_BLOCK
    padded = n128 * ROW_BLOCK
    pad_start = jnp.cumsum(padded) - padded
    experts = jnp.arange(N_EXPERTS, dtype=jnp.int32)[:, None, None]
    region = jnp.sum(jnp.where(idx[None] == experts, pad_start[:, None, None], 0), axis=0)
    dest = region + rank
    tk = t * TOP_K
    n_rows = (tk + ROW_BLOCK - 1) // ROW_BLOCK * ROW_BLOCK + N_EXPERTS * ROW_BLOCK

    subs = r_max // ROW_BLOCK
    n_items = N_EXPERTS + (n_rows // ROW_BLOCK - N_EXPERTS) // subs
    items_e = (n128 + subs - 1) // subs
    items_end = jnp.cumsum(items_e)
    total = items_end[-1]
    slot = jnp.arange(n_items, dtype=jnp.int32)
    live = slot < total
    s_eff = jnp.minimum(slot, total - 1)
    e = jnp.minimum(jnp.searchsorted(items_end, s_eff, side='right'), N_EXPERTS - 1).astype(jnp.int32)
    local = s_eff - (items_end[e] - items_e[e])
    item_start = (pad_start[e] + local * r_max).astype(jnp.int32)
    item_nsub = jnp.where(live, jnp.clip(n128[e] - local * subs, 0, subs), 0).astype(jnp.int32)
    tail_block = jnp.sum(n128).reshape(1).astype(jnp.int32)
    return dest.reshape(tk), n_rows, e, live, item_start, item_nsub, tail_block


def kernel(x, w_in, ret_gn_gain, w_out, ln1_gain, ln1_bias, w_router, b_router, w_gate_up, b_gate_up,
           w_down, b_down, ln2_gain, ln2_bias):
    b, s, d = x.shape
    t = b * s
    depth = w_in.shape[0]
    alpha = (2 * depth) ** 0.25
    r_max = 1152
    tf = 512
    n_f = (w_gate_up.shape[-1] // 2) // tf
    xt = x.reshape(t, d)
    for layer in range(depth):
        proj = _in_proj(xt, w_in[layer].astype(BF16))
        sb = _stickbreak(proj, b, s)
        ret = _retention(proj, ret_gn_gain[layer], b, s)
        x1, x1_slabs, meta, counts = _outproj(sb, ret, xt, w_out[layer].astype(BF16), ln1_gain[layer],
                                              ln1_bias[layer], w_router[layer], b_router[layer], alpha)
        dest_flat, n_rows, item_e, live, item_start, item_nsub, tail_block = _routing_plan(
            meta, counts, t, r_max)
        item_j = jnp.where(live, -1, n_f - 1).astype(jnp.int32)
        live_steps = (jnp.sum(live.astype(jnp.int32)) * n_f).reshape(1)
        y_rows = _moe_ffn(item_e, item_j, item_start, item_nsub, tail_block, live_steps, dest_flat, x1_slabs,
                          n_rows, w_gate_up[layer], b_gate_up[layer], w_down[layer], b_down[layer], r_max, tf)
        xt = _combine(dest_flat, y_rows, x1, meta, ln2_gain[layer], ln2_bias[layer], alpha)
    return xt.reshape(b, s, d)
```

```python
import functools
import math

import jax
import jax.numpy as jnp
from jax import lax
from jax.experimental import pallas as pl
from jax.experimental.pallas import tpu as pltpu

F32 = jnp.float32
BF16 = jnp.bfloat16

HEAD_DIM = 128
SB_HEADS = 8
RET_HEADS = 8
CHUNK = 64
ROPE_BASE = 10000.0
N_EXPERTS = 32
TOP_K = 4
SWIGLU_LIMIT = 7.0
SWIGLU_ALPHA = 1.702
LN_EPS = 1e-5
GN_EPS = 1e-5

V7X_VMEM_LIMIT_BYTES = 56 * 1024 * 1024
LANES = 128
ROW_BLOCK = 128
NEG_BIG = -1e30
EXP_UNDERFLOW = -105.0


def _cparams(sem, vmem=V7X_VMEM_LIMIT_BYTES):
    return pltpu.CompilerParams(dimension_semantics=sem, vmem_limit_bytes=vmem)


def _matmul_kernel(x_ref, w_ref, o_ref, x_bf):
    @pl.when(pl.program_id(1) == 0)
    def _():
        x_bf[...] = x_ref[...].astype(BF16)
    o_ref[...] = jnp.dot(x_bf[...], w_ref[...], preferred_element_type=F32).astype(o_ref.dtype)


def _in_proj(x, w_bf, tm=1024, tn=1792):
    t, d = x.shape
    n = w_bf.shape[1]
    tm = min(tm, t)
    return pl.pallas_call(
        _matmul_kernel,
        grid=(t // tm, n // tn),
        in_specs=[pl.BlockSpec((tm, d), lambda i, j: (i, 0)),
                  pl.BlockSpec((d, tn), lambda i, j: (0, j))],
        out_specs=pl.BlockSpec((tm, tn), lambda i, j: (i, j)),
        out_shape=jax.ShapeDtypeStruct((t, n), BF16),
        scratch_shapes=[pltpu.VMEM((tm, d), BF16)],
        compiler_params=_cparams(("arbitrary", "arbitrary")),
        name="in_proj",
    )(x, w_bf)


def _sb_kernel(q_ref, k_ref, v_ref, o_ref, *, seq, scale, group, sub):
    blk = 128
    nq = seq // blk
    row = lax.broadcasted_iota(jnp.int32, (blk, blk), 0)
    col = lax.broadcasted_iota(jnp.int32, (blk, blk), 1)
    causal = col < row
    r2 = lax.broadcasted_iota(jnp.int32, (2 * blk, 2 * blk), 0)
    c2 = lax.broadcasted_iota(jnp.int32, (2 * blk, 2 * blk), 1)
    cum_rhs = jnp.where((c2 >= blk) | ((r2 & (blk - 1)) >= c2), 1.0, 0.0).astype(BF16)

    n_sub = group // sub
    causal_sub = jnp.concatenate([causal] * sub, axis=0)
    heads_of = lambda s: range(s * sub, (s + 1) * sub)
    cols = lambda g: slice(g * blk, (g + 1) * blk)

    def block(qbs, kb, carries, accs, masked):
        ks = pl.ds(pl.multiple_of(kb * blk, blk), blk)
        zs = [jnp.concatenate(
            [lax.dot_general(qbs[g], k_ref[ks, cols(g)], (((1,), (1,)), ((), ())),
                             preferred_element_type=F32) for g in heads_of(s)], axis=0) * scale
              for s in range(n_sub)]
        sums = []
        for z in zs:
            neg_z = -z
            lnb = jnp.minimum(neg_z, 0.0) - jnp.log(1.0 + jnp.exp(jnp.minimum(z, neg_z)))
            if masked:
                lnb = jnp.where(causal_sub, lnb, 0.0)
            hi = lnb.astype(BF16)
            lo = (lnb - hi.astype(F32)).astype(BF16)
            sums.append(jnp.dot(jnp.concatenate([hi, lo], axis=1), cum_rhs, preferred_element_type=F32))
        new_carries, new_accs = [], []
        for s in range(n_sub):
            incl = sums[s][:, :blk]
            total = sums[s][:, blk:]
            w = jnp.exp(zs[s] + incl + carries[s])
            if masked:
                w = jnp.where(causal_sub, w, 0.0)
            w = w.astype(BF16)
            for n, g in enumerate(heads_of(s)):
                new_accs.append(accs[g] + jnp.dot(w[n * blk:(n + 1) * blk], v_ref[ks, cols(g)],
                                                  preferred_element_type=F32))
            new_carries.append(carries[s] + total)
        return tuple(new_carries), tuple(new_accs)

    def q_body(qi, _):
        qs = pl.ds(pl.multiple_of(qi * blk, blk), blk)
        qbs = [q_ref[qs, cols(g)] for g in range(group)]
        state = block(qbs, qi, (jnp.zeros((sub * blk, blk), F32),) * n_sub,
                      (jnp.zeros((blk, blk), F32),) * group, True)

        def kb_body(t, st):
            return block(qbs, qi - 1 - t, st[0], st[1], False)

        _, accs = lax.fori_loop(0, qi, kb_body, state)
        for g in range(group):
            o_ref[qs, cols(g)] = accs[g].astype(o_ref.dtype)
        return 0

    lax.fori_loop(0, nq, q_body, 0)


def _stickbreak(proj, batch, seq, group=8, sub=4):
    t = batch * seq
    h = SB_HEADS
    n_groups = h // group
    width = group * HEAD_DIM
    kern = functools.partial(_sb_kernel, seq=seq, scale=1.0 / math.sqrt(HEAD_DIM), group=group,
                             sub=sub)
    spec = lambda off: pl.BlockSpec((seq, width), lambda b, hg: (b, off * n_groups + hg))
    return pl.pallas_call(
        kern,
        grid=(batch, n_groups),
        in_specs=[spec(0), spec(1), spec(2)],
        out_specs=pl.BlockSpec((seq, width), lambda b, hg: (b, hg)),
        out_shape=jax.ShapeDtypeStruct((t, h * HEAD_DIM), BF16),
        compiler_params=_cparams(("arbitrary", "arbitrary")),
        name="stickbreak",
    )(proj, proj, proj)


def _ret_kernel(q_ref, k_ref, v_ref, g_ref, cos_ref, sin_ref, intra_ref, qdec_ref, kdec_ref,
                cdec_ref, gain_ref, o_ref, *, seq, heads):
    n_chunks = seq // CHUNK
    half = HEAD_DIM // 2
    k_scale = HEAD_DIM ** -0.5
    cols = lambda h: slice(h * HEAD_DIM, (h + 1) * HEAD_DIM)
    contract_last = (((1,), (1,)), ((), ()))
    contract_rows = (((0,), (0,)), ((), ()))

    def chunk(n, states):
        rs = pl.ds(pl.multiple_of(n * CHUNK, CHUNK), CHUNK)
        cos = cos_ref[rs, :]
        sin = sin_ref[rs, :]
        qrs, krs, crosses, kvs = [], [], [], []
        for h in range(heads):
            q = q_ref[rs, cols(h)].astype(F32)
            k = k_ref[rs, cols(h)].astype(F32)
            qr = q * cos + pltpu.roll(q, half, 1) * sin
            kr = (k * cos + pltpu.roll(k, half, 1) * sin) * k_scale
            qrs.append(qr.astype(BF16))
            krs.append(kr.astype(BF16))
            crosses.append(jnp.dot((qr * qdec_ref[h]).astype(BF16), states[h].astype(BF16),
                                   preferred_element_type=F32))
            kvs.append(lax.dot_general((kr * kdec_ref[h]).astype(BF16), v_ref[rs, cols(h)], contract_rows,
                                       preferred_element_type=F32))
        scores = [lax.dot_general(qrs[h], krs[h], contract_last, preferred_element_type=F32) * intra_ref[h]
                  for h in range(heads)]
        outs = [crosses[h] + jnp.dot(scores[h].astype(BF16), v_ref[rs, cols(h)], preferred_element_type=F32)
                for h in range(heads)]
        new_states = []
        for h in range(heads):
            o = outs[h]
            mu = jnp.mean(o, axis=-1, keepdims=True)
            var = jnp.mean(jnp.square(o - mu), axis=-1, keepdims=True)
            on = (o - mu) * lax.rsqrt(var + GN_EPS)
            g = g_ref[rs, cols(h)].astype(F32)
            out = on * gain_ref[:, cols(h)] * (g * jax.nn.sigmoid(g))
            o_ref[rs, cols(h)] = out.astype(o_ref.dtype)
            new_states.append(states[h] * cdec_ref[h] + kvs[h])
        return tuple(new_states)

    lax.fori_loop(0, n_chunks, chunk, (jnp.zeros((HEAD_DIM, HEAD_DIM), F32),) * heads)


def _retention_tables(seq):
    d = HEAD_DIM
    inv_freq = ROPE_BASE ** (-jnp.arange(0, d, 2, dtype=F32) / d)
    ang = jnp.arange(seq, dtype=F32)[:, None] * inv_freq[None, :]
    cos, sin = jnp.cos(ang), jnp.sin(ang)
    cos_full = jnp.concatenate([cos, cos], axis=-1)
    sin_signed = jnp.concatenate([-sin, sin], axis=-1)
    log_gamma = jnp.log1p(-jnp.exp2(-5.0 - jnp.arange(RET_HEADS, dtype=F32)))
    i = jnp.arange(CHUNK, dtype=F32)
    intra = jnp.exp(log_gamma[:, None, None] * jnp.abs(i[:, None] - i[None, :]))
    k_decay = jnp.exp(log_gamma[:, None] * (CHUNK - 1 - i))
    q_decay = jnp.exp(log_gamma[:, None] * (i + 1.0))
    c_decay = jnp.exp(log_gamma * CHUNK)
    bc = lambda a: jnp.broadcast_to(a[..., None], a.shape + (d,))
    return cos_full, sin_signed, intra, bc(q_decay), bc(k_decay), bc(c_decay[:, None])


def _retention(proj, gn_gain, batch, seq):
    t = batch * seq
    h = RET_HEADS
    base = 3 * SB_HEADS
    cos_full, sin_signed, intra, qdec, kdec, cdec = _retention_tables(seq)
    width = h * HEAD_DIM
    group0 = base // h
    spec = lambda off: pl.BlockSpec((seq, width), lambda b: (b, group0 + off))
    full = pl.BlockSpec((seq, HEAD_DIM), lambda b: (0, 0))
    table = lambda r, c: pl.BlockSpec((h, r, c), lambda b: (0, 0, 0))
    return pl.pallas_call(
        functools.partial(_ret_kernel, seq=seq, heads=h),
        grid=(batch,),
        in_specs=[spec(0), spec(1), spec(2), spec(3), full, full,
                  table(CHUNK, CHUNK), table(CHUNK, HEAD_DIM), table(CHUNK, HEAD_DIM),
                  table(1, HEAD_DIM),
                  pl.BlockSpec((1, width), lambda b: (0, 0))],
        out_specs=pl.BlockSpec((seq, width), lambda b: (b, 0)),
        out_shape=jax.ShapeDtypeStruct((t, width), BF16),
        compiler_params=_cparams(("arbitrary",)),
        name="retention",
    )(proj, proj, proj, proj, cos_full, sin_signed, intra, qdec, kdec, cdec,
      gn_gain.reshape(1, h * HEAD_DIM).astype(F32))


def _layer_norm(hid, gain, bias):
    mu = jnp.mean(hid, axis=-1, keepdims=True)
    cen = hid - mu
    var = jnp.mean(jnp.square(cen), axis=-1, keepdims=True)
    return cen * lax.rsqrt(var + LN_EPS) * gain + bias


SLAB_PAD = 4


def _to_slabs(slab_ref, base, rows, value, pitch):
    for c in range(value.shape[1] // LANES):
        slab_ref[pl.ds(base + c, rows, stride=pitch), :] = value[:, c * LANES:(c + 1) * LANES]


def _bf16_bits(x):
    u = pltpu.bitcast(x, jnp.uint32)
    return lax.shift_right_logical(u + jnp.uint32(0x7FFF) + (lax.shift_right_logical(u, jnp.uint32(16))
                                                             & jnp.uint32(1)), jnp.uint32(16))


def _to_packed_slabs(slab_ref, rows, value):
    words = value.shape[1] // (2 * LANES)
    for c in range(words):
        lo = _bf16_bits(value[:, (2 * c) * LANES:(2 * c + 1) * LANES])
        hi = _bf16_bits(value[:, (2 * c + 1) * LANES:(2 * c + 2) * LANES])
        slab_ref[pl.ds(c, rows, stride=words), :] = lo | lax.shift_left(hi, jnp.uint32(16))


def _from_packed_slabs(slab_ref, base, rows, words, pitch):
    out = []
    for c in range(words):
        w = slab_ref[pl.ds(base + c, rows, stride=pitch), :]
        out.append(pltpu.bitcast(lax.shift_left(w, jnp.uint32(16)), F32).astype(BF16))
        out.append(pltpu.bitcast(w & jnp.uint32(0xFFFF0000), F32).astype(BF16))
    return jnp.concatenate(out, axis=1)


def _outproj_kernel(sb_ref, ret_ref, x_ref, w_ref, g_ref, b_ref, wr_ref, br_ref,
                    x1_ref, x1s_ref, meta_ref, cnt_ref, *, alpha, sb_width):
    tm = x_ref.shape[0]
    mix = jnp.dot(sb_ref[...], w_ref[:sb_width, :], preferred_element_type=F32)
    mix = mix + jnp.dot(ret_ref[...], w_ref[sb_width:, :], preferred_element_type=F32)
    x1 = _layer_norm(alpha * x_ref[...] + mix, g_ref[...], b_ref[...])
    x1_ref[...] = x1
    _to_packed_slabs(x1s_ref, tm, x1)

    logits = jnp.dot(x1.astype(BF16), wr_ref[...], preferred_element_type=F32) + br_ref[...]
    lane = lax.broadcasted_iota(jnp.int32, (tm, LANES), 1).astype(F32)
    vals = logits
    tops, idxs, hots = [], [], []
    for _ in range(TOP_K):
        m = jnp.max(vals, axis=-1, keepdims=True)
        idx = jnp.min(jnp.where(vals == m, lane, float(LANES)), axis=-1, keepdims=True)
        hot = lane == idx
        vals = jnp.where(hot, NEG_BIG * 2.0, vals)
        tops.append(m)
        idxs.append(idx)
        hots.append(hot)
    exps = [jnp.exp(m - tops[0]) for m in tops]
    denom = exps[0] + exps[1] + exps[2] + exps[3]
    gates = [e / denom for e in exps]

    @pl.when(pl.program_id(0) == 0)
    def _():
        cnt_ref[...] = jnp.zeros_like(cnt_ref)

    multi = jnp.zeros((tm, LANES), F32)
    for hot in hots:
        multi = multi + jnp.where(hot, 1.0, 0.0)
    r = lax.broadcasted_iota(jnp.int32, (tm, tm), 0)
    c = lax.broadcasted_iota(jnp.int32, (tm, tm), 1)
    strict_lower = jnp.where(c < r, 1.0, 0.0).astype(BF16)
    before = jnp.dot(strict_lower, multi.astype(BF16), preferred_element_type=F32) + cnt_ref[...]
    cnt_ref[...] = cnt_ref[...] + jnp.sum(multi, axis=0, keepdims=True)

    meta = jnp.zeros((tm, LANES), F32)
    for kk in range(TOP_K):
        rank = jnp.sum(jnp.where(hots[kk], before, 0.0), axis=-1, keepdims=True)
        meta = jnp.where(lane == kk, idxs[kk], meta)
        meta = jnp.where(lane == TOP_K + kk, gates[kk], meta)
        meta = jnp.where(lane == 2 * TOP_K + kk, rank, meta)
    meta_ref[...] = meta


def _outproj(sb, ret, xt, w_out_bf, ln_g, ln_b, w_router, b_router, alpha, tm=512):
    t, d = xt.shape
    tm = min(tm, t)
    sbw = sb.shape[1]
    wr = jnp.zeros((d, LANES), BF16).at[:, :N_EXPERTS].set(w_router.astype(BF16))
    br = jnp.full((1, LANES), NEG_BIG, F32).at[0, :N_EXPERTS].set(b_router.astype(F32))
    row = lambda w: pl.BlockSpec((tm, w), lambda i: (i, 0))
    const = lambda r, c: pl.BlockSpec((r, c), lambda i: (0, 0))
    return pl.pallas_call(
        functools.partial(_outproj_kernel, alpha=alpha, sb_width=sbw),
        grid=(t // tm,),
        in_specs=[row(sbw), row(ret.shape[1]), row(d), const(d, d), const(1, d), const(1, d),
                  const(d, LANES), const(1, LANES)],
        out_specs=[row(d), pl.BlockSpec((tm * (d // (2 * LANES)), LANES), lambda i: (i, 0)), row(LANES),
                   const(1, LANES)],
        out_shape=[jax.ShapeDtypeStruct((t, d), F32),
                   jax.ShapeDtypeStruct((t * (d // (2 * LANES)), LANES), jnp.uint32),
                   jax.ShapeDtypeStruct((t, LANES), F32),
                   jax.ShapeDtypeStruct((1, LANES), F32)],
        compiler_params=_cparams(("arbitrary",)),
        name="outproj_ln_router",
    )(sb, ret, xt, w_out_bf, ln_g.reshape(1, d).astype(F32), ln_b.reshape(1, d).astype(F32), wr, br)


DMA_UNROLL = 8
PAIRS_PER_TRIP = 4


def _wait_slabs(hbm, buf, sem, slot, count, chunks):
    def body(n, _):
        pltpu.make_async_copy(hbm.at[pl.ds(0, chunks), :], buf.at[slot, pl.ds(0, chunks), :],
                              sem.at[slot]).wait()
        return 0
    lax.fori_loop(0, count, body, 0, unroll=DMA_UNROLL)


def _moe_kernel(ie_ref, ij_ref, ist_ref, ins_ref, tail_ref, live_ref, dest_ref, x_hbm, wgu_hbm, wd_hbm, bg_ref,
                bu_ref, bd_ref, zeros_hbm, y_hbm, x_vmem, acc, wg_f, wu_f, wd_f, wgu_bf, wd_bf, stage, gbuf, tok_ref,
                sem_g, sem_out, sem_w, *, n_f):
    del ij_ref
    i = pl.program_id(0)
    j = pl.program_id(1)
    nsub = ins_ref[i]
    start = ist_ref[i]
    d = acc.shape[1]
    tf = wd_bf.shape[0]
    d_ff = n_f * tf
    y_pitch = d // LANES + SLAB_PAD
    block_rows = ROW_BLOCK * y_pitch

    def weight_copies(e, jj, slot):
        col_g = pl.ds(pl.multiple_of(jj * tf, tf), tf)
        col_u = pl.ds(pl.multiple_of(d_ff + jj * tf, tf), tf)
        copies = []
        for half in range(2):
            rows_k = pl.ds(half * (d // 2), d // 2)
            rows_f = pl.ds(pl.multiple_of(jj * tf + half * (tf // 2), tf // 2), tf // 2)
            dst_f = pl.ds(half * (tf // 2), tf // 2)
            copies += [
                pltpu.make_async_copy(wgu_hbm.at[e, rows_k, col_g], wg_f.at[slot, rows_k, :], sem_w.at[slot]),
                pltpu.make_async_copy(wgu_hbm.at[e, rows_k, col_u], wu_f.at[slot, rows_k, :], sem_w.at[slot]),
                pltpu.make_async_copy(wd_hbm.at[e, rows_f, :], wd_f.at[slot, dst_f, :], sem_w.at[slot]),
            ]
        return copies

    def start_weights(e, jj, slot):
        for n, c in enumerate(weight_copies(e, jj, slot)):
            c.start(priority=n % 2)

    step = i * n_f + j
    w_slot = step % 2

    @pl.when(step == 0)
    def _():
        start_weights(ie_ref[0], 0, 0)

    @pl.when(step + 1 < live_ref[0])
    def _():
        last = j == n_f - 1
        start_weights(ie_ref[jnp.where(last, i + 1, i)], jnp.where(last, 0, j + 1), 1 - w_slot)

    def rows_of(r):
        return pl.ds(pl.multiple_of(r * ROW_BLOCK, ROW_BLOCK), ROW_BLOCK)

    def slabs_of(block):
        return pl.ds(pl.multiple_of(block * block_rows, block_rows), block_rows)

    def for_each(lo, hi, fn):
        def body(r, _):
            fn(r)
            return 0
        lax.fori_loop(lo, hi, body, 0)

    @pl.when((i == 0) & (j == 0))
    def _():
        stage[...] = jnp.zeros(stage.shape, F32)

        def zero_copy(bk):
            return pltpu.make_async_copy(stage.at[0], y_hbm.at[slabs_of(bk), :], sem_out.at[0])

        n_blocks = y_hbm.shape[0] // block_rows
        for_each(tail_ref[0], n_blocks, lambda bk: zero_copy(bk).start())
        for_each(tail_ref[0], n_blocks, lambda bk: zero_copy(bk).wait())

    first_block = start // ROW_BLOCK
    words = d // (2 * LANES)

    pitch = words + 1

    def build_row_tokens():
        n_tok = dest_ref.shape[0] // TOP_K
        clear = pltpu.make_async_copy(zeros_hbm, tok_ref, sem_g)
        clear.start()
        clear.wait()

        def fill(m, _):
            for u in range(DMA_UNROLL):
                t = m * DMA_UNROLL + u
                for kk in range(TOP_K):
                    tok_ref[dest_ref[kk * n_tok + t]] = t
            return 0
        lax.fori_loop(0, n_tok // DMA_UNROLL, fill, 0)

    def start_gather(item):
        row0 = ist_ref[item]

        def body(m, _):
            for u in range(DMA_UNROLL):
                r = m * DMA_UNROLL + u
                src = pl.ds(pl.multiple_of(tok_ref[row0 + r] * words, words), words)
                pltpu.make_async_copy(x_hbm.at[src, :], gbuf.at[pl.ds(r * pitch, words), :], sem_g).start(
                    priority=u % 2)
            return 0
        lax.fori_loop(0, ins_ref[item] * (ROW_BLOCK // DMA_UNROLL), body, 0)

    def wait_gather(item):
        def body(m, _):
            for _u in range(DMA_UNROLL):
                pltpu.make_async_copy(x_hbm.at[pl.ds(0, words), :], gbuf.at[pl.ds(0, words), :], sem_g).wait()
            return 0
        lax.fori_loop(0, ins_ref[item] * (ROW_BLOCK // DMA_UNROLL), body, 0)

    def out_copy(r, slot):
        return pltpu.make_async_copy(stage.at[slot], y_hbm.at[slabs_of(first_block + r), :],
                                     sem_out.at[slot])

    def span(r, n_blocks):
        return pl.ds(pl.multiple_of(r * ROW_BLOCK, ROW_BLOCK), n_blocks * ROW_BLOCK)

    def hidden(rows):
        xb = x_vmem[rows, :]
        gate_up = jnp.dot(xb, wgu_bf[...], preferred_element_type=F32)
        gate = gate_up[:, :tf] + bg_ref[...]
        up = gate_up[:, tf:] + bu_ref[...]
        gate = jnp.minimum(gate, SWIGLU_LIMIT)
        up = jnp.clip(up, -SWIGLU_LIMIT, SWIGLU_LIMIT)
        act = (up + 1.0) * (gate * jax.nn.sigmoid(SWIGLU_ALPHA * gate))
        return jnp.dot(act.astype(BF16), wd_bf[...], preferred_element_type=F32)

    @pl.when(nsub > 0)
    def _():
        @pl.when(j == 0)
        def _():
            @pl.when(i == 0)
            def _():
                build_row_tokens()
                start_gather(0)
            wait_gather(i)
            bias_rows = jnp.broadcast_to(bd_ref[...], (ROW_BLOCK, d))

            def unpack(r):
                x_vmem[rows_of(r), :] = _from_packed_slabs(gbuf, r * (ROW_BLOCK * pitch), ROW_BLOCK, words, pitch)
                acc[rows_of(r), :] = bias_rows
            for_each(0, nsub, unpack)

            @pl.when(i + 1 < pl.num_programs(0))
            def _():
                start_gather(i + 1)

        for c in weight_copies(ie_ref[i], j, w_slot):
            c.wait()
        wgu_bf[:, :tf] = wg_f[w_slot].astype(BF16)
        wgu_bf[:, tf:] = wu_f[w_slot].astype(BF16)
        wd_bf[...] = wd_f[w_slot].astype(BF16)

        n_pairs = nsub // 2
        odd = nsub % 2 == 1

        def run_pairs(pair, group):
            def trip(q):
                for u in range(group):
                    pair(q * group + u)
            for_each(0, n_pairs // group, trip)
            for_each((n_pairs // group) * group, n_pairs, pair)

        @pl.when(j < n_f - 1)
        def _():
            def pair(p):
                acc[span(2 * p, 2), :] += hidden(span(2 * p, 2))
            run_pairs(pair, PAIRS_PER_TRIP)

            @pl.when(odd)
            def _():
                acc[span(nsub - 1, 1), :] += hidden(span(nsub - 1, 1))

        @pl.when(j == n_f - 1)
        def _():
            def write_out(r, slot, final, reuse):
                @pl.when(reuse)
                def _():
                    out_copy(r, slot).wait()
                _to_slabs(stage.at[slot], 0, ROW_BLOCK, final, y_pitch)
                out_copy(r, slot).start()

            def pair(p):
                final = acc[span(2 * p, 2), :] + hidden(span(2 * p, 2))
                write_out(2 * p, 0, final[:ROW_BLOCK], p > 0)
                write_out(2 * p + 1, 1, final[ROW_BLOCK:], p > 0)
            run_pairs(pair, 2)

            @pl.when(odd)
            def _():
                final = acc[span(nsub - 1, 1), :] + hidden(span(nsub - 1, 1))
                write_out(nsub - 1, 0, final, n_pairs > 0)

            out_copy(0, 0).wait()

            @pl.when(n_pairs > 0)
            def _():
                out_copy(0, 1).wait()


def _moe_ffn(item_e, item_j, item_start, item_nsub, tail_block, live_steps, dest_flat, x1_packed, n_rows,
             w_gate_up, b_gate_up, w_down, b_down, r_max, tf=512):
    n_e, d, two_f = w_gate_up.shape
    chunks = d // LANES
    words = d // (2 * LANES)
    d_ff = two_f // 2
    n_f = d_ff // tf
    n_items = item_e.shape[0]

    def jf(i, j, ij):
        return jnp.where(ij[i] < 0, j, ij[i])

    hbm = pl.BlockSpec(memory_space=pl.ANY)
    in_specs = [
        hbm, hbm, hbm,
        pl.BlockSpec((None, 1, tf), lambda i, j, ie, ij, ist, ins, tl, lv, tk: (ie[i], 0, jf(i, j, ij))),
        pl.BlockSpec((None, 1, tf), lambda i, j, ie, ij, ist, ins, tl, lv, tk: (ie[i], 0, n_f + jf(i, j, ij))),
        pl.BlockSpec((None, 1, d), lambda i, j, ie, ij, ist, ins, tl, lv, tk: (ie[i], 0, 0)),
        hbm,
    ]
    return pl.pallas_call(
        functools.partial(_moe_kernel, n_f=n_f),
        grid_spec=pltpu.PrefetchScalarGridSpec(
            num_scalar_prefetch=7,
            grid=(n_items, n_f),
            in_specs=in_specs,
            out_specs=pl.BlockSpec(memory_space=pl.ANY),
            scratch_shapes=[pltpu.VMEM((r_max, d), BF16), pltpu.VMEM((r_max, d), F32),
                            pltpu.VMEM((2, d, tf), F32), pltpu.VMEM((2, d, tf), F32),
                            pltpu.VMEM((2, tf, d), F32),
                            pltpu.VMEM((d, 2 * tf), BF16), pltpu.VMEM((tf, d), BF16),
                            pltpu.VMEM((2, ROW_BLOCK * (chunks + SLAB_PAD), LANES), F32),
                            pltpu.VMEM((r_max * (words + 1), LANES), jnp.uint32),
                            pltpu.SMEM((n_rows,), jnp.int32),
                            pltpu.SemaphoreType.DMA(()), pltpu.SemaphoreType.DMA((2,)),
                            pltpu.SemaphoreType.DMA((2,))],
        ),
        out_shape=jax.ShapeDtypeStruct((n_rows * (chunks + SLAB_PAD), LANES), F32),
        compiler_params=_cparams(("arbitrary", "arbitrary")),
        name="moe_ffn",
    )(item_e, item_j, item_start, item_nsub, tail_block, live_steps, dest_flat, x1_packed, w_gate_up, w_down,
      b_gate_up.reshape(n_e, 1, two_f), b_gate_up.reshape(n_e, 1, two_f), b_down.reshape(n_e, 1, d),
      jnp.zeros((n_rows,), jnp.int32))


def _combine_kernel(dest_ref, y_hbm, x1_ref, meta_ref, g_ref, b_ref, o_ref, buf, sem, *, alpha, tm, chunks):
    i = pl.program_id(0)
    per_step = tm * TOP_K
    n_tok = dest_ref.shape[0] // TOP_K
    pitch = chunks + 1

    def issue_tile(step, slot):
        def body(tt, _):
            for kk in range(TOP_K):
                src = pl.ds(dest_ref[kk * n_tok + step * tm + tt] * (chunks + SLAB_PAD), chunks)
                dst = pl.ds((kk * tm + tt) * pitch, chunks)
                pltpu.make_async_copy(y_hbm.at[src, :], buf.at[slot, dst, :], sem.at[slot]).start(
                    priority=kk % 2)
            return 0
        lax.fori_loop(0, tm, body, 0, unroll=DMA_UNROLL // TOP_K)

    @pl.when(i == 0)
    def _():
        issue_tile(0, 0)

    @pl.when(i + 1 < pl.num_programs(0))
    def _():
        issue_tile(i + 1, (i + 1) % 2)

    slot = i % 2
    _wait_slabs(y_hbm, buf, sem, slot, per_step, chunks)
    meta = meta_ref[...]
    y = jnp.zeros(x1_ref.shape, F32)
    for kk in range(TOP_K):
        rows = jnp.concatenate(
            [buf[slot, pl.ds(kk * tm * pitch + c, tm, stride=pitch), :] for c in range(chunks)], axis=1)
        y = y + meta[:, TOP_K + kk:TOP_K + kk + 1] * rows
    o_ref[...] = _layer_norm(alpha * x1_ref[...] + y, g_ref[...], b_ref[...])


def _combine(dest_flat, y_rows, x1, meta, ln_g, ln_b, alpha, tm=256):
    t, d = x1.shape
    tm = min(tm, t)
    chunks = d // LANES
    row = lambda w: pl.BlockSpec((tm, w), lambda i, dest: (i, 0))
    const = pl.BlockSpec((1, d), lambda i, dest: (0, 0))
    return pl.pallas_call(
        functools.partial(_combine_kernel, alpha=alpha, tm=tm, chunks=chunks),
        grid_spec=pltpu.PrefetchScalarGridSpec(
            num_scalar_prefetch=1,
            grid=(t // tm,),
            in_specs=[pl.BlockSpec(memory_space=pl.ANY), row(d), row(LANES), const, const],
            out_specs=row(d),
            scratch_shapes=[pltpu.VMEM((2, TOP_K * tm * (chunks + 1), LANES), F32),
                            pltpu.SemaphoreType.DMA((2,))],
        ),
        out_shape=jax.ShapeDtypeStruct((t, d), F32),
        compiler_params=_cparams(("arbitrary",)),
        name="combine_ln",
    )(dest_flat, y_rows, x1, meta, ln_g.reshape(1, d).astype(F32), ln_b.reshape(1, d).astype(F32))


def _routing_plan(meta, counts_f, t, r_max):
    meta_t = meta.T
    idx = meta_t[0:TOP_K].astype(jnp.int32)
    rank = meta_t[2 * TOP_K:3 * TOP_K].astype(jnp.int32)
    counts = counts_f[0, :N_EXPERTS].astype(jnp.int32)
    n128 = (counts + ROW_BLOCK - 1) // ROW_BLOCK
    padded = n128 * ROW_BLOCK
    pad_start = jnp.cumsum(padded) - padded
    experts = jnp.arange(N_EXPERTS, dtype=jnp.int32)[:, None, None]
    region = jnp.sum(jnp.where(idx[None] == experts, pad_start[:, None, None], 0), axis=0)
    dest = region + rank
    tk = t * TOP_K
    n_rows = (tk + ROW_BLOCK - 1) // ROW_BLOCK * ROW_BLOCK + N_EXPERTS * ROW_BLOCK

    subs = r_max // ROW_BLOCK
    n_items = N_EXPERTS + (n_rows // ROW_BLOCK - N_EXPERTS) // subs
    items_e = (n128 + subs - 1) // subs
    items_end = jnp.cumsum(items_e)
    total = items_end[-1]
    slot = jnp.arange(n_items, dtype=jnp.int32)
    live = slot < total
    s_eff = jnp.minimum(slot, total - 1)
    e = jnp.minimum(jnp.searchsorted(items_end, s_eff, side='right'), N_EXPERTS - 1).astype(jnp.int32)
    local = s_eff - (items_end[e] - items_e[e])
    item_start = (pad_start[e] + local * r_max).astype(jnp.int32)
    item_nsub = jnp.where(live, jnp.clip(n128[e] - local * subs, 0, subs), 0).astype(jnp.int32)
    tail_block = jnp.sum(n128).reshape(1).astype(jnp.int32)
    return dest.reshape(tk), n_rows, e, live, item_start, item_nsub, tail_block


def kernel(x, w_in, ret_gn_gain, w_out, ln1_gain, ln1_bias, w_router, b_router, w_gate_up, b_gate_up,
           w_down, b_down, ln2_gain, ln2_bias):
    b, s, d = x.shape
    t = b * s
    depth = w_in.shape[0]
    alpha = (2 * depth) ** 0.25
    r_max = 1152
    tf = 512
    n_f = (w_gate_up.shape[-1] // 2) // tf
    xt = x.reshape(t, d)
    for layer in range(depth):
        proj = _in_proj(xt, w_in[layer].astype(BF16))
        sb = _stickbreak(proj, b, s)
        ret = _retention(proj, ret_gn_gain[layer], b, s)
        x1, x1_slabs, meta, counts = _outproj(sb, ret, xt, w_out[layer].astype(BF16), ln1_gain[layer],
                                              ln1_bias[layer], w_router[layer], b_router[layer], alpha)
        dest_flat, n_rows, item_e, live, item_start, item_nsub, tail_block = _routing_plan(
            meta, counts, t, r_max)
        item_j = jnp.where(live, -1, n_f - 1).astype(jnp.int32)
        live_steps = (jnp.sum(live.astype(jnp.int32)) * n_f).reshape(1)
        y_rows = _moe_ffn(item_e, item_j, item_start, item_nsub, tail_block, live_steps, dest_flat, x1_slabs,
                          n_rows, w_gate_up[layer], b_gate_up[layer], w_down[layer], b_down[layer], r_max, tf)
        xt = _combine(dest_flat, y_rows, x1, meta, ln2_gain[layer], ln2_bias[layer], alpha)
    return xt.reshape(b, s, d)
```

```python
import functools
import math

import jax
import jax.numpy as jnp
from jax import lax
from jax.experimental import pallas as pl
from jax.experimental.pallas import tpu as pltpu

F32 = jnp.float32
BF16 = jnp.bfloat16

HEAD_DIM = 128
SB_HEADS = 8
RET_HEADS = 8
CHUNK = 64
ROPE_BASE = 10000.0
N_EXPERTS = 32
TOP_K = 4
SWIGLU_LIMIT = 7.0
SWIGLU_ALPHA = 1.702
LN_EPS = 1e-5
GN_EPS = 1e-5

V7X_VMEM_LIMIT_BYTES = 56 * 1024 * 1024
LANES = 128
ROW_BLOCK = 128
NEG_BIG = -1e30
EXP_UNDERFLOW = -105.0


def _cparams(sem, vmem=V7X_VMEM_LIMIT_BYTES):
    return pltpu.CompilerParams(dimension_semantics=sem, vmem_limit_bytes=vmem)


def _matmul_kernel(x_ref, w_ref, o_ref, x_bf):
    @pl.when(pl.program_id(1) == 0)
    def _():
        x_bf[...] = x_ref[...].astype(BF16)
    o_ref[...] = jnp.dot(x_bf[...], w_ref[...], preferred_element_type=F32).astype(o_ref.dtype)


def _in_proj(x, w_bf, tm=1024, tn=1792):
    t, d = x.shape
    n = w_bf.shape[1]
    tm = min(tm, t)
    return pl.pallas_call(
        _matmul_kernel,
        grid=(t // tm, n // tn),
        in_specs=[pl.BlockSpec((tm, d), lambda i, j: (i, 0)),
                  pl.BlockSpec((d, tn), lambda i, j: (0, j))],
        out_specs=pl.BlockSpec((tm, tn), lambda i, j: (i, j)),
        out_shape=jax.ShapeDtypeStruct((t, n), BF16),
        scratch_shapes=[pltpu.VMEM((tm, d), BF16)],
        compiler_params=_cparams(("arbitrary", "arbitrary")),
        name="in_proj",
    )(x, w_bf)


def _sb_kernel(q_ref, k_ref, v_ref, o_ref, *, seq, scale, group, sub):
    blk = 128
    nq = seq // blk
    row = lax.broadcasted_iota(jnp.int32, (blk, blk), 0)
    col = lax.broadcasted_iota(jnp.int32, (blk, blk), 1)
    causal = col < row
    r2 = lax.broadcasted_iota(jnp.int32, (2 * blk, 2 * blk), 0)
    c2 = lax.broadcasted_iota(jnp.int32, (2 * blk, 2 * blk), 1)
    cum_rhs = jnp.where((c2 >= blk) | ((r2 & (blk - 1)) >= c2), 1.0, 0.0).astype(BF16)

    n_sub = group // sub
    causal_sub = jnp.concatenate([causal] * sub, axis=0)
    heads_of = lambda s: range(s * sub, (s + 1) * sub)
    cols = lambda g: slice(g * blk, (g + 1) * blk)

    def block(qbs, kb, carries, accs, masked):
        ks = pl.ds(pl.multiple_of(kb * blk, blk), blk)
        zs = [jnp.concatenate(
            [lax.dot_general(qbs[g], k_ref[ks, cols(g)], (((1,), (1,)), ((), ())),
                             preferred_element_type=F32) for g in heads_of(s)], axis=0) * scale
              for s in range(n_sub)]
        sums = []
        for z in zs:
            neg_z = -z
            lnb = jnp.minimum(neg_z, 0.0) - jnp.log(1.0 + jnp.exp(jnp.minimum(z, neg_z)))
            if masked:
                lnb = jnp.where(causal_sub, lnb, 0.0)
            hi = lnb.astype(BF16)
            lo = (lnb - hi.astype(F32)).astype(BF16)
            sums.append(jnp.dot(jnp.concatenate([hi, lo], axis=1), cum_rhs, preferred_element_type=F32))
        new_carries, new_accs = [], []
        for s in range(n_sub):
            incl = sums[s][:, :blk]
            total = sums[s][:, blk:]
            w = jnp.exp(zs[s] + incl + carries[s])
            if masked:
                w = jnp.where(causal_sub, w, 0.0)
            w = w.astype(BF16)
            for n, g in enumerate(heads_of(s)):
                new_accs.append(accs[g] + jnp.dot(w[n * blk:(n + 1) * blk], v_ref[ks, cols(g)],
                                                  preferred_element_type=F32))
            new_carries.append(carries[s] + total)
        return tuple(new_carries), tuple(new_accs)

    def q_body(qi, _):
        qs = pl.ds(pl.multiple_of(qi * blk, blk), blk)
        qbs = [q_ref[qs, cols(g)] for g in range(group)]
        state = block(qbs, qi, (jnp.zeros((sub * blk, blk), F32),) * n_sub,
                      (jnp.zeros((blk, blk), F32),) * group, True)

        def some_weight_left(carries):
            top = functools.reduce(jnp.maximum, carries)
            return (jnp.max(top) > EXP_UNDERFLOW).astype(jnp.int32)

        def live(st):
            t, more, _, _ = st
            return jnp.logical_and(t < qi, more > 0)

        def kb_body(st):
            t, _, carries, accs = st
            carries, accs = block(qbs, qi - 1 - t, carries, accs, False)
            return t + 1, some_weight_left(carries), carries, accs

        _, _, _, accs = lax.while_loop(live, kb_body, (0, 1, state[0], state[1]))
        for g in range(group):
            o_ref[qs, cols(g)] = accs[g].astype(o_ref.dtype)
        return 0

    lax.fori_loop(0, nq, q_body, 0)


def _stickbreak(proj, batch, seq, group=8, sub=4):
    t = batch * seq
    h = SB_HEADS
    n_groups = h // group
    width = group * HEAD_DIM
    kern = functools.partial(_sb_kernel, seq=seq, scale=1.0 / math.sqrt(HEAD_DIM), group=group,
                             sub=sub)
    spec = lambda off: pl.BlockSpec((seq, width), lambda b, hg: (b, off * n_groups + hg))
    return pl.pallas_call(
        kern,
        grid=(batch, n_groups),
        in_specs=[spec(0), spec(1), spec(2)],
        out_specs=pl.BlockSpec((seq, width), lambda b, hg: (b, hg)),
        out_shape=jax.ShapeDtypeStruct((t, h * HEAD_DIM), BF16),
        compiler_params=_cparams(("arbitrary", "arbitrary")),
        name="stickbreak",
    )(proj, proj, proj)


def _ret_kernel(q_ref, k_ref, v_ref, g_ref, cos_ref, sin_ref, intra_ref, qdec_ref, kdec_ref,
                cdec_ref, gain_ref, o_ref, *, seq, heads):
    n_chunks = seq // CHUNK
    half = HEAD_DIM // 2
    k_scale = HEAD_DIM ** -0.5
    cols = lambda h: slice(h * HEAD_DIM, (h + 1) * HEAD_DIM)
    contract_last = (((1,), (1,)), ((), ()))
    contract_rows = (((0,), (0,)), ((), ()))

    def chunk(n, states):
        rs = pl.ds(pl.multiple_of(n * CHUNK, CHUNK), CHUNK)
        cos = cos_ref[rs, :]
        sin = sin_ref[rs, :]
        qrs, krs, crosses, kvs = [], [], [], []
        for h in range(heads):
            q = q_ref[rs, cols(h)].astype(F32)
            k = k_ref[rs, cols(h)].astype(F32)
            qr = q * cos + pltpu.roll(q, half, 1) * sin
            kr = (k * cos + pltpu.roll(k, half, 1) * sin) * k_scale
            qrs.append(qr.astype(BF16))
            krs.append(kr.astype(BF16))
            crosses.append(jnp.dot((qr * qdec_ref[h]).astype(BF16), states[h].astype(BF16),
                                   preferred_element_type=F32))
            kvs.append(lax.dot_general((kr * kdec_ref[h]).astype(BF16), v_ref[rs, cols(h)], contract_rows,
                                       preferred_element_type=F32))
        scores = [lax.dot_general(qrs[h], krs[h], contract_last, preferred_element_type=F32) * intra_ref[h]
                  for h in range(heads)]
        outs = [crosses[h] + jnp.dot(scores[h].astype(BF16), v_ref[rs, cols(h)], preferred_element_type=F32)
                for h in range(heads)]
        new_states = []
        for h in range(heads):
            o = outs[h]
            mu = jnp.mean(o, axis=-1, keepdims=True)
            var = jnp.mean(jnp.square(o - mu), axis=-1, keepdims=True)
            on = (o - mu) * lax.rsqrt(var + GN_EPS)
            g = g_ref[rs, cols(h)].astype(F32)
            out = on * gain_ref[:, cols(h)] * (g * jax.nn.sigmoid(g))
            o_ref[rs, cols(h)] = out.astype(o_ref.dtype)
            new_states.append(states[h] * cdec_ref[h] + kvs[h])
        return tuple(new_states)

    lax.fori_loop(0, n_chunks, chunk, (jnp.zeros((HEAD_DIM, HEAD_DIM), F32),) * heads, unroll=2)


def _retention_tables(seq):
    d = HEAD_DIM
    inv_freq = ROPE_BASE ** (-jnp.arange(0, d, 2, dtype=F32) / d)
    ang = jnp.arange(seq, dtype=F32)[:, None] * inv_freq[None, :]
    cos, sin = jnp.cos(ang), jnp.sin(ang)
    cos_full = jnp.concatenate([cos, cos], axis=-1)
    sin_signed = jnp.concatenate([-sin, sin], axis=-1)
    log_gamma = jnp.log1p(-jnp.exp2(-5.0 - jnp.arange(RET_HEADS, dtype=F32)))
    i = jnp.arange(CHUNK, dtype=F32)
    intra = jnp.exp(log_gamma[:, None, None] * jnp.abs(i[:, None] - i[None, :]))
    k_decay = jnp.exp(log_gamma[:, None] * (CHUNK - 1 - i))
    q_decay = jnp.exp(log_gamma[:, None] * (i + 1.0))
    c_decay = jnp.exp(log_gamma * CHUNK)
    bc = lambda a: jnp.broadcast_to(a[..., None], a.shape + (d,))
    return cos_full, sin_signed, intra, bc(q_decay), bc(k_decay), bc(c_decay[:, None])


def _retention(proj, gn_gain, batch, seq):
    t = batch * seq
    h = RET_HEADS
    base = 3 * SB_HEADS
    cos_full, sin_signed, intra, qdec, kdec, cdec = _retention_tables(seq)
    width = h * HEAD_DIM
    group0 = base // h
    spec = lambda off: pl.BlockSpec((seq, width), lambda b: (b, group0 + off))
    full = pl.BlockSpec((seq, HEAD_DIM), lambda b: (0, 0))
    table = lambda r, c: pl.BlockSpec((h, r, c), lambda b: (0, 0, 0))
    return pl.pallas_call(
        functools.partial(_ret_kernel, seq=seq, heads=h),
        grid=(batch,),
        in_specs=[spec(0), spec(1), spec(2), spec(3), full, full,
                  table(CHUNK, CHUNK), table(CHUNK, HEAD_DIM), table(CHUNK, HEAD_DIM),
                  table(1, HEAD_DIM),
                  pl.BlockSpec((1, width), lambda b: (0, 0))],
        out_specs=pl.BlockSpec((seq, width), lambda b: (b, 0)),
        out_shape=jax.ShapeDtypeStruct((t, width), BF16),
        compiler_params=_cparams(("arbitrary",)),
        name="retention",
    )(proj, proj, proj, proj, cos_full, sin_signed, intra, qdec, kdec, cdec,
      gn_gain.reshape(1, h * HEAD_DIM).astype(F32))


def _layer_norm(hid, gain, bias):
    mu = jnp.mean(hid, axis=-1, keepdims=True)
    cen = hid - mu
    var = jnp.mean(jnp.square(cen), axis=-1, keepdims=True)
    return cen * lax.rsqrt(var + LN_EPS) * gain + bias


SLAB_PAD = 4


def _to_slabs(slab_ref, base, rows, value, pitch):
    for c in range(value.shape[1] // LANES):
        slab_ref[pl.ds(base + c, rows, stride=pitch), :] = value[:, c * LANES:(c + 1) * LANES]


def _bf16_bits(x):
    u = pltpu.bitcast(x, jnp.uint32)
    return lax.shift_right_logical(u + jnp.uint32(0x7FFF) + (lax.shift_right_logical(u, jnp.uint32(16))
                                                             & jnp.uint32(1)), jnp.uint32(16))


def _to_packed_slabs(slab_ref, rows, value):
    words = value.shape[1] // (2 * LANES)
    for c in range(words):
        lo = _bf16_bits(value[:, (2 * c) * LANES:(2 * c + 1) * LANES])
        hi = _bf16_bits(value[:, (2 * c + 1) * LANES:(2 * c + 2) * LANES])
        slab_ref[pl.ds(c, rows, stride=words), :] = lo | lax.shift_left(hi, jnp.uint32(16))


def _from_packed_slabs(slab_ref, base, rows, words, pitch):
    out = []
    for c in range(words):
        w = slab_ref[pl.ds(base + c, rows, stride=pitch), :]
        out.append(pltpu.bitcast(lax.shift_left(w, jnp.uint32(16)), F32).astype(BF16))
        out.append(pltpu.bitcast(w & jnp.uint32(0xFFFF0000), F32).astype(BF16))
    return jnp.concatenate(out, axis=1)


def _outproj_kernel(sb_ref, ret_ref, x_ref, w_ref, g_ref, b_ref, wr_ref, br_ref,
                    x1_ref, x1s_ref, meta_ref, cnt_ref, *, alpha, sb_width):
    tm = x_ref.shape[0]
    mix = jnp.dot(sb_ref[...], w_ref[:sb_width, :], preferred_element_type=F32)
    mix = mix + jnp.dot(ret_ref[...], w_ref[sb_width:, :], preferred_element_type=F32)
    x1 = _layer_norm(alpha * x_ref[...] + mix, g_ref[...], b_ref[...])
    x1_ref[...] = x1
    _to_packed_slabs(x1s_ref, tm, x1)

    logits = jnp.dot(x1.astype(BF16), wr_ref[...], preferred_element_type=F32) + br_ref[...]
    lane = lax.broadcasted_iota(jnp.int32, (tm, LANES), 1).astype(F32)
    vals = logits
    tops, idxs, hots = [], [], []
    for _ in range(TOP_K):
        m = jnp.max(vals, axis=-1, keepdims=True)
        idx = jnp.min(jnp.where(vals == m, lane, float(LANES)), axis=-1, keepdims=True)
        hot = lane == idx
        vals = jnp.where(hot, NEG_BIG * 2.0, vals)
        tops.append(m)
        idxs.append(idx)
        hots.append(hot)
    exps = [jnp.exp(m - tops[0]) for m in tops]
    denom = exps[0] + exps[1] + exps[2] + exps[3]
    gates = [e / denom for e in exps]

    @pl.when(pl.program_id(0) == 0)
    def _():
        cnt_ref[...] = jnp.zeros_like(cnt_ref)

    multi = jnp.zeros((tm, LANES), F32)
    for hot in hots:
        multi = multi + jnp.where(hot, 1.0, 0.0)
    r = lax.broadcasted_iota(jnp.int32, (tm, tm), 0)
    c = lax.broadcasted_iota(jnp.int32, (tm, tm), 1)
    strict_lower = jnp.where(c < r, 1.0, 0.0).astype(BF16)
    before = jnp.dot(strict_lower, multi.astype(BF16), preferred_element_type=F32) + cnt_ref[...]
    cnt_ref[...] = cnt_ref[...] + jnp.sum(multi, axis=0, keepdims=True)

    meta = jnp.zeros((tm, LANES), F32)
    for kk in range(TOP_K):
        rank = jnp.sum(jnp.where(hots[kk], before, 0.0), axis=-1, keepdims=True)
        meta = jnp.where(lane == kk, idxs[kk], meta)
        meta = jnp.where(lane == TOP_K + kk, gates[kk], meta)
        meta = jnp.where(lane == 2 * TOP_K + kk, rank, meta)
    meta_ref[...] = meta


def _outproj(sb, ret, xt, w_out_bf, ln_g, ln_b, w_router, b_router, alpha, tm=512):
    t, d = xt.shape
    tm = min(tm, t)
    sbw = sb.shape[1]
    wr = jnp.zeros((d, LANES), BF16).at[:, :N_EXPERTS].set(w_router.astype(BF16))
    br = jnp.full((1, LANES), NEG_BIG, F32).at[0, :N_EXPERTS].set(b_router.astype(F32))
    row = lambda w: pl.BlockSpec((tm, w), lambda i: (i, 0))
    const = lambda r, c: pl.BlockSpec((r, c), lambda i: (0, 0))
    return pl.pallas_call(
        functools.partial(_outproj_kernel, alpha=alpha, sb_width=sbw),
        grid=(t // tm,),
        in_specs=[row(sbw), row(ret.shape[1]), row(d), const(d, d), const(1, d), const(1, d),
                  const(d, LANES), const(1, LANES)],
        out_specs=[row(d), pl.BlockSpec((tm * (d // (2 * LANES)), LANES), lambda i: (i, 0)), row(LANES),
                   const(1, LANES)],
        out_shape=[jax.ShapeDtypeStruct((t, d), F32),
                   jax.ShapeDtypeStruct((t * (d // (2 * LANES)), LANES), jnp.uint32),
                   jax.ShapeDtypeStruct((t, LANES), F32),
                   jax.ShapeDtypeStruct((1, LANES), F32)],
        compiler_params=_cparams(("arbitrary",)),
        name="outproj_ln_router",
    )(sb, ret, xt, w_out_bf, ln_g.reshape(1, d).astype(F32), ln_b.reshape(1, d).astype(F32), wr, br)


DMA_UNROLL = 8
PAIRS_PER_TRIP = 4


def _wait_slabs(hbm, buf, sem, slot, count, chunks):
    def body(n, _):
        pltpu.make_async_copy(hbm.at[pl.ds(0, chunks), :], buf.at[slot, pl.ds(0, chunks), :],
                              sem.at[slot]).wait()
        return 0
    lax.fori_loop(0, count, body, 0, unroll=DMA_UNROLL)


def _moe_kernel(ie_ref, ij_ref, ist_ref, ins_ref, tail_ref, live_ref, dest_ref, x_hbm, wgu_hbm, wd_hbm, bg_ref,
                bu_ref, bd_ref, zeros_hbm, y_hbm, x_vmem, acc, wg_f, wu_f, wd_f, wgu_bf, wd_bf, stage, gbuf, tok_ref,
                sem_g, sem_out, sem_w, *, n_f):
    del ij_ref
    i = pl.program_id(0)
    j = pl.program_id(1)
    nsub = ins_ref[i]
    start = ist_ref[i]
    d = acc.shape[1]
    tf = wd_bf.shape[0]
    d_ff = n_f * tf
    y_pitch = d // LANES + SLAB_PAD
    block_rows = ROW_BLOCK * y_pitch

    def weight_copies(e, jj, slot):
        col_g = pl.ds(pl.multiple_of(jj * tf, tf), tf)
        col_u = pl.ds(pl.multiple_of(d_ff + jj * tf, tf), tf)
        copies = []
        for half in range(2):
            rows_k = pl.ds(half * (d // 2), d // 2)
            rows_f = pl.ds(pl.multiple_of(jj * tf + half * (tf // 2), tf // 2), tf // 2)
            dst_f = pl.ds(half * (tf // 2), tf // 2)
            copies += [
                pltpu.make_async_copy(wgu_hbm.at[e, rows_k, col_g], wg_f.at[slot, rows_k, :], sem_w.at[slot]),
                pltpu.make_async_copy(wgu_hbm.at[e, rows_k, col_u], wu_f.at[slot, rows_k, :], sem_w.at[slot]),
                pltpu.make_async_copy(wd_hbm.at[e, rows_f, :], wd_f.at[slot, dst_f, :], sem_w.at[slot]),
            ]
        return copies

    def start_weights(e, jj, slot):
        for n, c in enumerate(weight_copies(e, jj, slot)):
            c.start(priority=n % 2)

    step = i * n_f + j
    w_slot = step % 2

    @pl.when(step == 0)
    def _():
        start_weights(ie_ref[0], 0, 0)

    @pl.when(step + 1 < live_ref[0])
    def _():
        last = j == n_f - 1
        start_weights(ie_ref[jnp.where(last, i + 1, i)], jnp.where(last, 0, j + 1), 1 - w_slot)

    def rows_of(r):
        return pl.ds(pl.multiple_of(r * ROW_BLOCK, ROW_BLOCK), ROW_BLOCK)

    def slabs_of(block):
        return pl.ds(pl.multiple_of(block * block_rows, block_rows), block_rows)

    def for_each(lo, hi, fn):
        def body(r, _):
            fn(r)
            return 0
        lax.fori_loop(lo, hi, body, 0)

    @pl.when((i == 0) & (j == 0))
    def _():
        stage[...] = jnp.zeros(stage.shape, F32)

        def zero_copy(bk):
            return pltpu.make_async_copy(stage.at[0], y_hbm.at[slabs_of(bk), :], sem_out.at[0])

        n_blocks = y_hbm.shape[0] // block_rows
        for_each(tail_ref[0], n_blocks, lambda bk: zero_copy(bk).start())
        for_each(tail_ref[0], n_blocks, lambda bk: zero_copy(bk).wait())

    first_block = start // ROW_BLOCK
    words = d // (2 * LANES)

    pitch = words + 1

    def build_row_tokens():
        n_tok = dest_ref.shape[0] // TOP_K
        clear = pltpu.make_async_copy(zeros_hbm, tok_ref, sem_g)
        clear.start()
        clear.wait()

        def fill(m, _):
            for u in range(DMA_UNROLL):
                t = m * DMA_UNROLL + u
                for kk in range(TOP_K):
                    tok_ref[dest_ref[kk * n_tok + t]] = t
            return 0
        lax.fori_loop(0, n_tok // DMA_UNROLL, fill, 0)

    def start_gather(item):
        row0 = ist_ref[item]

        def body(m, _):
            for u in range(DMA_UNROLL):
                r = m * DMA_UNROLL + u
                src = pl.ds(pl.multiple_of(tok_ref[row0 + r] * words, words), words)
                pltpu.make_async_copy(x_hbm.at[src, :], gbuf.at[pl.ds(r * pitch, words), :], sem_g).start(
                    priority=u % 2)
            return 0
        lax.fori_loop(0, ins_ref[item] * (ROW_BLOCK // DMA_UNROLL), body, 0)

    def wait_gather(item):
        def body(m, _):
            for _u in range(DMA_UNROLL):
                pltpu.make_async_copy(x_hbm.at[pl.ds(0, words), :], gbuf.at[pl.ds(0, words), :], sem_g).wait()
            return 0
        lax.fori_loop(0, ins_ref[item] * (ROW_BLOCK // DMA_UNROLL), body, 0)

    def out_copy(r, slot):
        return pltpu.make_async_copy(stage.at[slot], y_hbm.at[slabs_of(first_block + r), :],
                                     sem_out.at[slot])

    def span(r, n_blocks):
        return pl.ds(pl.multiple_of(r * ROW_BLOCK, ROW_BLOCK), n_blocks * ROW_BLOCK)

    def hidden(rows):
        xb = x_vmem[rows, :]
        gate_up = jnp.dot(xb, wgu_bf[...], preferred_element_type=F32)
        gate = gate_up[:, :tf] + bg_ref[...]
        up = gate_up[:, tf:] + bu_ref[...]
        gate = jnp.minimum(gate, SWIGLU_LIMIT)
        up = jnp.clip(up, -SWIGLU_LIMIT, SWIGLU_LIMIT)
        act = (up + 1.0) * (gate * jax.nn.sigmoid(SWIGLU_ALPHA * gate))
        return jnp.dot(act.astype(BF16), wd_bf[...], preferred_element_type=F32)

    @pl.when(nsub > 0)
    def _():
        @pl.when(j == 0)
        def _():
            @pl.when(i == 0)
            def _():
                build_row_tokens()
                start_gather(0)
            wait_gather(i)
            bias_rows = jnp.broadcast_to(bd_ref[...], (ROW_BLOCK, d))

            def unpack(r):
                x_vmem[rows_of(r), :] = _from_packed_slabs(gbuf, r * (ROW_BLOCK * pitch), ROW_BLOCK, words, pitch)
                acc[rows_of(r), :] = bias_rows
            for_each(0, nsub, unpack)

            @pl.when(i + 1 < pl.num_programs(0))
            def _():
                start_gather(i + 1)

        for c in weight_copies(ie_ref[i], j, w_slot):
            c.wait()
        wgu_bf[:, :tf] = wg_f[w_slot].astype(BF16)
        wgu_bf[:, tf:] = wu_f[w_slot].astype(BF16)
        wd_bf[...] = wd_f[w_slot].astype(BF16)

        n_pairs = nsub // 2
        odd = nsub % 2 == 1

        def run_pairs(pair, group):
            def trip(q):
                for u in range(group):
                    pair(q * group + u)
            for_each(0, n_pairs // group, trip)
            for_each((n_pairs // group) * group, n_pairs, pair)

        @pl.when(j < n_f - 1)
        def _():
            def pair(p):
                acc[span(2 * p, 2), :] += hidden(span(2 * p, 2))
            run_pairs(pair, PAIRS_PER_TRIP)

            @pl.when(odd)
            def _():
                acc[span(nsub - 1, 1), :] += hidden(span(nsub - 1, 1))

        @pl.when(j == n_f - 1)
        def _():
            def write_out(r, slot, final, reuse):
                @pl.when(reuse)
                def _():
                    out_copy(r, slot).wait()
                _to_slabs(stage.at[slot], 0, ROW_BLOCK, final, y_pitch)
                out_copy(r, slot).start()

            def pair(p):
                final = acc[span(2 * p, 2), :] + hidden(span(2 * p, 2))
                write_out(2 * p, 0, final[:ROW_BLOCK], p > 0)
                write_out(2 * p + 1, 1, final[ROW_BLOCK:], p > 0)
            run_pairs(pair, 2)

            @pl.when(odd)
            def _():
                final = acc[span(nsub - 1, 1), :] + hidden(span(nsub - 1, 1))
                write_out(nsub - 1, 0, final, n_pairs > 0)

            out_copy(0, 0).wait()

            @pl.when(n_pairs > 0)
            def _():
                out_copy(0, 1).wait()


def _moe_ffn(item_e, item_j, item_start, item_nsub, tail_block, live_steps, dest_flat, x1_packed, n_rows,
             w_gate_up, b_gate_up, w_down, b_down, r_max, tf=512):
    n_e, d, two_f = w_gate_up.shape
    chunks = d // LANES
    words = d // (2 * LANES)
    d_ff = two_f // 2
    n_f = d_ff // tf
    n_items = item_e.shape[0]

    def jf(i, j, ij):
        return jnp.where(ij[i] < 0, j, ij[i])

    hbm = pl.BlockSpec(memory_space=pl.ANY)
    in_specs = [
        hbm, hbm, hbm,
        pl.BlockSpec((None, 1, tf), lambda i, j, ie, ij, ist, ins, tl, lv, tk: (ie[i], 0, jf(i, j, ij))),
        pl.BlockSpec((None, 1, tf), lambda i, j, ie, ij, ist, ins, tl, lv, tk: (ie[i], 0, n_f + jf(i, j, ij))),
        pl.BlockSpec((None, 1, d), lambda i, j, ie, ij, ist, ins, tl, lv, tk: (ie[i], 0, 0)),
        hbm,
    ]
    return pl.pallas_call(
        functools.partial(_moe_kernel, n_f=n_f),
        grid_spec=pltpu.PrefetchScalarGridSpec(
            num_scalar_prefetch=7,
            grid=(n_items, n_f),
            in_specs=in_specs,
            out_specs=pl.BlockSpec(memory_space=pl.ANY),
            scratch_shapes=[pltpu.VMEM((r_max, d), BF16), pltpu.VMEM((r_max, d), F32),
                            pltpu.VMEM((2, d, tf), F32), pltpu.VMEM((2, d, tf), F32),
                            pltpu.VMEM((2, tf, d), F32),
                            pltpu.VMEM((d, 2 * tf), BF16), pltpu.VMEM((tf, d), BF16),
                            pltpu.VMEM((2, ROW_BLOCK * (chunks + SLAB_PAD), LANES), F32),
                            pltpu.VMEM((r_max * (words + 1), LANES), jnp.uint32),
                            pltpu.SMEM((n_rows,), jnp.int32),
                            pltpu.SemaphoreType.DMA(()), pltpu.SemaphoreType.DMA((2,)),
                            pltpu.SemaphoreType.DMA((2,))],
        ),
        out_shape=jax.ShapeDtypeStruct((n_rows * (chunks + SLAB_PAD), LANES), F32),
        compiler_params=_cparams(("arbitrary", "arbitrary")),
        name="moe_ffn",
    )(item_e, item_j, item_start, item_nsub, tail_block, live_steps, dest_flat, x1_packed, w_gate_up, w_down,
      b_gate_up.reshape(n_e, 1, two_f), b_gate_up.reshape(n_e, 1, two_f), b_down.reshape(n_e, 1, d),
      jnp.zeros((n_rows,), jnp.int32))


def _combine_kernel(dest_ref, y_hbm, x1_ref, meta_ref, g_ref, b_ref, o_ref, buf, sem, *, alpha, tm, chunks):
    i = pl.program_id(0)
    per_step = tm * TOP_K
    n_tok = dest_ref.shape[0] // TOP_K
    pitch = chunks + 1

    def issue_tile(step, slot):
        def body(tt, _):
            for kk in range(TOP_K):
                src = pl.ds(dest_ref[kk * n_tok + step * tm + tt] * (chunks + SLAB_PAD), chunks)
                dst = pl.ds((kk * tm + tt) * pitch, chunks)
                pltpu.make_async_copy(y_hbm.at[src, :], buf.at[slot, dst, :], sem.at[slot]).start(
                    priority=kk % 2)
            return 0
        lax.fori_loop(0, tm, body, 0, unroll=DMA_UNROLL // TOP_K)

    @pl.when(i == 0)
    def _():
        issue_tile(0, 0)

    @pl.when(i + 1 < pl.num_programs(0))
    def _():
        issue_tile(i + 1, (i + 1) % 2)

    slot = i % 2
    _wait_slabs(y_hbm, buf, sem, slot, per_step, chunks)
    meta = meta_ref[...]
    y = jnp.zeros(x1_ref.shape, F32)
    for kk in range(TOP_K):
        rows = jnp.concatenate(
            [buf[slot, pl.ds(kk * tm * pitch + c, tm, stride=pitch), :] for c in range(chunks)], axis=1)
        y = y + meta[:, TOP_K + kk:TOP_K + kk + 1] * rows
    o_ref[...] = _layer_norm(alpha * x1_ref[...] + y, g_ref[...], b_ref[...])


def _combine(dest_flat, y_rows, x1, meta, ln_g, ln_b, alpha, tm=256):
    t, d = x1.shape
    tm = min(tm, t)
    chunks = d // LANES
    row = lambda w: pl.BlockSpec((tm, w), lambda i, dest: (i, 0))
    const = pl.BlockSpec((1, d), lambda i, dest: (0, 0))
    return pl.pallas_call(
        functools.partial(_combine_kernel, alpha=alpha, tm=tm, chunks=chunks),
        grid_spec=pltpu.PrefetchScalarGridSpec(
            num_scalar_prefetch=1,
            grid=(t // tm,),
            in_specs=[pl.BlockSpec(memory_space=pl.ANY), row(d), row(LANES), const, const],
            out_specs=row(d),
            scratch_shapes=[pltpu.VMEM((2, TOP_K * tm * (chunks + 1), LANES), F32),
                            pltpu.SemaphoreType.DMA((2,))],
        ),
        out_shape=jax.ShapeDtypeStruct((t, d), F32),
        compiler_params=_cparams(("arbitrary",)),
        name="combine_ln",
    )(dest_flat, y_rows, x1, meta, ln_g.reshape(1, d).astype(F32), ln_b.reshape(1, d).astype(F32))


def _routing_plan(meta, counts_f, t, r_max):
    meta_t = meta.T
    idx = meta_t[0:TOP_K].astype(jnp.int32)
    rank = meta_t[2 * TOP_K:3 * TOP_K].astype(jnp.int32)
    counts = counts_f[0, :N_EXPERTS].astype(jnp.int32)
    n128 = (counts + ROW_BLOCK - 1) // ROW_BLOCK
    padded = n128 * ROW_BLOCK
    pad_start = jnp.cumsum(padded) - padded
    experts = jnp.arange(N_EXPERTS, dtype=jnp.int32)[:, None, None]
    region = jnp.sum(jnp.where(idx[None] == experts, pad_start[:, None, None], 0), axis=0)
    dest = region + rank
    tk = t * TOP_K
    n_rows = (tk + ROW_BLOCK - 1) // ROW_BLOCK * ROW_BLOCK + N_EXPERTS * ROW_BLOCK

    subs = r_max // ROW_BLOCK
    n_items = N_EXPERTS + (n_rows // ROW_BLOCK - N_EXPERTS) // subs
    items_e = (n128 + subs - 1) // subs
    items_end = jnp.cumsum(items_e)
    total = items_end[-1]
    slot = jnp.arange(n_items, dtype=jnp.int32)
    live = slot < total
    s_eff = jnp.minimum(slot, total - 1)
    e = jnp.minimum(jnp.searchsorted(items_end, s_eff, side='right'), N_EXPERTS - 1).astype(jnp.int32)
    local = s_eff - (items_end[e] - items_e[e])
    item_start = (pad_start[e] + local * r_max).astype(jnp.int32)
    item_nsub = jnp.where(live, jnp.clip(n128[e] - local * subs, 0, subs), 0).astype(jnp.int32)
    tail_block = jnp.sum(n128).reshape(1).astype(jnp.int32)
    return dest.reshape(tk), n_rows, e, live, item_start, item_nsub, tail_block


def kernel(x, w_in, ret_gn_gain, w_out, ln1_gain, ln1_bias, w_router, b_router, w_gate_up, b_gate_up,
           w_down, b_down, ln2_gain, ln2_bias):
    b, s, d = x.shape
    t = b * s
    depth = w_in.shape[0]
    alpha = (2 * depth) ** 0.25
    r_max = 1152
    tf = 512
    n_f = (w_gate_up.shape[-1] // 2) // tf
    xt = x.reshape(t, d)
    for layer in range(depth):
        proj = _in_proj(xt, w_in[layer].astype(BF16))
        sb = _stickbreak(proj, b, s)
        ret = _retention(proj, ret_gn_gain[layer], b, s)
        x1, x1_slabs, meta, counts = _outproj(sb, ret, xt, w_out[layer].astype(BF16), ln1_gain[layer],
                                              ln1_bias[layer], w_router[layer], b_router[layer], alpha)
        dest_flat, n_rows, item_e, live, item_start, item_nsub, tail_block = _routing_plan(
            meta, counts, t, r_max)
        item_j = jnp.where(live, -1, n_f - 1).astype(jnp.int32)
        live_steps = (jnp.sum(live.astype(jnp.int32)) * n_f).reshape(1)
        y_rows = _moe_ffn(item_e, item_j, item_start, item_nsub, tail_block, live_steps, dest_flat, x1_slabs,
                          n_rows, w_gate_up[layer], b_gate_up[layer], w_down[layer], b_down[layer], r_max, tf)
        xt = _combine(dest_flat, y_rows, x1, meta, ln2_gain[layer], ln2_bias[layer], alpha)
    return xt.reshape(b, s, d)
```
